```python
import math
import jax, jax.numpy as jnp
from jax import lax
import numpy as np

D_MODEL = 1024
BATCH = 4
SEQ = 4096
DEPTH = 4

N_MIXERS = 3
Q_BLOCK = 128
ROPE_THETA = 10000.0
EPS = 1e-6
MAX_POS_OFFSET = 1024

SB_HEADS = 16
SB_HEAD_DIM = D_MODEL // SB_HEADS

MLA_HEADS = 16
MLA_Q_RANK = 384
MLA_KV_RANK = 256
MLA_NOPE = 64
MLA_ROPE = 32
MLA_V = 64

DIFF_HEADS = 8
DIFF_HEAD_DIM = D_MODEL // (2 * DIFF_HEADS)

N_GROUPS = 4
EXPERTS_PER_GROUP = 4
N_EXPERTS = N_GROUPS * EXPERTS_PER_GROUP
EXPERT_TOPK = 2
EXPERT_FF = 512

kernel_name = 'hybrid_sb_mla_diff_hmoe_adaln'


def rmsnorm(x, g):
    xf = x.astype(jnp.float32)
    y = xf * lax.rsqrt(jnp.mean(xf * xf, axis=-1, keepdims=True) + EPS)
    return (y * g.astype(jnp.float32)).astype(x.dtype)


def rope_tables(positions, dim):
    inv_freq = ROPE_THETA ** (-jnp.arange(0, dim, 2, dtype=jnp.float32) / dim)
    ang = positions.astype(jnp.float32)[..., None] * inv_freq
    return jnp.cos(ang), jnp.sin(ang)


def apply_rope(t, cos, sin):
    c = cos[:, None].astype(t.dtype)
    s = sin[:, None].astype(t.dtype)
    t1, t2 = jnp.split(t, 2, axis=-1)
    return jnp.concatenate([t1 * c - t2 * s, t2 * c + t1 * s], axis=-1)


def split_heads(t, n):
    b, s, _ = t.shape
    return t.reshape(b, s, n, -1).transpose(0, 2, 1, 3)


def merge_heads(t):
    b, h, s, d = t.shape
    return t.transpose(0, 2, 1, 3).reshape(b, s, h * d)


def block_mask(start, seq, strict):
    q_pos = start + jnp.arange(Q_BLOCK, dtype=jnp.int32)
    k_pos = jnp.arange(seq, dtype=jnp.int32)
    if strict:
        return k_pos[None, :] < q_pos[:, None]
    return k_pos[None, :] <= q_pos[:, None]


def sweep_query_blocks(fn, qs):
    b, _, s, _ = qs[0].shape
    nb = s // Q_BLOCK
    blocks = tuple(q.reshape(q.shape[0], q.shape[1], nb, Q_BLOCK, q.shape[3]).transpose(2, 0, 1, 3, 4) for q in qs)
    starts = jnp.arange(nb, dtype=jnp.int32) * Q_BLOCK
    out = lax.map(lambda a: fn(a[0], *a[1]), (starts, blocks))
    return out.transpose(1, 2, 0, 3, 4).reshape(b, out.shape[2], s, out.shape[-1])


def stick_breaking_attention(h, w_qkv, w_o):
    q, k, v = jnp.split(h @ w_qkv, 3, axis=-1)
    q, k, v = (split_heads(t, SB_HEADS) for t in (q, k, v))
    seq = k.shape[2]
    scale = SB_HEAD_DIM ** -0.5

    def blk(start, qb):
        z = jnp.einsum('bhqd,bhkd->bhqk', qb, k).astype(jnp.float32) * scale
        mask = block_mask(start, seq, True)
        log_beta = jax.nn.log_sigmoid(z)
        log_one_minus = jnp.where(mask, jax.nn.log_sigmoid(-z), 0.0)
        log_stick = lax.cumsum(log_one_minus, axis=3, reverse=True) - log_one_minus
        a = jnp.where(mask, jnp.exp(log_beta + log_stick), 0.0)
        return jnp.einsum('bhqk,bhkd->bhqd', a.astype(v.dtype), v)

    return merge_heads(sweep_query_blocks(blk, (q,))) @ w_o


def latent_attention(h, cos, sin, w_in, g_q, g_kv, w_q_up, w_kv_up, w_o):
    lat = h @ w_in
    c_q = rmsnorm(lat[..., :MLA_Q_RANK], g_q)
    c_kv = rmsnorm(lat[..., MLA_Q_RANK:MLA_Q_RANK + MLA_KV_RANK], g_kv)
    k_rope = apply_rope(lat[:, None, :, MLA_Q_RANK + MLA_KV_RANK:], cos, sin)[:, 0]
    q = split_heads(c_q @ w_q_up, MLA_HEADS)
    q_nope = q[..., :MLA_NOPE]
    q_rope = apply_rope(q[..., MLA_NOPE:], cos, sin)
    kv = split_heads(c_kv @ w_kv_up, MLA_HEADS)
    k_nope, v = kv[..., :MLA_NOPE], kv[..., MLA_NOPE:]
    seq = k_nope.shape[2]
    scale = (MLA_NOPE + MLA_ROPE) ** -0.5

    def blk(start, qn, qr):
        s = (jnp.einsum('bhqd,bhkd->bhqk', qn, k_nope)
             + jnp.einsum('bhqr,bkr->bhqk', qr, k_rope)).astype(jnp.float32) * scale
        s = jnp.where(block_mask(start, seq, False), s, -jnp.inf)
        p = jax.nn.softmax(s, axis=-1)
        return jnp.einsum('bhqk,bhkd->bhqd', p.astype(v.dtype), v)

    return merge_heads(sweep_query_blocks(blk, (q_nope, q_rope))) @ w_o


def differential_attention(h, cos, sin, w_qkv, lam_q1, lam_k1, lam_q2, lam_k2, g_sub, w_o, lambda_init):
    q, k, v = jnp.split(h @ w_qkv, 3, axis=-1)
    q = apply_rope(split_heads(q, 2 * DIFF_HEADS), cos, sin)
    k = apply_rope(split_heads(k, 2 * DIFF_HEADS), cos, sin)
    v = split_heads(v, DIFF_HEADS)
    f32 = jnp.float32
    lam = (jnp.exp(jnp.sum(lam_q1.astype(f32) * lam_k1.astype(f32)))
           - jnp.exp(jnp.sum(lam_q2.astype(f32) * lam_k2.astype(f32))) + lambda_init)
    b, seq = k.shape[0], k.shape[2]
    scale = DIFF_HEAD_DIM ** -0.5

    def blk(start, qb):
        s = jnp.einsum('bhqd,bhkd->bhqk', qb, k).astype(f32) * scale
        s = jnp.where(block_mask(start, seq, False), s, -jnp.inf)
        p = jax.nn.softmax(s, axis=-1).reshape(b, DIFF_HEADS, 2, Q_BLOCK, seq)
        a = p[:, :, 0] - lam * p[:, :, 1]
        return jnp.einsum('bhqk,bhkd->bhqd', a.astype(v.dtype), v)

    o = sweep_query_blocks(blk, (q,))
    o = rmsnorm(o, g_sub) * (1.0 - lambda_init)
    return merge_heads(o) @ w_o


def hierarchical_moe(h, w_group, b_group, w_router, b_router, w_gate_up, w_down):
    f32 = jnp.float32
    g_logits = (h @ w_group).astype(f32) + b_group.astype(f32)
    g_prob = jax.nn.softmax(g_logits, axis=-1)
    g_w, g_idx = lax.top_k(g_prob, 1)
    e_logits = ((h @ w_router).astype(f32) + b_router.astype(f32)).reshape(
        h.shape[0], h.shape[1], N_GROUPS, EXPERTS_PER_GROUP)
    g_onehot = jax.nn.one_hot(g_idx[..., 0], N_GROUPS, dtype=f32)
    e_in_group = jnp.sum(e_logits * g_onehot[..., None], axis=-2)
    top_logit, top_local = lax.top_k(e_in_group, EXPERT_TOPK)
    top_w = jax.nn.softmax(top_logit, axis=-1) * g_w
    expert_id = g_idx * EXPERTS_PER_GROUP + top_local
    gates = jnp.sum(jax.nn.one_hot(expert_id, N_EXPERTS, dtype=f32) * top_w[..., None], axis=-2)
    out = jnp.zeros_like(h)
    for e in range(N_EXPERTS):
        gt, up = jnp.split(h @ w_gate_up[e], 2, axis=-1)
        out = out + gates[..., e:e + 1].astype(h.dtype) * ((jax.nn.silu(gt) * up) @ w_down[e])
    return out


def diff_lambda_init(layer):
    return 0.8 - 0.6 * math.exp(-0.3 * layer)


def setup_inputs(seed: int = 0) -> dict:
    key = jax.random.key(seed)
    ks = iter(jax.random.split(key, 40))

    def nrm(shape, scale):
        return jax.random.normal(next(ks), shape, jnp.float32) * scale

    D = D_MODEL
    n_sb = len(range(0, DEPTH, N_MIXERS))
    n_mla = len(range(1, DEPTH, N_MIXERS))
    n_diff = len(range(2, DEPTH, N_MIXERS))
    x = nrm((BATCH, SEQ, D), 1.0)
    c = nrm((BATCH, D), 1.0)
    positions = jnp.arange(SEQ, dtype=jnp.int32)[None, :] + jax.random.randint(
        next(ks), (BATCH, 1), 0, MAX_POS_OFFSET, dtype=jnp.int32)
    gate_offset = jnp.zeros((6 * D,), jnp.float32).at[2 * D:3 * D].set(1.0).at[5 * D:].set(1.0)
    lat_w = MLA_Q_RANK + MLA_KV_RANK + MLA_ROPE
    return {
        'x': x,
        'c': c,
        'positions': positions,
        'norm_mix_g': 1.0 + nrm((DEPTH, D), 0.1),
        'norm_ffn_g': 1.0 + nrm((DEPTH, D), 0.1),
        'ada_w': nrm((DEPTH, D, 6 * D), 0.1 * D ** -0.5),
        'ada_b': nrm((DEPTH, 6 * D), 0.02) + gate_offset,
        'sb_w_qkv': nrm((n_sb, D, 3 * D), D ** -0.5),
        'sb_w_o': nrm((n_sb, D, D), D ** -0.5),
        'mla_w_in': nrm((n_mla, D, lat_w), D ** -0.5),
        'mla_g_q': 1.0 + nrm((n_mla, MLA_Q_RANK), 0.1),
        'mla_g_kv': 1.0 + nrm((n_mla, MLA_KV_RANK), 0.1),
        'mla_w_q_up': nrm((n_mla, MLA_Q_RANK, MLA_HEADS * (MLA_NOPE + MLA_ROPE)), MLA_Q_RANK ** -0.5),
        'mla_w_kv_up': nrm((n_mla, MLA_KV_RANK, MLA_HEADS * (MLA_NOPE + MLA_V)), MLA_KV_RANK ** -0.5),
        'mla_w_o': nrm((n_mla, MLA_HEADS * MLA_V, D), (MLA_HEADS * MLA_V) ** -0.5),
        'diff_w_qkv': nrm((n_diff, D, 3 * D), D ** -0.5),
        'diff_lam_q1': nrm((n_diff, DIFF_HEAD_DIM), 0.1),
        'diff_lam_k1': nrm((n_diff, DIFF_HEAD_DIM), 0.1),
        'diff_lam_q2': nrm((n_diff, DIFF_HEAD_DIM), 0.1),
        'diff_lam_k2': nrm((n_diff, DIFF_HEAD_DIM), 0.1),
        'diff_g_sub': 1.0 + nrm((n_diff, 2 * DIFF_HEAD_DIM), 0.1),
        'diff_w_o': nrm((n_diff, D, D), D ** -0.5),
        'moe_w_group': nrm((DEPTH, D, N_GROUPS), D ** -0.5),
        'moe_b_group': nrm((DEPTH, N_GROUPS), 0.01),
        'moe_w_router': nrm((DEPTH, D, N_EXPERTS), D ** -0.5),
        'moe_b_router': nrm((DEPTH, N_EXPERTS), 0.01),
        'moe_w_gate_up': nrm((DEPTH, N_EXPERTS, D, 2 * EXPERT_FF), D ** -0.5),
        'moe_w_down': nrm((DEPTH, N_EXPERTS, EXPERT_FF, D), EXPERT_FF ** -0.5),
        'final_g': 1.0 + nrm((D,), 0.1),
    }


def reference(x, c, positions, norm_mix_g, norm_ffn_g, ada_w, ada_b,
              sb_w_qkv, sb_w_o,
              mla_w_in, mla_g_q, mla_g_kv, mla_w_q_up, mla_w_kv_up, mla_w_o,
              diff_w_qkv, diff_lam_q1, diff_lam_k1, diff_lam_q2, diff_lam_k2, diff_g_sub, diff_w_o,
              moe_w_group, moe_b_group, moe_w_router, moe_b_router, moe_w_gate_up, moe_w_down,
              final_g):
    cos_mla, sin_mla = rope_tables(positions, MLA_ROPE)
    cos_diff, sin_diff = rope_tables(positions, DIFF_HEAD_DIM)
    c_act = jax.nn.silu(c)
    for i in range(DEPTH):
        mod = (c_act @ ada_w[i] + ada_b[i])[:, None, :]
        sh_m, sc_m, gt_m, sh_f, sc_f, gt_f = jnp.split(mod, 6, axis=-1)
        h = rmsnorm(x, norm_mix_g[i]) * (1.0 + sc_m) + sh_m
        kind, j = i % N_MIXERS, i // N_MIXERS
        if kind == 0:
            y = stick_breaking_attention(h, sb_w_qkv[j], sb_w_o[j])
        elif kind == 1:
            y = latent_attention(h, cos_mla, sin_mla, mla_w_in[j], mla_g_q[j], mla_g_kv[j],
                                 mla_w_q_up[j], mla_w_kv_up[j], mla_w_o[j])
        else:
            y = differential_attention(h, cos_diff, sin_diff, diff_w_qkv[j], diff_lam_q1[j], diff_lam_k1[j],
                                       diff_lam_q2[j], diff_lam_k2[j], diff_g_sub[j], diff_w_o[j],
                                       diff_lambda_init(i))
        x = x + gt_m * y
        h = rmsnorm(x, norm_ffn_g[i]) * (1.0 + sc_f) + sh_f
        x = x + gt_f * hierarchical_moe(h, moe_w_group[i], moe_b_group[i], moe_w_router[i], moe_b_router[i],
                                        moe_w_gate_up[i], moe_w_down[i])
    return rmsnorm(x, final_g)
```

```python
import functools
import math

import jax
import jax.numpy as jnp
from jax import lax
from jax.experimental import pallas as pl
from jax.experimental.pallas import tpu as pltpu

F32 = jnp.float32
BF16 = jnp.bfloat16

N_MIXERS = 3
ROPE_THETA = 10000.0
EPS = 1e-6
HEAD_DIM = 64
SB_HEADS = 16
MLA_HEADS = 16
MLA_Q_RANK = 384
MLA_KV_RANK = 256
MLA_NOPE = 64
MLA_ROPE = 32
MLA_V = 64
DIFF_HEADS = 8
N_GROUPS = 4
EXPERTS_PER_GROUP = 4
N_EXPERTS = N_GROUPS * EXPERTS_PER_GROUP
EXPERT_FF = 512

LANES = 128
LOG2E = 1.4426950408889634
NEG_BIG = -1e30
VMEM_LIMIT = 56 * 1024 * 1024

ROW_TILE = 512
ATT_TILE = 256
MOE_TILE = 1024


def _cparams(n_axes):
    return pltpu.CompilerParams(dimension_semantics=("arbitrary",) * n_axes, vmem_limit_bytes=VMEM_LIMIT)


def _dot(a, b):
    return jnp.dot(a, b, preferred_element_type=F32)


def _dot_nt(a, b):
    return lax.dot_general(a, b, (((1,), (1,)), ((), ())), preferred_element_type=F32)


def _rms(x):
    return x * lax.rsqrt(jnp.mean(x * x, axis=-1, keepdims=True) + EPS)


def _rope_blocks(y, cos, sin_signed):
    out = []
    for j in range(y.shape[1] // LANES):
        yb = y[:, j * LANES:(j + 1) * LANES]
        out.append(yb * cos + pltpu.roll(yb, 64, 1) * sin_signed)
    return out[0] if len(out) == 1 else jnp.concatenate(out, axis=1)


def _adaln_kernel(c_ref, w_ref, b_ref, o_ref):
    c = c_ref[...]
    ca = c / (1.0 + jnp.exp(-c))
    o_ref[0] = jnp.dot(ca, w_ref[0], preferred_element_type=F32, precision=lax.Precision.HIGHEST) + b_ref[0]


def _adaln(c, ada_w, ada_b):
    depth, d, n = ada_w.shape
    bsz = c.shape[0]
    rows = 8
    cp = jnp.zeros((rows, d), F32).at[:bsz].set(c)
    tn = 1536
    out = pl.pallas_call(
        _adaln_kernel,
        grid=(depth, n // tn),
        in_specs=[
            pl.BlockSpec((rows, d), lambda i, j: (0, 0)),
            pl.BlockSpec((1, d, tn), lambda i, j: (i, 0, j)),
            pl.BlockSpec((1, 1, tn), lambda i, j: (i, 0, j)),
        ],
        out_specs=pl.BlockSpec((1, rows, tn), lambda i, j: (i, 0, j)),
        out_shape=jax.ShapeDtypeStruct((depth, rows, n), F32),
        compiler_params=_cparams(2),
        name="adaln_mod",
    )(cp, ada_w, ada_b.reshape(depth, 1, n))
    return out[:, :bsz]


def _rope_table_kernel(pos_ref, invf_ref, sign_ref, cos_ref, sin_ref):
    ang = pos_ref[...].astype(F32) * invf_ref[...]
    cos_ref[...] = jnp.cos(ang)
    sin_ref[...] = jnp.sin(ang) * sign_ref[...]


def _rope_tables(positions, dim):
    t = positions.size
    half = dim // 2
    inv_freq = ROPE_THETA ** (-jnp.arange(0, dim, 2, dtype=F32) / dim)
    invf = jnp.tile(inv_freq, LANES // half).reshape(1, LANES)
    sign = jnp.where(jnp.arange(LANES) < 64, -1.0, 1.0).astype(F32).reshape(1, LANES)
    tm = 2048
    return pl.pallas_call(
        _rope_table_kernel,
        grid=(t // tm,),
        in_specs=[
            pl.BlockSpec((tm, 1), lambda i: (i, 0)),
            pl.BlockSpec((1, LANES), lambda i: (0, 0)),
            pl.BlockSpec((1, LANES), lambda i: (0, 0)),
        ],
        out_specs=[pl.BlockSpec((tm, LANES), lambda i: (i, 0))] * 2,
        out_shape=[jax.ShapeDtypeStruct((t, LANES), F32)] * 2,
        compiler_params=_cparams(1),
        name="rope_tables",
    )(positions.reshape(t, 1), invf, sign)


def _modulated_norm(x_ref, g_ref, sc_ref, sh_ref):
    h = _rms(x_ref[...]) * g_ref[...]
    return h * (1.0 + sc_ref[0]) + sh_ref[0]


def _qkv_kernel(*refs, n_rope, tn):
    if n_rope:
        x_ref, g_ref, sc_ref, sh_ref, w_ref, cos_ref, sin_ref, o_ref = refs
    else:
        x_ref, g_ref, sc_ref, sh_ref, w_ref, o_ref = refs
    hb = _modulated_norm(x_ref, g_ref, sc_ref, sh_ref).astype(BF16)
    n = w_ref.shape[1]
    for j in range(n // tn):
        y = _dot(hb, w_ref[:, j * tn:(j + 1) * tn])
        if j * tn < n_rope:
            y = _rope_blocks(y, cos_ref[...], sin_ref[...])
        o_ref[:, j * tn:(j + 1) * tn] = y.astype(BF16)


def _qkv_proj(x, g, sc, sh, w, seq, rope=None, n_rope=0):
    t, d = x.shape
    n = w.shape[1]
    tm = ROW_TILE
    per_batch = seq // tm
    row = pl.BlockSpec((tm, d), lambda i: (i, 0))
    vec = pl.BlockSpec((1, d), lambda i: (0, 0))
    mod = pl.BlockSpec((1, 1, d), lambda i: (i // per_batch, 0, 0))
    in_specs = [row, vec, mod, mod, pl.BlockSpec((d, n), lambda i: (0, 0))]
    args = [x, g.reshape(1, d), sc, sh, w]
    if n_rope:
        in_specs += [pl.BlockSpec((tm, LANES), lambda i: (i, 0))] * 2
        args += list(rope)
    return pl.pallas_call(
        functools.partial(_qkv_kernel, n_rope=n_rope, tn=512),
        grid=(t // tm,),
        in_specs=in_specs,
        out_specs=pl.BlockSpec((tm, n), lambda i: (i, 0)),
        out_shape=jax.ShapeDtypeStruct((t, n), BF16),
        compiler_params=_cparams(1),
        name="qkv_proj",
    )(*args)


def _mla_proj_kernel(x_ref, g_ref, sc_ref, sh_ref, win_ref, gq_ref, gkv_ref, wq_ref, wkv_ref, cos_ref, sin_ref,
                     q_ref, kv_ref, kr_ref):
    hb = _modulated_norm(x_ref, g_ref, sc_ref, sh_ref).astype(BF16)
    lat = _dot(hb, win_ref[...])
    cq = (_rms(lat[:, :MLA_Q_RANK]) * gq_ref[...]).astype(BF16)
    ckv = (_rms(lat[:, MLA_Q_RANK:MLA_Q_RANK + MLA_KV_RANK]) * gkv_ref[...]).astype(BF16)
    cos = cos_ref[...]
    sin = sin_ref[...]
    kr_ref[...] = _rope_blocks(lat[:, MLA_Q_RANK + MLA_KV_RANK:], cos, sin).astype(BF16)
    n_nope = MLA_HEADS * MLA_NOPE
    q_ref[:, :n_nope] = _dot(cq, wq_ref[:, :n_nope]).astype(BF16)
    q_ref[:, n_nope:] = _rope_blocks(_dot(cq, wq_ref[:, n_nope:]), cos, sin).astype(BF16)
    kv_ref[...] = _dot(ckv, wkv_ref[...]).astype(BF16)


def _mla_proj(x, g, sc, sh, w_in, g_q, g_kv, w_q, w_kv, rope, seq):
    t, d = x.shape
    tm = ROW_TILE
    per_batch = seq // tm
    row = pl.BlockSpec((tm, d), lambda i: (i, 0))
    mod = pl.BlockSpec((1, 1, d), lambda i: (i // per_batch, 0, 0))

    def full(a):
        return pl.BlockSpec(a.shape, lambda i: (0,) * a.ndim)

    tab = pl.BlockSpec((tm, LANES), lambda i: (i, 0))
    g2, gq2, gkv2 = g.reshape(1, d), g_q.reshape(1, -1), g_kv.reshape(1, -1)
    nq, nkv = w_q.shape[1], w_kv.shape[1]
    return pl.pallas_call(
        _mla_proj_kernel,
        grid=(t // tm,),
        in_specs=[row, full(g2), mod, mod, full(w_in), full(gq2), full(gkv2), full(w_q), full(w_kv), tab, tab],
        out_specs=[pl.BlockSpec((tm, nq), lambda i: (i, 0)), pl.BlockSpec((tm, nkv), lambda i: (i, 0)),
                   pl.BlockSpec((tm, LANES), lambda i: (i, 0))],
        out_shape=[jax.ShapeDtypeStruct((t, nq), BF16), jax.ShapeDtypeStruct((t, nkv), BF16),
                   jax.ShapeDtypeStruct((t, LANES), BF16)],
        compiler_params=_cparams(1),
        name="mla_proj",
    )(x, g2, sc, sh, w_in, gq2, gkv2, w_q, w_kv, *rope)


def _lane_iota():
    return lax.broadcasted_iota(jnp.int32, (1, LANES), 1)


def _tile_masks(t, strict):
    row = lax.broadcasted_iota(jnp.int32, (t, t), 0)
    col = lax.broadcasted_iota(jnp.int32, (t, t), 1)
    return col < row if strict else col <= row


def _sb_flash_kernel(q_ref, k_ref, v_ref, u_ref, o_ref, acc_ref, carry_ref, *, scale):
    t = q_ref.shape[0]
    qi = pl.program_id(2)
    lane = _lane_iota()
    q2 = q_ref[...]
    zero = jnp.zeros_like(q2)
    q_heads = (jnp.where(lane < HEAD_DIM, q2, zero), jnp.where(lane >= HEAD_DIM, q2, zero))
    u = u_ref[...]
    acc_ref[...] = jnp.zeros_like(acc_ref)
    carry_ref[...] = jnp.zeros_like(carry_ref)

    def block(j, masked):
        start = pl.multiple_of(j * t, t)
        kb = k_ref[pl.ds(start, t), :]
        vb = v_ref[pl.ds(start, t), :]
        tri = _tile_masks(t, True) if masked else None
        for hd in range(2):
            z = _dot_nt(q_heads[hd], kb) * scale
            sp = jnp.maximum(z, 0.0) + jnp.log(1.0 + jnp.exp(-jnp.abs(z)))
            if masked:
                sp = jnp.where(tri, sp, 0.0)
            hi = sp.astype(BF16)
            lo = (sp - hi.astype(F32)).astype(BF16)
            cs = _dot(hi, u) + _dot(lo, u)
            carry = carry_ref[hd]
            w = jnp.exp(z - cs - jnp.concatenate([carry] * (t // LANES), axis=1))
            if masked:
                w = jnp.where(tri, w, 0.0)
            acc_ref[hd] += _dot(w.astype(BF16), vb)
            carry_ref[hd] = carry + jnp.broadcast_to(cs[:, 0:1], (t, LANES))

    block(qi, True)

    def body(it, c):
        block(qi - 1 - it, False)
        return c

    lax.fori_loop(0, qi, body, 0)
    o_ref[...] = jnp.where(lane < HEAD_DIM, acc_ref[0], acc_ref[1]).astype(o_ref.dtype)


def _softmax_block(hd, s, vb, m_ref, l_ref, acc_ref, mask):
    if mask is not None:
        s = jnp.where(mask, s, -jnp.inf)
    m_old = m_ref[hd]
    m_new = jnp.maximum(m_old, jnp.max(s, axis=1, keepdims=True))
    alpha = jnp.exp2(m_old - m_new)
    p = jnp.exp2(s - m_new)
    l_ref[hd] = alpha * l_ref[hd] + jnp.sum(p, axis=1, keepdims=True)
    acc_ref[hd] = alpha * acc_ref[hd] + _dot(p.astype(BF16), vb)
    m_ref[hd] = m_new


def _softmax_sweep(qi, t, q_heads, k_ref, v_ref, m_ref, l_ref, acc_ref, c_log2):
    m_ref[...] = jnp.full_like(m_ref, NEG_BIG)
    l_ref[...] = jnp.zeros_like(l_ref)
    acc_ref[...] = jnp.zeros_like(acc_ref)

    def block(j, masked):
        start = pl.multiple_of(j * t, t)
        kb = k_ref[pl.ds(start, t), :]
        vb = v_ref[pl.ds(start, t), :]
        mask = _tile_masks(t, False) if masked else None
        for hd in range(2):
            s = _dot_nt(q_heads[hd], kb) * c_log2
            _softmax_block(hd, s, vb, m_ref, l_ref, acc_ref, mask)

    def body(j, c):
        block(j, False)
        return c

    lax.fori_loop(0, qi, body, 0)
    block(qi, True)


def _mla_flash_kernel(qn_ref, qr_ref, kn_ref, kr_ref, v_ref, o_ref, kcat_ref, m_ref, l_ref, acc_ref, *, scale):
    t = qn_ref.shape[0]
    qi = pl.program_id(2)
    lane = _lane_iota()

    @pl.when(qi == 0)
    def _():
        kcat_ref[:, :LANES] = kn_ref[...]
        kcat_ref[:, LANES:] = kr_ref[...]

    qn = qn_ref[...]
    qr = qr_ref[...]
    zero = jnp.zeros_like(qn)
    half = MLA_ROPE // 2
    rope_a = (lane < half) | ((lane >= 64) & (lane < 64 + half))
    rope_b = ((lane >= half) & (lane < 2 * half)) | ((lane >= 64 + half) & (lane < 64 + 2 * half))
    q_heads = (
        jnp.concatenate([jnp.where(lane < MLA_NOPE, qn, zero), jnp.where(rope_a, qr, zero)], axis=1),
        jnp.concatenate([jnp.where(lane >= MLA_NOPE, qn, zero), jnp.where(rope_b, qr, zero)], axis=1),
    )
    _softmax_sweep(qi, t, q_heads, kcat_ref, v_ref, m_ref, l_ref, acc_ref, scale * LOG2E)
    o = jnp.where(lane < MLA_V, acc_ref[0] / l_ref[0], acc_ref[1] / l_ref[1])
    o_ref[...] = o.astype(o_ref.dtype)


def _diff_flash_kernel(q_ref, k_ref, v_ref, lam_ref, gsub_ref, o_ref, m_ref, l_ref, acc_ref, *, scale, lambda_init):
    t = q_ref.shape[0]
    qi = pl.program_id(2)
    lane = _lane_iota()
    q2 = q_ref[...]
    zero = jnp.zeros_like(q2)
    first = (lane & 63) < 32
    q_heads = (jnp.where(first, q2, zero), jnp.where(first, zero, q2))
    _softmax_sweep(qi, t, q_heads, k_ref, v_ref, m_ref, l_ref, acc_ref, scale * LOG2E)
    lp = lam_ref[...]
    lam = (jnp.exp(jnp.sum(lp[0:1] * lp[1:2], axis=1, keepdims=True))
           - jnp.exp(jnp.sum(lp[2:3] * lp[3:4], axis=1, keepdims=True)) + lambda_init)
    o = acc_ref[0] / l_ref[0] - lam * (acc_ref[1] / l_ref[1])
    o = _rms(o) * gsub_ref[...] * (1.0 - lambda_init)
    o_ref[...] = o.astype(o_ref.dtype)


def _att_scratch(t):
    return [pltpu.VMEM((2, t, 1), F32), pltpu.VMEM((2, t, 1), F32), pltpu.VMEM((2, t, LANES), F32)]


def _sb_attention(qkv, bsz, seq):
    t = ATT_TILE
    nq = seq // t
    pairs = SB_HEADS // 2
    u = (jnp.arange(t)[:, None] >= jnp.arange(t)[None, :]).astype(BF16)
    return pl.pallas_call(
        functools.partial(_sb_flash_kernel, scale=HEAD_DIM ** -0.5),
        grid=(bsz, pairs, nq),
        in_specs=[
            pl.BlockSpec((t, LANES), lambda b, p, i: (b * nq + i, p)),
            pl.BlockSpec((seq, LANES), lambda b, p, i: (b, pairs + p)),
            pl.BlockSpec((seq, LANES), lambda b, p, i: (b, 2 * pairs + p)),
            pl.BlockSpec((t, t), lambda b, p, i: (0, 0)),
        ],
        out_specs=pl.BlockSpec((t, LANES), lambda b, p, i: (b * nq + i, p)),
        out_shape=jax.ShapeDtypeStruct((bsz * seq, pairs * LANES), BF16),
        scratch_shapes=[pltpu.VMEM((2, t, LANES), F32), pltpu.VMEM((2, t, LANES), F32)],
        compiler_params=_cparams(3),
        name="sb_attention",
    )(qkv, qkv, qkv, u)


def _mla_attention(q, kv, kr, bsz, seq):
    t = ATT_TILE
    nq = seq // t
    pairs = MLA_HEADS // 2
    return pl.pallas_call(
        functools.partial(_mla_flash_kernel, scale=(MLA_NOPE + MLA_ROPE) ** -0.5),
        grid=(bsz, pairs, nq),
        in_specs=[
            pl.BlockSpec((t, LANES), lambda b, p, i: (b * nq + i, p)),
            pl.BlockSpec((t, LANES), lambda b, p, i: (b * nq + i, pairs + p)),
            pl.BlockSpec((seq, LANES), lambda b, p, i: (b, p)),
            pl.BlockSpec((seq, LANES), lambda b, p, i: (b, 0)),
            pl.BlockSpec((seq, LANES), lambda b, p, i: (b, pairs + p)),
        ],
        out_specs=pl.BlockSpec((t, LANES), lambda b, p, i: (b * nq + i, p)),
        out_shape=jax.ShapeDtypeStruct((bsz * seq, pairs * LANES), BF16),
        scratch_shapes=[pltpu.VMEM((seq, 2 * LANES), BF16)] + _att_scratch(t),
        compiler_params=_cparams(3),
        name="mla_attention",
    )(q, q, kv, kr, kv)


def _diff_attention(qkv, lam_params, g_sub, lambda_init, bsz, seq):
    t = ATT_TILE
    nq = seq // t
    return pl.pallas_call(
        functools.partial(_diff_flash_kernel, scale=HEAD_DIM ** -0.5, lambda_init=lambda_init),
        grid=(bsz, DIFF_HEADS, nq),
        in_specs=[
            pl.BlockSpec((t, LANES), lambda b, p, i: (b * nq + i, p)),
            pl.BlockSpec((seq, LANES), lambda b, p, i: (b, DIFF_HEADS + p)),
            pl.BlockSpec((seq, LANES), lambda b, p, i: (b, 2 * DIFF_HEADS + p)),
            pl.BlockSpec(lam_params.shape, lambda b, p, i: (0, 0)),
            pl.BlockSpec((1, LANES), lambda b, p, i: (0, 0)),
        ],
        out_specs=pl.BlockSpec((t, LANES), lambda b, p, i: (b * nq + i, p)),
        out_shape=jax.ShapeDtypeStruct((bsz * seq, DIFF_HEADS * LANES), BF16),
        scratch_shapes=_att_scratch(t),
        compiler_params=_cparams(3),
        name="diff_attention",
    )(qkv, qkv, qkv, lam_params, g_sub.reshape(1, LANES))


def _route(logits):
    lane = lax.broadcasted_iota(jnp.int32, logits.shape, 1).astype(F32)
    big = jnp.float32(1 << 20)

    def top1(vals):
        v = jnp.max(vals, axis=1, keepdims=True)
        i = jnp.min(jnp.where(vals == v, lane, big), axis=1, keepdims=True)
        return v, i

    is_group = (lane >= N_EXPERTS) & (lane < N_EXPERTS + N_GROUPS)
    gl = jnp.where(is_group, logits, -jnp.inf)
    gmax, gidx = top1(gl)
    g_w = 1.0 / jnp.sum(jnp.exp(gl - gmax), axis=1, keepdims=True)
    first = (gidx - N_EXPERTS) * EXPERTS_PER_GROUP
    el = jnp.where((lane >= first) & (lane < first + EXPERTS_PER_GROUP), logits, -jnp.inf)
    v1, i1 = top1(el)
    v2, i2 = top1(jnp.where(lane == i1, -jnp.inf, el))
    e2 = jnp.exp(v2 - v1)
    w1 = g_w / (1.0 + e2)
    w2 = g_w * e2 / (1.0 + e2)
    return jnp.where(lane == i1, w1, 0.0) + jnp.where(lane == i2, w2, 0.0)


def _oproj_kernel(a_ref, wo_ref, x_ref, gt_ref, g_ref, sc_ref, sh_ref, wr_ref, br_ref, xo_ref, h_ref, gates_ref):
    xn = x_ref[...] + gt_ref[0] * _dot(a_ref[...], wo_ref[...])
    xo_ref[...] = xn
    h = _rms(xn) * g_ref[...]
    h = h * (1.0 + sc_ref[0]) + sh_ref[0]
    h_ref[...] = h.astype(BF16)
    logits = jnp.dot(h, wr_ref[...], preferred_element_type=F32, precision=lax.Precision.HIGHEST) + br_ref[...]
    gates_ref[...] = _route(logits)


def _oproj(attn, w_o, x, gt, g, sc, sh, w_route, b_route, seq):
    t, d = x.shape
    tm = ROW_TILE
    per_batch = seq // tm
    row = pl.BlockSpec((tm, d), lambda i: (i, 0))
    mod = pl.BlockSpec((1, 1, d), lambda i: (i // per_batch, 0, 0))

    def full(a):
        return pl.BlockSpec(a.shape, lambda i: (0,) * a.ndim)

    g2 = g.reshape(1, d)
    return pl.pallas_call(
        _oproj_kernel,
        grid=(t // tm,),
        in_specs=[pl.BlockSpec((tm, attn.shape[1]), lambda i: (i, 0)), full(w_o), row, mod, full(g2), mod, mod,
                  full(w_route), full(b_route)],
        out_specs=[row, row, pl.BlockSpec((tm, LANES), lambda i: (i, 0))],
        out_shape=[jax.ShapeDtypeStruct((t, d), F32), jax.ShapeDtypeStruct((t, d), BF16),
                   jax.ShapeDtypeStruct((t, LANES), F32)],
        compiler_params=_cparams(1),
        name="oproj_router",
    )(attn, w_o, x, gt, g2, sc, sh, w_route, b_route)


def _moe_kernel(h_ref, gates_ref, wgu_ref, wd_ref, x_ref, gt_ref, o_ref, acc_ref):
    e = pl.program_id(1)

    @pl.when(e == 0)
    def _():
        acc_ref[...] = jnp.zeros_like(acc_ref)

    gu = _dot(h_ref[...], wgu_ref[0])
    gate, up = gu[:, :EXPERT_FF], gu[:, EXPERT_FF:]
    act = (gate / (1.0 + jnp.exp(-gate)) * up).astype(BF16)
    lane = lax.broadcasted_iota(jnp.int32, gates_ref.shape, 1)
    ge = jnp.sum(jnp.where(lane == e, gates_ref[...], 0.0), axis=1, keepdims=True)
    acc_ref[...] += ge * _dot(act, wd_ref[0])

    @pl.when(e == pl.num_programs(1) - 1)
    def _():
        o_ref[...] = x_ref[...] + gt_ref[0] * acc_ref[...]


def _moe(h, gates, w_gu, w_d, x, gt, seq):
    t, d = x.shape
    tm = MOE_TILE
    per_batch = seq // tm
    n_exp = w_gu.shape[0]
    row = lambda w: pl.BlockSpec((tm, w), lambda i, e: (i, 0))
    return pl.pallas_call(
        _moe_kernel,
        grid=(t // tm, n_exp),
        in_specs=[row(d), row(LANES),
                  pl.BlockSpec((1,) + w_gu.shape[1:], lambda i, e: (e, 0, 0)),
                  pl.BlockSpec((1,) + w_d.shape[1:], lambda i, e: (e, 0, 0)),
                  row(d), pl.BlockSpec((1, 1, d), lambda i, e: (i // per_batch, 0, 0))],
        out_specs=row(d),
        out_shape=jax.ShapeDtypeStruct((t, d), F32),
        scratch_shapes=[pltpu.VMEM((tm, d), F32)],
        compiler_params=_cparams(2),
        name="moe_experts",
    )(h, gates, w_gu, w_d, x, gt)


def _final_norm_kernel(x_ref, g_ref, o_ref):
    o_ref[...] = _rms(x_ref[...]) * g_ref[...]


def _final_norm(x, g):
    t, d = x.shape
    tm = 1024
    return pl.pallas_call(
        _final_norm_kernel,
        grid=(t // tm,),
        in_specs=[pl.BlockSpec((tm, d), lambda i: (i, 0)), pl.BlockSpec((1, d), lambda i: (0, 0))],
        out_specs=pl.BlockSpec((tm, d), lambda i: (i, 0)),
        out_shape=jax.ShapeDtypeStruct((t, d), F32),
        compiler_params=_cparams(1),
        name="final_norm",
    )(x, g.reshape(1, d))


def _diff_qk_layout(w):
    d = w.shape[0]
    w = w.reshape(d, DIFF_HEADS, 2, 2, HEAD_DIM // 2)
    return w.transpose(0, 1, 3, 2, 4).reshape(d, -1)


def _mla_layouts(w_in, w_q_up, w_kv_up):
    d = w_in.shape[0]
    half = MLA_ROPE // 2
    base = MLA_Q_RANK + MLA_KV_RANK
    r1, r2 = w_in[:, base:base + half], w_in[:, base + half:]
    z = jnp.zeros((d, 64 - 2 * half), w_in.dtype)
    w_in_l = jnp.concatenate([w_in[:, :base], r1, r1, z, r2, r2, z], axis=1)

    r = w_q_up.shape[0]
    wq = w_q_up.reshape(r, MLA_HEADS, MLA_NOPE + MLA_ROPE)
    nope = wq[:, :, :MLA_NOPE].reshape(r, -1)
    q1 = wq[:, :, MLA_NOPE:MLA_NOPE + half].reshape(r, MLA_HEADS // 2, 2 * half)
    q2 = wq[:, :, MLA_NOPE + half:].reshape(r, MLA_HEADS // 2, 2 * half)
    zq = jnp.zeros((r, MLA_HEADS // 2, 64 - 2 * half), w_q_up.dtype)
    rope = jnp.concatenate([q1, zq, q2, zq], axis=2).reshape(r, -1)
    w_q_l = jnp.concatenate([nope, rope], axis=1)

    rk = w_kv_up.shape[0]
    wkv = w_kv_up.reshape(rk, MLA_HEADS, MLA_NOPE + MLA_V)
    w_kv_l = jnp.concatenate([wkv[:, :, :MLA_NOPE].reshape(rk, -1), wkv[:, :, MLA_NOPE:].reshape(rk, -1)], axis=1)
    return w_in_l.astype(BF16), w_q_l.astype(BF16), w_kv_l.astype(BF16)


def _router_layout(w_group, b_group, w_router, b_router):
    d = w_group.shape[0]
    pad = LANES - N_EXPERTS - N_GROUPS
    w = jnp.concatenate([w_router, w_group, jnp.zeros((d, pad), F32)], axis=1)
    b = jnp.concatenate([b_router, b_group, jnp.zeros((pad,), F32)]).reshape(1, LANES)
    return w, b


def kernel(x, c, positions, norm_mix_g, norm_ffn_g, ada_w, ada_b, sb_w_qkv, sb_w_o, mla_w_in, mla_g_q, mla_g_kv, mla_w_q_up, mla_w_kv_up, mla_w_o, diff_w_qkv, diff_lam_q1, diff_lam_k1, diff_lam_q2, diff_lam_k2, diff_g_sub, diff_w_o, moe_w_group, moe_b_group, moe_w_router, moe_b_router, moe_w_gate_up, moe_w_down, final_g):
    bsz, seq, d = x.shape
    depth = ada_w.shape[0]
    xt = x.reshape(bsz * seq, d)
    mod = _adaln(c, ada_w, ada_b)
    rope_mla = _rope_tables(positions, MLA_ROPE) if depth > 1 else None
    rope_diff = _rope_tables(positions, HEAD_DIM) if depth > 2 else None

    for i in range(depth):
        sh_m, sc_m, gt_m, sh_f, sc_f, gt_f = (mod[i, :, k * d:(k + 1) * d].reshape(bsz, 1, d) for k in range(6))
        kind, j = i % N_MIXERS, i // N_MIXERS
        if kind == 0:
            qkv = _qkv_proj(xt, norm_mix_g[i], sc_m, sh_m, sb_w_qkv[j].astype(BF16), seq)
            attn = _sb_attention(qkv, bsz, seq)
            w_o = sb_w_o[j]
        elif kind == 1:
            w_in_l, w_q_l, w_kv_l = _mla_layouts(mla_w_in[j], mla_w_q_up[j], mla_w_kv_up[j])
            q, kv, kr = _mla_proj(xt, norm_mix_g[i], sc_m, sh_m, w_in_l, mla_g_q[j], mla_g_kv[j], w_q_l, w_kv_l,
                                  rope_mla, seq)
            attn = _mla_attention(q, kv, kr, bsz, seq)
            w_o = mla_w_o[j]
        else:
            w = diff_w_qkv[j]
            w = jnp.concatenate([_diff_qk_layout(w[:, :d]), _diff_qk_layout(w[:, d:2 * d]), w[:, 2 * d:]], axis=1)
            qkv = _qkv_proj(xt, norm_mix_g[i], sc_m, sh_m, w.astype(BF16), seq, rope=rope_diff, n_rope=2 * d)
            lam_params = jnp.stack([diff_lam_q1[j], diff_lam_k1[j], diff_lam_q2[j], diff_lam_k2[j]])
            lambda_init = 0.8 - 0.6 * math.exp(-0.3 * i)
            attn = _diff_attention(qkv, lam_params, diff_g_sub[j], lambda_init, bsz, seq)
            w_o = diff_w_o[j]
        w_route, b_route = _router_layout(moe_w_group[i], moe_b_group[i], moe_w_router[i], moe_b_router[i])
        xt, h, gates = _oproj(attn, w_o.astype(BF16), xt, gt_m, norm_ffn_g[i], sc_f, sh_f, w_route, b_route, seq)
        xt = _moe(h, gates, moe_w_gate_up[i].astype(BF16), moe_w_down[i].astype(BF16), xt, gt_f, seq)
    return _final_norm(xt, final_g).reshape(bsz, seq, d)
```

```python
import functools
import math

import jax
import jax.numpy as jnp
from jax import lax
from jax.experimental import pallas as pl
from jax.experimental.pallas import tpu as pltpu

F32 = jnp.float32
BF16 = jnp.bfloat16

N_MIXERS = 3
ROPE_THETA = 10000.0
EPS = 1e-6
HEAD_DIM = 64
SB_HEADS = 16
MLA_HEADS = 16
MLA_Q_RANK = 384
MLA_KV_RANK = 256
MLA_NOPE = 64
MLA_ROPE = 32
MLA_V = 64
DIFF_HEADS = 8
N_GROUPS = 4
EXPERTS_PER_GROUP = 4
N_EXPERTS = N_GROUPS * EXPERTS_PER_GROUP
EXPERT_FF = 512

LANES = 128
LOG2E = 1.4426950408889634
NEG_BIG = -1e30
VMEM_LIMIT = 56 * 1024 * 1024

ROW_TILE = 512
ATT_TILE = 256
MOE_TILE = 1024
PROJ_CHUNK = 256


def _cparams(n_axes):
    return pltpu.CompilerParams(dimension_semantics=("arbitrary",) * n_axes, vmem_limit_bytes=VMEM_LIMIT)


def _dot(a, b):
    return jnp.dot(a, b, preferred_element_type=F32)


def _dot_nt(a, b):
    return lax.dot_general(a, b, (((1,), (1,)), ((), ())), preferred_element_type=F32)


def _rms(x):
    return x * lax.rsqrt(jnp.mean(x * x, axis=-1, keepdims=True) + EPS)


def _rope_blocks(y, cos, sin_signed):
    out = []
    for j in range(y.shape[1] // LANES):
        yb = y[:, j * LANES:(j + 1) * LANES]
        out.append(yb * cos + pltpu.roll(yb, 64, 1) * sin_signed)
    return out[0] if len(out) == 1 else jnp.concatenate(out, axis=1)


def _rope_rows(y, cos_t, sin_t):
    out = []
    for j in range(y.shape[0] // LANES):
        yb = y[j * LANES:(j + 1) * LANES]
        out.append(yb * cos_t + jnp.concatenate([yb[64:], yb[:64]], axis=0) * sin_t)
    return out[0] if len(out) == 1 else jnp.concatenate(out, axis=0)


def _adaln_kernel(c_ref, w_ref, b_ref, o_ref):
    c = c_ref[...]
    ca = c / (1.0 + jnp.exp(-c))
    o_ref[0] = jnp.dot(ca, w_ref[0], preferred_element_type=F32, precision=lax.Precision.HIGHEST) + b_ref[0]


def _adaln(c, ada_w, ada_b):
    depth, d, n = ada_w.shape
    bsz = c.shape[0]
    rows = 8
    cp = jnp.zeros((rows, d), F32).at[:bsz].set(c)
    tn = 1536
    out = pl.pallas_call(
        _adaln_kernel,
        grid=(depth, n // tn),
        in_specs=[
            pl.BlockSpec((rows, d), lambda i, j: (0, 0)),
            pl.BlockSpec((1, d, tn), lambda i, j: (i, 0, j)),
            pl.BlockSpec((1, 1, tn), lambda i, j: (i, 0, j)),
        ],
        out_specs=pl.BlockSpec((1, rows, tn), lambda i, j: (i, 0, j)),
        out_shape=jax.ShapeDtypeStruct((depth, rows, n), F32),
        compiler_params=_cparams(2),
        name="adaln_mod",
    )(cp, ada_w, ada_b.reshape(depth, 1, n))
    return out[:, :bsz]


def _rope_table_kernel(pos_c_ref, pos_r_ref, invf_r_ref, sign_r_ref, invf_c_ref, sign_c_ref,
                       cos_ref, sin_ref, cost_ref, sint_ref):
    ang = pos_c_ref[...].astype(F32) * invf_r_ref[...]
    cos_ref[...] = jnp.cos(ang)
    sin_ref[...] = jnp.sin(ang) * sign_r_ref[...]
    ang_t = invf_c_ref[...] * pos_r_ref[...].astype(F32)
    cost_ref[...] = jnp.cos(ang_t)
    sint_ref[...] = jnp.sin(ang_t) * sign_c_ref[...]


def _rope_tables(positions, dim):
    t = positions.size
    half = dim // 2
    inv_freq = ROPE_THETA ** (-jnp.arange(0, dim, 2, dtype=F32) / dim)
    invf = jnp.tile(inv_freq, LANES // half)
    sign = jnp.where(jnp.arange(LANES) < 64, -1.0, 1.0).astype(F32)
    tm = 2048
    small_r = pl.BlockSpec((1, LANES), lambda i: (0, 0))
    small_c = pl.BlockSpec((LANES, 1), lambda i: (0, 0))
    return pl.pallas_call(
        _rope_table_kernel,
        grid=(t // tm,),
        in_specs=[pl.BlockSpec((tm, 1), lambda i: (i, 0)), pl.BlockSpec((1, tm), lambda i: (0, i)),
                  small_r, small_r, small_c, small_c],
        out_specs=[pl.BlockSpec((tm, LANES), lambda i: (i, 0))] * 2 + [pl.BlockSpec((LANES, tm), lambda i: (0, i))] * 2,
        out_shape=[jax.ShapeDtypeStruct((t, LANES), F32)] * 2 + [jax.ShapeDtypeStruct((LANES, t), F32)] * 2,
        compiler_params=_cparams(1),
        name="rope_tables",
    )(positions.reshape(t, 1), positions.reshape(1, t), invf.reshape(1, LANES), sign.reshape(1, LANES),
      invf.reshape(LANES, 1), sign.reshape(LANES, 1))


def _modulated_norm(x_ref, g_ref, sc_ref, sh_ref):
    h = _rms(x_ref[...]) * g_ref[...]
    return h * (1.0 + sc_ref[0]) + sh_ref[0]


def _store_token_blocks(o_ref, w_t_ref, hb, post=None):
    tb = o_ref.shape[2]
    for r in range(0, w_t_ref.shape[0], PROJ_CHUNK):
        y = _dot_nt(w_t_ref[r:r + PROJ_CHUNK, :], hb)
        if post is not None:
            y = post(y, r)
        y = y.astype(BF16)
        for c in range(o_ref.shape[0]):
            o_ref[c, r:r + PROJ_CHUNK, :] = y[:, c * tb:(c + 1) * tb]


def _qkv_kernel(*refs, rope, q_scale):
    if rope:
        x_ref, g_ref, sc_ref, sh_ref, wqt_ref, wk_ref, wvt_ref, cos_ref, sin_ref, cost_ref, sint_ref = refs[:11]
    else:
        x_ref, g_ref, sc_ref, sh_ref, wqt_ref, wk_ref, wvt_ref = refs[:7]
    q_ref, k_ref, v_ref = refs[-3:]
    hb = _modulated_norm(x_ref, g_ref, sc_ref, sh_ref).astype(BF16)

    def q_post(qt, r):
        if rope:
            qt = _rope_rows(qt, cost_ref[...], sint_ref[...])
        return qt * q_scale

    _store_token_blocks(q_ref, wqt_ref, hb, q_post)
    for r in range(0, wk_ref.shape[1], 512):
        k = _dot(hb, wk_ref[:, r:r + 512])
        if rope:
            k = _rope_blocks(k, cos_ref[...], sin_ref[...])
        k_ref[:, r:r + 512] = k.astype(BF16)
    _store_token_blocks(v_ref, wvt_ref, hb)


def _qkv_proj(x, g, sc, sh, wq_t, wk, wv_t, seq, rope=None):
    t, d = x.shape
    tm, tk = ROW_TILE, ATT_TILE
    per_batch = seq // tm
    row = pl.BlockSpec((tm, d), lambda i: (i, 0))
    mod = pl.BlockSpec((1, 1, d), lambda i: (i // per_batch, 0, 0))

    def full(a):
        return pl.BlockSpec(a.shape, lambda i: (0,) * a.ndim)

    g2 = g.reshape(1, d)
    in_specs = [row, full(g2), mod, mod, full(wq_t), full(wk), full(wv_t)]
    args = [x, g2, sc, sh, wq_t, wk, wv_t]
    if rope is not None:
        tab = pl.BlockSpec((tm, LANES), lambda i: (i, 0))
        tab_t = pl.BlockSpec((LANES, tm), lambda i: (0, i))
        in_specs += [tab, tab, tab_t, tab_t]
        args += list(rope)
    nq, nk, nv = wq_t.shape[0], wk.shape[1], wv_t.shape[0]
    return pl.pallas_call(
        functools.partial(_qkv_kernel, rope=rope is not None, q_scale=HEAD_DIM ** -0.5),
        grid=(t // tm,),
        in_specs=in_specs,
        out_specs=[pl.BlockSpec((tm // tk, nq, tk), lambda i: (i, 0, 0)), pl.BlockSpec((tm, nk), lambda i: (i, 0)),
                   pl.BlockSpec((tm // tk, nv, tk), lambda i: (i, 0, 0))],
        out_shape=[jax.ShapeDtypeStruct((t // tk, nq, tk), BF16), jax.ShapeDtypeStruct((t, nk), BF16),
                   jax.ShapeDtypeStruct((t // tk, nv, tk), BF16)],
        compiler_params=_cparams(1),
        name="qkv_proj",
    )(*args)


def _mla_proj_kernel(x_ref, g_ref, sc_ref, sh_ref, win_ref, gq_ref, gkv_ref, wqt_ref, wkn_ref, wvt_ref,
                     cos_ref, sin_ref, cost_ref, sint_ref, q_ref, kn_ref, kr_ref, v_ref):
    hb = _modulated_norm(x_ref, g_ref, sc_ref, sh_ref).astype(BF16)
    lat = _dot(hb, win_ref[...])
    cq = (_rms(lat[:, :MLA_Q_RANK]) * gq_ref[...]).astype(BF16)
    ckv = (_rms(lat[:, MLA_Q_RANK:MLA_Q_RANK + MLA_KV_RANK]) * gkv_ref[...]).astype(BF16)
    kr_ref[...] = _rope_blocks(lat[:, MLA_Q_RANK + MLA_KV_RANK:], cos_ref[...], sin_ref[...]).astype(BF16)
    n_nope = MLA_HEADS * MLA_NOPE
    _store_token_blocks(q_ref, wqt_ref, cq,
                        lambda qt, r: _rope_rows(qt, cost_ref[...], sint_ref[...]) if r >= n_nope else qt)
    kn_ref[...] = _dot(ckv, wkn_ref[...]).astype(BF16)
    _store_token_blocks(v_ref, wvt_ref, ckv)


def _mla_proj(x, g, sc, sh, w_in, g_q, g_kv, wq_t, wkn, wv_t, rope, seq):
    t, d = x.shape
    tm, tk = ROW_TILE, ATT_TILE
    per_batch = seq // tm
    row = pl.BlockSpec((tm, d), lambda i: (i, 0))
    mod = pl.BlockSpec((1, 1, d), lambda i: (i // per_batch, 0, 0))

    def full(a):
        return pl.BlockSpec(a.shape, lambda i: (0,) * a.ndim)

    tab = pl.BlockSpec((tm, LANES), lambda i: (i, 0))
    tab_t = pl.BlockSpec((LANES, tm), lambda i: (0, i))
    g2, gq2, gkv2 = g.reshape(1, d), g_q.reshape(1, -1), g_kv.reshape(1, -1)
    nq, nk, nv = wq_t.shape[0], wkn.shape[1], wv_t.shape[0]
    return pl.pallas_call(
        _mla_proj_kernel,
        grid=(t // tm,),
        in_specs=[row, full(g2), mod, mod, full(w_in), full(gq2), full(gkv2), full(wq_t), full(wkn), full(wv_t),
                  tab, tab, tab_t, tab_t],
        out_specs=[pl.BlockSpec((tm // tk, nq, tk), lambda i: (i, 0, 0)), pl.BlockSpec((tm, nk), lambda i: (i, 0)),
                   pl.BlockSpec((tm, LANES), lambda i: (i, 0)), pl.BlockSpec((tm // tk, nv, tk), lambda i: (i, 0, 0))],
        out_shape=[jax.ShapeDtypeStruct((t // tk, nq, tk), BF16), jax.ShapeDtypeStruct((t, nk), BF16),
                   jax.ShapeDtypeStruct((t, LANES), BF16), jax.ShapeDtypeStruct((t // tk, nv, tk), BF16)],
        compiler_params=_cparams(1),
        name="mla_proj",
    )(x, g2, sc, sh, w_in, gq2, gkv2, wq_t, wkn, wv_t, *rope)


def _keep_rows(x, keep):
    n_rows, n = x.shape
    parts, pos = [], 0
    for a, b in keep:
        if a > pos:
            parts.append(jnp.zeros((a - pos, n), x.dtype))
        parts.append(x[a:b])
        pos = b
    if pos < n_rows:
        parts.append(jnp.zeros((n_rows - pos, n), x.dtype))
    return jnp.concatenate(parts, axis=0)


def _causal_valid(t, strict):
    key = lax.broadcasted_iota(jnp.int32, (t, t), 0)
    query = lax.broadcasted_iota(jnp.int32, (t, t), 1)
    return key < query if strict else key <= query


def _rem3(step):
    return step % 3 if isinstance(step, int) else lax.rem(step, 3)


def _sb_flash_kernel(qt_ref, k_ref, vt_ref, lmat_ref, o_ref, z_scr, cs_scr, acc_ref, carry_ref):
    t = qt_ref.shape[1]
    qi = pl.program_id(2)
    qt = qt_ref[...]
    qt_heads = (_keep_rows(qt, [(0, HEAD_DIM)]), _keep_rows(qt, [(HEAD_DIM, 2 * HEAD_DIM)]))
    lmat = lmat_ref[...]

    acc_ref[...] = jnp.zeros_like(acc_ref)
    carry_ref[...] = jnp.zeros_like(carry_ref)
    valid = _causal_valid(t, True)

    def block_of(step):
        return jnp.maximum(qi - step, 0)

    def scores(step):
        kb = k_ref[pl.ds(pl.multiple_of(block_of(step) * t, t), t), :]
        for hd in range(2):
            z_scr[_rem3(step), hd] = _dot(kb, qt_heads[hd])

    def sums(step, masked=False):
        for hd in range(2):
            z = z_scr[_rem3(step), hd]
            sp = jnp.maximum(z, 0.0) + jnp.log(1.0 + jnp.exp2(jnp.abs(z) * -LOG2E))
            if masked:
                sp = jnp.where(valid, sp, 0.0)
            cs_scr[step & 1, hd] = _dot(lmat, sp.astype(BF16))

    def values(step, masked=False):
        vb = vt_ref[block_of(step)]
        for hd in range(2):
            cs = cs_scr[step & 1, hd]
            w = jnp.exp(z_scr[_rem3(step), hd] - cs - carry_ref[hd])
            if masked:
                w = jnp.where(valid, w, 0.0)
            acc_ref[hd] += _dot(vb, w.astype(BF16))
            carry_ref[hd] += cs[0:1, :]

    def iteration(i, first=False):
        if not first or i >= 2:
            values(i - 2, masked=first)
        if not first or i >= 1:
            sums(i - 1, masked=first and i == 1)
        scores(i)

    for i in range(3):
        iteration(i, first=True)

    def body(i, c):
        iteration(i)
        return c

    lax.fori_loop(3, qi + 3, body, 0)
    o_t = jnp.concatenate([acc_ref[0, :HEAD_DIM], acc_ref[1, HEAD_DIM:]], axis=0)
    o_ref[...] = o_t.T.astype(o_ref.dtype)


def _softmax_sweep(qi, t, qt_heads, k_ref, vt_ref, scr, c_log2):
    s_scr, p_scr, alpha_scr, m_ref, l_ref, acc_ref = scr
    n = qi + 1
    m_ref[...] = jnp.full_like(m_ref, NEG_BIG)
    l_ref[...] = jnp.zeros_like(l_ref)
    acc_ref[...] = jnp.zeros_like(acc_ref)
    p_scr[1] = jnp.zeros(p_scr.shape[1:], p_scr.dtype)
    alpha_scr[1] = jnp.ones(alpha_scr.shape[1:], alpha_scr.dtype)

    def scores(j):
        kb = k_ref[pl.ds(pl.multiple_of(j * t, t), t), :]
        for hd in range(2):
            s_scr[j & 1, hd] = _dot(kb, qt_heads[hd]) * c_log2

    rows = 32

    def softmax(j, masked):
        slot = j & 1
        for hd in range(2):
            def chunk(r):
                s = s_scr[slot, hd, r:r + rows, :]
                if masked:
                    key = lax.broadcasted_iota(jnp.int32, (rows, t), 0) + r
                    query = lax.broadcasted_iota(jnp.int32, (rows, t), 1)
                    s = jnp.where(key <= query, s, -jnp.inf)
                return s

            mx = chunk(0)
            for r in range(rows, t, rows):
                mx = jnp.maximum(mx, chunk(r))
            m_old = m_ref[hd]
            m_new = jnp.maximum(m_old, jnp.max(mx, axis=0, keepdims=True))
            alpha = jnp.exp2(m_old - m_new)
            psum = None
            for r in range(0, t, rows):
                p = jnp.exp2(chunk(r) - m_new)
                p_scr[slot, hd, r:r + rows, :] = p.astype(BF16)
                psum = p if psum is None else psum + p
            l_ref[hd] = alpha * l_ref[hd] + jnp.sum(psum, axis=0, keepdims=True)
            m_ref[hd] = m_new
            alpha_scr[slot, hd] = alpha

    def values(j):
        slot = j & 1
        vb = vt_ref[jnp.maximum(j, 0)]
        for hd in range(2):
            acc_ref[hd] = alpha_scr[slot, hd] * acc_ref[hd] + _dot(vb, p_scr[slot, hd])

    scores(0)

    def body(i, c):
        values(i - 1)
        softmax(i, False)
        scores(i + 1)
        return c

    lax.fori_loop(0, n - 1, body, 0)
    values(n - 2)
    softmax(n - 1, True)
    values(n - 1)


def _mla_flash_kernel(qn_ref, qr_ref, kn_ref, kr_ref, vt_ref, o_ref, kcat_ref, *scr, scale):
    l_ref, acc_ref = scr[-2:]
    t = qn_ref.shape[1]
    qi = pl.program_id(2)

    @pl.when(qi == 0)
    def _():
        kcat_ref[:, :LANES] = kn_ref[...]
        kcat_ref[:, LANES:] = kr_ref[...]

    qn = qn_ref[...]
    qr = qr_ref[...]
    half = MLA_ROPE // 2
    qt_heads = (
        jnp.concatenate([_keep_rows(qn, [(0, MLA_NOPE)]), _keep_rows(qr, [(0, half), (64, 64 + half)])], axis=0),
        jnp.concatenate([_keep_rows(qn, [(MLA_NOPE, 2 * MLA_NOPE)]),
                         _keep_rows(qr, [(half, 2 * half), (64 + half, 64 + 2 * half)])], axis=0),
    )
    _softmax_sweep(qi, t, qt_heads, kcat_ref, vt_ref, scr, scale * LOG2E)
    o_t = jnp.concatenate([acc_ref[0, :MLA_V] / l_ref[0], acc_ref[1, MLA_V:] / l_ref[1]], axis=0)
    o_ref[...] = o_t.T.astype(o_ref.dtype)


def _diff_flash_kernel(qt_ref, k_ref, vt_ref, lam_ref, gsub_ref, o_ref, *scr, lambda_init):
    l_ref, acc_ref = scr[-2:]
    t = qt_ref.shape[1]
    qi = pl.program_id(2)
    qt = qt_ref[...]
    qt_heads = (_keep_rows(qt, [(0, 32), (64, 96)]), _keep_rows(qt, [(32, 64), (96, 128)]))
    _softmax_sweep(qi, t, qt_heads, k_ref, vt_ref, scr, LOG2E)
    lp = lam_ref[...]
    lam = (jnp.exp(jnp.sum(lp[0:1] * lp[1:2], axis=1, keepdims=True))
           - jnp.exp(jnp.sum(lp[2:3] * lp[3:4], axis=1, keepdims=True)) + lambda_init)
    o_t = acc_ref[0] / l_ref[0] - lam * (acc_ref[1] / l_ref[1])
    o_t = o_t * lax.rsqrt(jnp.mean(o_t * o_t, axis=0, keepdims=True) + EPS)
    o_ref[...] = (o_t.T * gsub_ref[...] * (1.0 - lambda_init)).astype(o_ref.dtype)


def _att_scratch(t):
    return [pltpu.VMEM((2, 2, t, t), F32), pltpu.VMEM((2, 2, t, t), BF16), pltpu.VMEM((2, 2, 1, t), F32),
            pltpu.VMEM((2, 1, t), F32), pltpu.VMEM((2, 1, t), F32), pltpu.VMEM((2, LANES, t), F32)]


def _att_specs(t, nq, seq, q_blocks):
    q = [pl.BlockSpec((LANES, t), functools.partial(lambda b, p, i, off: (off + p, b * nq + i), off=off))
         for off in q_blocks]
    k = pl.BlockSpec((seq, LANES), lambda b, p, i: (b, p))
    v = pl.BlockSpec((seq // t, LANES, t), lambda b, p, i: (b, p, 0))
    return q, k, v


def _sb_attention(q_t, k, v_t, bsz, seq):
    t = ATT_TILE
    nq = seq // t
    pairs = SB_HEADS // 2
    lmat = (jnp.arange(t)[None, :] >= jnp.arange(t)[:, None]).astype(BF16)
    (q_spec,), k_spec, v_spec = _att_specs(t, nq, seq, [0])
    return pl.pallas_call(
        _sb_flash_kernel,
        grid=(bsz, pairs, nq),
        in_specs=[q_spec, k_spec, v_spec, pl.BlockSpec((t, t), lambda b, p, i: (0, 0))],
        out_specs=pl.BlockSpec((t, LANES), lambda b, p, i: (b * nq + i, p)),
        out_shape=jax.ShapeDtypeStruct((bsz * seq, pairs * LANES), BF16),
        scratch_shapes=[pltpu.VMEM((3, 2, t, t), F32), pltpu.VMEM((2, 2, t, t), F32),
                        pltpu.VMEM((2, LANES, t), F32), pltpu.VMEM((2, 1, t), F32)],
        compiler_params=_cparams(3),
        name="sb_attention",
    )(q_t, k, v_t, lmat)


def _mla_attention(q_t, kn, kr, v_t, bsz, seq):
    t = ATT_TILE
    nq = seq // t
    pairs = MLA_HEADS // 2
    (qn_spec, qr_spec), kn_spec, v_spec = _att_specs(t, nq, seq, [0, pairs])
    return pl.pallas_call(
        functools.partial(_mla_flash_kernel, scale=(MLA_NOPE + MLA_ROPE) ** -0.5),
        grid=(bsz, pairs, nq),
        in_specs=[qn_spec, qr_spec, kn_spec, pl.BlockSpec((seq, LANES), lambda b, p, i: (b, 0)), v_spec],
        out_specs=pl.BlockSpec((t, LANES), lambda b, p, i: (b * nq + i, p)),
        out_shape=jax.ShapeDtypeStruct((bsz * seq, pairs * LANES), BF16),
        scratch_shapes=[pltpu.VMEM((seq, 2 * LANES), BF16)] + _att_scratch(t),
        compiler_params=_cparams(3),
        name="mla_attention",
    )(q_t, q_t, kn, kr, v_t)


def _diff_attention(q_t, k, v_t, lam_params, g_sub, lambda_init, bsz, seq):
    t = ATT_TILE
    nq = seq // t
    (q_spec,), k_spec, v_spec = _att_specs(t, nq, seq, [0])
    return pl.pallas_call(
        functools.partial(_diff_flash_kernel, lambda_init=lambda_init),
        grid=(bsz, DIFF_HEADS, nq),
        in_specs=[q_spec, k_spec, v_spec, pl.BlockSpec(lam_params.shape, lambda b, p, i: (0, 0)),
                  pl.BlockSpec((1, LANES), lambda b, p, i: (0, 0))],
        out_specs=pl.BlockSpec((t, LANES), lambda b, p, i: (b * nq + i, p)),
        out_shape=jax.ShapeDtypeStruct((bsz * seq, DIFF_HEADS * LANES), BF16),
        scratch_shapes=_att_scratch(t),
        compiler_params=_cparams(3),
        name="diff_attention",
    )(q_t, k, v_t, lam_params, g_sub.reshape(1, LANES))


def _attention_items(nq, depth):
    rows = [(qi, qi - s, int(s == 0), int(s == qi), qi & 1) for qi in range(nq) for s in range(qi + 1)]
    rows += [(0, 0, 1, 0, nq & 1)] * (depth - 1)
    return jnp.asarray(list(zip(*rows)), jnp.int32)


def _diag_bias(t, strict):
    key = jnp.arange(t)[:, None]
    query = jnp.arange(t)[None, :]
    masked = key >= query if strict else key > query
    return jnp.stack([jnp.zeros((t, t), F32), jnp.where(masked, NEG_BIG, 0.0).astype(F32)])


def _run_pipeline(stages, n_items, finalize):
    depth = len(stages)

    def iteration(it, static):
        for k in reversed(range(depth)):
            if static and it < k:
                continue
            stages[k](it - k)

    for it in range(depth - 1):
        iteration(it, True)

    def body(it, c):
        iteration(it, False)
        finalize(it - (depth - 1))
        return c

    lax.fori_loop(depth - 1, n_items, body, 0)


def _store_tile(o_ref, qi, o_t):
    t = o_t.shape[1]
    o_ref[pl.ds(pl.multiple_of(qi * t, t), t), :] = o_t.T.astype(o_ref.dtype)


def _sb_pipe_kernel(items_ref, q_ref, k_ref, vt_ref, lmat_ref, bias_ref, o_ref,
                    z_scr, sp_scr, cs_scr, w_scr, acc_ref, carry_ref):
    t = z_scr.shape[-1]
    lmat = lmat_ref[...]
    acc_ref[...] = jnp.zeros_like(acc_ref)
    carry_ref[...] = jnp.zeros_like(carry_ref)

    def scores(w):
        qi, j, first = items_ref[0, w], items_ref[1, w], items_ref[2, w]
        kb = k_ref[pl.ds(pl.multiple_of(j * t, t), t), :]
        qt = q_ref[qi]
        bias = bias_ref[first]
        for hd, rows in enumerate(([(0, HEAD_DIM)], [(HEAD_DIM, 2 * HEAD_DIM)])):
            z_scr[w & 3, hd] = _dot(kb, _keep_rows(qt, rows)) + bias

    def softplus(w):
        for hd in range(2):
            z = z_scr[w & 3, hd]
            sp = jnp.maximum(z, 0.0) + jnp.log(1.0 + jnp.exp2(jnp.abs(z) * -LOG2E))
            sp_scr[w & 1, hd] = sp.astype(BF16)

    def sums(w):
        for hd in range(2):
            cs_scr[w & 1, hd] = _dot(lmat, sp_scr[w & 1, hd])

    def weights(w):
        first = items_ref[2, w] == 1
        for hd in range(2):
            cs = cs_scr[w & 1, hd]
            carry = jnp.where(first, 0.0, carry_ref[hd])
            w_scr[w & 1, hd] = jnp.exp(z_scr[w & 3, hd] - cs - carry).astype(BF16)
            carry_ref[hd] = carry + cs[0:1, :]

    def values(w):
        vb = vt_ref[items_ref[1, w]]
        keep = jnp.where(items_ref[2, w] == 1, 0.0, 1.0)
        for hd in range(2):
            acc_ref[hd] = keep * acc_ref[hd] + _dot(vb, w_scr[w & 1, hd])

    def finalize(w):
        @pl.when(items_ref[3, w] == 1)
        def _():
            o_t = jnp.concatenate([acc_ref[0, :HEAD_DIM], acc_ref[1, HEAD_DIM:]], axis=0)
            _store_tile(o_ref, items_ref[0, w], o_t)

    _run_pipeline([scores, softplus, sums, weights, values], items_ref.shape[1], finalize)


SOFTMAX_ROWS = 32


def _softmax_pipeline(items_ref, q_heads_of, k_ref, vt_ref, bias_ref, scr, c_log2, finalize_tile):
    s_scr, p_scr, alpha_scr, lfin_scr, m_ref, l_ref, acc_ref = scr
    t = s_scr.shape[-1]
    m_ref[...] = jnp.zeros_like(m_ref)
    l_ref[...] = jnp.zeros_like(l_ref)
    acc_ref[...] = jnp.zeros_like(acc_ref)

    def scores(w):
        qi, j, first = items_ref[0, w], items_ref[1, w], items_ref[2, w]
        kb = k_ref[pl.ds(pl.multiple_of(j * t, t), t), :]
        bias = bias_ref[first]
        for hd, qt in enumerate(q_heads_of(qi)):
            s_scr[w & 1, hd] = _dot(kb, qt) * c_log2 + bias

    def softmax(w):
        slot = w & 1
        first = items_ref[2, w] == 1
        for hd in range(2):
            mx = s_scr[slot, hd, 0:SOFTMAX_ROWS, :]
            for r in range(SOFTMAX_ROWS, t, SOFTMAX_ROWS):
                mx = jnp.maximum(mx, s_scr[slot, hd, r:r + SOFTMAX_ROWS, :])
            m_old = jnp.where(first, NEG_BIG, m_ref[hd])
            m_new = jnp.maximum(m_old, jnp.max(mx, axis=0, keepdims=True))
            alpha = jnp.where(first, 0.0, jnp.exp2(m_old - m_new))
            psum = None
            for r in range(0, t, SOFTMAX_ROWS):
                p = jnp.exp2(s_scr[slot, hd, r:r + SOFTMAX_ROWS, :] - m_new)
                p_scr[slot, hd, r:r + SOFTMAX_ROWS, :] = p.astype(BF16)
                psum = p if psum is None else psum + p
            l_new = alpha * l_ref[hd] + jnp.sum(psum, axis=0, keepdims=True)
            l_ref[hd] = l_new
            m_ref[hd] = m_new
            alpha_scr[slot, hd] = alpha
            lfin_scr[items_ref[4, w], hd] = l_new

    def values(w):
        slot = w & 1
        vb = vt_ref[items_ref[1, w]]
        for hd in range(2):
            acc_ref[hd] = alpha_scr[slot, hd] * acc_ref[hd] + _dot(vb, p_scr[slot, hd])

    def finalize(w):
        @pl.when(items_ref[3, w] == 1)
        def _():
            finalize_tile(items_ref[0, w], acc_ref, lfin_scr[items_ref[4, w]])

    _run_pipeline([scores, softmax, values], items_ref.shape[1], finalize)


def _mla_pipe_kernel(items_ref, qn_ref, qr_ref, kn_ref, kr_ref, vt_ref, bias_ref, o_ref, kcat_ref, *scr, scale):
    kcat_ref[:, :LANES] = kn_ref[...]
    kcat_ref[:, LANES:] = kr_ref[...]
    half = MLA_ROPE // 2

    def q_heads_of(qi):
        qn = qn_ref[qi]
        qr = qr_ref[qi]
        return (
            jnp.concatenate([_keep_rows(qn, [(0, MLA_NOPE)]), _keep_rows(qr, [(0, half), (64, 64 + half)])], axis=0),
            jnp.concatenate([_keep_rows(qn, [(MLA_NOPE, 2 * MLA_NOPE)]),
                             _keep_rows(qr, [(half, 2 * half), (64 + half, 64 + 2 * half)])], axis=0),
        )

    def finalize_tile(qi, acc_ref, l_fin):
        o_t = jnp.concatenate([acc_ref[0, :MLA_V] / l_fin[0], acc_ref[1, MLA_V:] / l_fin[1]], axis=0)
        _store_tile(o_ref, qi, o_t)

    _softmax_pipeline(items_ref, q_heads_of, kcat_ref, vt_ref, bias_ref, scr, scale * LOG2E, finalize_tile)


def _diff_pipe_kernel(items_ref, q_ref, k_ref, vt_ref, bias_ref, lam_ref, gsub_ref, o_ref, *scr, lambda_init):
    def q_heads_of(qi):
        qt = q_ref[qi]
        return _keep_rows(qt, [(0, 32), (64, 96)]), _keep_rows(qt, [(32, 64), (96, 128)])

    def finalize_tile(qi, acc_ref, l_fin):
        lp = lam_ref[...]
        lam = (jnp.exp(jnp.sum(lp[0:1] * lp[1:2], axis=1, keepdims=True))
               - jnp.exp(jnp.sum(lp[2:3] * lp[3:4], axis=1, keepdims=True)) + lambda_init)
        o_t = acc_ref[0] / l_fin[0] - lam * (acc_ref[1] / l_fin[1])
        o_t = o_t * lax.rsqrt(jnp.mean(o_t * o_t, axis=0, keepdims=True) + EPS)
        t = o_t.shape[1]
        o = o_t.T * gsub_ref[...] * (1.0 - lambda_init)
        o_ref[pl.ds(pl.multiple_of(qi * t, t), t), :] = o.astype(o_ref.dtype)

    _softmax_pipeline(items_ref, q_heads_of, k_ref, vt_ref, bias_ref, scr, LOG2E, finalize_tile)


def _softmax_scratch(t):
    return [pltpu.VMEM((2, 2, t, t), F32), pltpu.VMEM((2, 2, t, t), BF16), pltpu.VMEM((2, 2, 1, t), F32),
            pltpu.VMEM((2, 2, 1, t), F32), pltpu.VMEM((2, 1, t), F32), pltpu.VMEM((2, 1, t), F32),
            pltpu.VMEM((2, LANES, t), F32)]


def _pipe_call(kernel_fn, name, depth, bsz, seq, pairs, in_specs, args, scratch, strict):
    t = ATT_TILE
    items = _attention_items(seq // t, depth)
    grid_spec = pltpu.PrefetchScalarGridSpec(
        num_scalar_prefetch=1,
        grid=(bsz, pairs),
        in_specs=in_specs,
        out_specs=pl.BlockSpec((seq, LANES), lambda b, p, it: (b, p)),
        scratch_shapes=scratch,
    )
    return pl.pallas_call(
        kernel_fn,
        grid_spec=grid_spec,
        out_shape=jax.ShapeDtypeStruct((bsz * seq, pairs * LANES), BF16),
        compiler_params=_cparams(2),
        name=name,
    )(items, *args)


def _blocked_spec(seq, t, offset=0):
    return pl.BlockSpec((seq // t, LANES, t), lambda b, p, it: (b, offset + p, 0))


def _const_spec(a):
    return pl.BlockSpec(a.shape, lambda b, p, it: (0,) * a.ndim)


def _sb_attention(q_t, k, v_t, bsz, seq):
    t = ATT_TILE
    lmat = (jnp.arange(t)[None, :] >= jnp.arange(t)[:, None]).astype(BF16)
    bias = _diag_bias(t, True)
    in_specs = [_blocked_spec(seq, t), pl.BlockSpec((seq, LANES), lambda b, p, it: (b, p)), _blocked_spec(seq, t),
                _const_spec(lmat), _const_spec(bias)]
    scratch = [pltpu.VMEM((4, 2, t, t), F32), pltpu.VMEM((2, 2, t, t), BF16), pltpu.VMEM((2, 2, t, t), F32),
               pltpu.VMEM((2, 2, t, t), BF16), pltpu.VMEM((2, LANES, t), F32), pltpu.VMEM((2, 1, t), F32)]
    return _pipe_call(_sb_pipe_kernel, "sb_attention", 5, bsz, seq, SB_HEADS // 2, in_specs,
                      (q_t, k, v_t, lmat, bias), scratch, True)


def _mla_attention(q_t, kn, kr, v_t, bsz, seq):
    t = ATT_TILE
    pairs = MLA_HEADS // 2
    bias = _diag_bias(t, False)
    in_specs = [_blocked_spec(seq, t), _blocked_spec(seq, t, pairs),
                pl.BlockSpec((seq, LANES), lambda b, p, it: (b, p)), pl.BlockSpec((seq, LANES), lambda b, p, it: (b, 0)),
                _blocked_spec(seq, t), _const_spec(bias)]
    scratch = [pltpu.VMEM((seq, 2 * LANES), BF16)] + _softmax_scratch(t)
    kernel_fn = functools.partial(_mla_pipe_kernel, scale=(MLA_NOPE + MLA_ROPE) ** -0.5)
    return _pipe_call(kernel_fn, "mla_attention", 3, bsz, seq, pairs, in_specs, (q_t, q_t, kn, kr, v_t, bias),
                      scratch, False)


def _diff_attention(q_t, k, v_t, lam_params, g_sub, lambda_init, bsz, seq):
    t = ATT_TILE
    bias = _diag_bias(t, False)
    g2 = g_sub.reshape(1, LANES)
    in_specs = [_blocked_spec(seq, t), pl.BlockSpec((seq, LANES), lambda b, p, it: (b, p)), _blocked_spec(seq, t),
                _const_spec(bias), _const_spec(lam_params), _const_spec(g2)]
    kernel_fn = functools.partial(_diff_pipe_kernel, lambda_init=lambda_init)
    return _pipe_call(kernel_fn, "diff_attention", 3, bsz, seq, DIFF_HEADS, in_specs,
                      (q_t, k, v_t, bias, lam_params, g2), _softmax_scratch(t), False)


def _route(logits):
    lane = lax.broadcasted_iota(jnp.int32, logits.shape, 1).astype(F32)
    big = jnp.float32(1 << 20)

    def top1(vals):
        v = jnp.max(vals, axis=1, keepdims=True)
        i = jnp.min(jnp.where(vals == v, lane, big), axis=1, keepdims=True)
        return v, i

    is_group = (lane >= N_EXPERTS) & (lane < N_EXPERTS + N_GROUPS)
    gl = jnp.where(is_group, logits, -jnp.inf)
    gmax, gidx = top1(gl)
    g_w = 1.0 / jnp.sum(jnp.exp(gl - gmax), axis=1, keepdims=True)
    first = (gidx - N_EXPERTS) * EXPERTS_PER_GROUP
    el = jnp.where((lane >= first) & (lane < first + EXPERTS_PER_GROUP), logits, -jnp.inf)
    v1, i1 = top1(el)
    v2, i2 = top1(jnp.where(lane == i1, -jnp.inf, el))
    e2 = jnp.exp(v2 - v1)
    w1 = g_w / (1.0 + e2)
    w2 = g_w * e2 / (1.0 + e2)
    return jnp.where(lane == i1, w1, 0.0) + jnp.where(lane == i2, w2, 0.0)


def _oproj_kernel(a_ref, wo_ref, x_ref, gt_ref, g_ref, sc_ref, sh_ref, wr_ref, br_ref, xo_ref, h_ref, gates_ref):
    xn = x_ref[...] + gt_ref[0] * _dot(a_ref[...], wo_ref[...])
    xo_ref[...] = xn
    h = _rms(xn) * g_ref[...]
    h = h * (1.0 + sc_ref[0]) + sh_ref[0]
    h_ref[...] = h.astype(BF16)
    logits = jnp.dot(h, wr_ref[...], preferred_element_type=F32, precision=lax.Precision.HIGHEST) + br_ref[...]
    gates_ref[...] = _route(logits)


def _oproj(attn, w_o, x, gt, g, sc, sh, w_route, b_route, seq):
    t, d = x.shape
    tm = ROW_TILE
    per_batch = seq // tm
    row = pl.BlockSpec((tm, d), lambda i: (i, 0))
    mod = pl.BlockSpec((1, 1, d), lambda i: (i // per_batch, 0, 0))

    def full(a):
        return pl.BlockSpec(a.shape, lambda i: (0,) * a.ndim)

    g2 = g.reshape(1, d)
    return pl.pallas_call(
        _oproj_kernel,
        grid=(t // tm,),
        in_specs=[pl.BlockSpec((tm, attn.shape[1]), lambda i: (i, 0)), full(w_o), row, mod, full(g2), mod, mod,
                  full(w_route), full(b_route)],
        out_specs=[row, row, pl.BlockSpec((tm, LANES), lambda i: (i, 0))],
        out_shape=[jax.ShapeDtypeStruct((t, d), F32), jax.ShapeDtypeStruct((t, d), BF16),
                   jax.ShapeDtypeStruct((t, LANES), F32)],
        compiler_params=_cparams(1),
        name="oproj_router",
    )(attn, w_o, x, gt, g2, sc, sh, w_route, b_route)


def _moe_kernel(h_ref, gates_ref, wgu_ref, wd_ref, x_ref, gt_ref, o_ref, acc_ref):
    e = pl.program_id(1)

    @pl.when(e == 0)
    def _():
        acc_ref[...] = jnp.zeros_like(acc_ref)

    gu = _dot(h_ref[...], wgu_ref[0])
    gate, up = gu[:, :EXPERT_FF], gu[:, EXPERT_FF:]
    act = (gate / (1.0 + jnp.exp(-gate)) * up).astype(BF16)
    lane = lax.broadcasted_iota(jnp.int32, gates_ref.shape, 1)
    ge = jnp.sum(jnp.where(lane == e, gates_ref[...], 0.0), axis=1, keepdims=True)
    acc_ref[...] += ge * _dot(act, wd_ref[0])

    @pl.when(e == pl.num_programs(1) - 1)
    def _():
        o_ref[...] = x_ref[...] + gt_ref[0] * acc_ref[...]


def _moe(h, gates, w_gu, w_d, x, gt, seq):
    t, d = x.shape
    tm = MOE_TILE
    per_batch = seq // tm
    n_exp = w_gu.shape[0]
    row = lambda w: pl.BlockSpec((tm, w), lambda i, e: (i, 0))
    return pl.pallas_call(
        _moe_kernel,
        grid=(t // tm, n_exp),
        in_specs=[row(d), row(LANES),
                  pl.BlockSpec((1,) + w_gu.shape[1:], lambda i, e: (e, 0, 0)),
                  pl.BlockSpec((1,) + w_d.shape[1:], lambda i, e: (e, 0, 0)),
                  row(d), pl.BlockSpec((1, 1, d), lambda i, e: (i // per_batch, 0, 0))],
        out_specs=row(d),
        out_shape=jax.ShapeDtypeStruct((t, d), F32),
        scratch_shapes=[pltpu.VMEM((tm, d), F32)],
        compiler_params=_cparams(2),
        name="moe_experts",
    )(h, gates, w_gu, w_d, x, gt)


def _final_norm_kernel(x_ref, g_ref, o_ref):
    o_ref[...] = _rms(x_ref[...]) * g_ref[...]


def _final_norm(x, g):
    t, d = x.shape
    tm = 1024
    return pl.pallas_call(
        _final_norm_kernel,
        grid=(t // tm,),
        in_specs=[pl.BlockSpec((tm, d), lambda i: (i, 0)), pl.BlockSpec((1, d), lambda i: (0, 0))],
        out_specs=pl.BlockSpec((tm, d), lambda i: (i, 0)),
        out_shape=jax.ShapeDtypeStruct((t, d), F32),
        compiler_params=_cparams(1),
        name="final_norm",
    )(x, g.reshape(1, d))


def _diff_qk_layout(w):
    d = w.shape[0]
    w = w.reshape(d, DIFF_HEADS, 2, 2, HEAD_DIM // 2)
    return w.transpose(0, 1, 3, 2, 4).reshape(d, -1)


def _mla_layouts(w_in, w_q_up, w_kv_up):
    d = w_in.shape[0]
    half = MLA_ROPE // 2
    base = MLA_Q_RANK + MLA_KV_RANK
    r1, r2 = w_in[:, base:base + half], w_in[:, base + half:]
    z = jnp.zeros((d, 64 - 2 * half), w_in.dtype)
    w_in_l = jnp.concatenate([w_in[:, :base], r1, r1, z, r2, r2, z], axis=1)

    r = w_q_up.shape[0]
    wq = w_q_up.reshape(r, MLA_HEADS, MLA_NOPE + MLA_ROPE)
    nope = wq[:, :, :MLA_NOPE].reshape(r, -1)
    q1 = wq[:, :, MLA_NOPE:MLA_NOPE + half].reshape(r, MLA_HEADS // 2, 2 * half)
    q2 = wq[:, :, MLA_NOPE + half:].reshape(r, MLA_HEADS // 2, 2 * half)
    zq = jnp.zeros((r, MLA_HEADS // 2, 64 - 2 * half), w_q_up.dtype)
    rope = jnp.concatenate([q1, zq, q2, zq], axis=2).reshape(r, -1)
    w_q_l = jnp.concatenate([nope, rope], axis=1)

    rk = w_kv_up.shape[0]
    wkv = w_kv_up.reshape(rk, MLA_HEADS, MLA_NOPE + MLA_V)
    w_kn = wkv[:, :, :MLA_NOPE].reshape(rk, -1)
    w_v = wkv[:, :, MLA_NOPE:].reshape(rk, -1)
    return w_in_l.astype(BF16), w_q_l.T.astype(BF16), w_kn.astype(BF16), w_v.T.astype(BF16)


def _router_layout(w_group, b_group, w_router, b_router):
    d = w_group.shape[0]
    pad = LANES - N_EXPERTS - N_GROUPS
    w = jnp.concatenate([w_router, w_group, jnp.zeros((d, pad), F32)], axis=1)
    b = jnp.concatenate([b_router, b_group, jnp.zeros((pad,), F32)]).reshape(1, LANES)
    return w, b


def kernel(x, c, positions, norm_mix_g, norm_ffn_g, ada_w, ada_b, sb_w_qkv, sb_w_o, mla_w_in, mla_g_q, mla_g_kv, mla_w_q_up, mla_w_kv_up, mla_w_o, diff_w_qkv, diff_lam_q1, diff_lam_k1, diff_lam_q2, diff_lam_k2, diff_g_sub, diff_w_o, moe_w_group, moe_b_group, moe_w_router, moe_b_router, moe_w_gate_up, moe_w_down, final_g):
    bsz, seq, d = x.shape
    depth = ada_w.shape[0]
    xt = x.reshape(bsz * seq, d)
    mod = _adaln(c, ada_w, ada_b)
    rope_mla = _rope_tables(positions, MLA_ROPE) if depth > 1 else None
    rope_diff = _rope_tables(positions, HEAD_DIM) if depth > 2 else None

    for i in range(depth):
        sh_m, sc_m, gt_m, sh_f, sc_f, gt_f = (mod[i, :, k * d:(k + 1) * d].reshape(bsz, 1, d) for k in range(6))
        kind, j = i % N_MIXERS, i // N_MIXERS
        if kind == 0:
            w = sb_w_qkv[j].astype(BF16)
            q_t, k, v_t = _qkv_proj(xt, norm_mix_g[i], sc_m, sh_m, w[:, :d].T, w[:, d:2 * d], w[:, 2 * d:].T, seq)
            attn = _sb_attention(q_t, k, v_t, bsz, seq)
            w_o = sb_w_o[j]
        elif kind == 1:
            w_in_l, wq_t, w_kn, wv_t = _mla_layouts(mla_w_in[j], mla_w_q_up[j], mla_w_kv_up[j])
            q_t, kn, kr, v_t = _mla_proj(xt, norm_mix_g[i], sc_m, sh_m, w_in_l, mla_g_q[j], mla_g_kv[j], wq_t, w_kn,
                                         wv_t, rope_mla, seq)
            attn = _mla_attention(q_t, kn, kr, v_t, bsz, seq)
            w_o = mla_w_o[j]
        else:
            w = diff_w_qkv[j].astype(BF16)
            q_t, k, v_t = _qkv_proj(xt, norm_mix_g[i], sc_m, sh_m, _diff_qk_layout(w[:, :d]).T,
                                    _diff_qk_layout(w[:, d:2 * d]), w[:, 2 * d:].T, seq, rope=rope_diff)
            lam_params = jnp.stack([diff_lam_q1[j], diff_lam_k1[j], diff_lam_q2[j], diff_lam_k2[j]])
            lambda_init = 0.8 - 0.6 * math.exp(-0.3 * i)
            attn = _diff_attention(q_t, k, v_t, lam_params, diff_g_sub[j], lambda_init, bsz, seq)
            w_o = diff_w_o[j]
        w_route, b_route = _router_layout(moe_w_group[i], moe_b_group[i], moe_w_router[i], moe_b_router[i])
        xt, h, gates = _oproj(attn, w_o.astype(BF16), xt, gt_m, norm_ffn_g[i], sc_f, sh_f, w_route, b_route, seq)
        xt = _moe(h, gates, moe_w_gate_up[i].astype(BF16), moe_w_down[i].astype(BF16), xt, gt_f, seq)
    return _final_norm(xt, final_g).reshape(bsz, seq, d)
```

```python
import functools
import math

import jax
import jax.numpy as jnp
from jax import lax
from jax.experimental import pallas as pl
from jax.experimental.pallas import tpu as pltpu

F32 = jnp.float32
BF16 = jnp.bfloat16

N_MIXERS = 3
ROPE_THETA = 10000.0
EPS = 1e-6
HEAD_DIM = 64
SB_HEADS = 16
MLA_HEADS = 16
MLA_Q_RANK = 384
MLA_KV_RANK = 256
MLA_NOPE = 64
MLA_ROPE = 32
MLA_V = 64
DIFF_HEADS = 8
N_GROUPS = 4
EXPERTS_PER_GROUP = 4
N_EXPERTS = N_GROUPS * EXPERTS_PER_GROUP
EXPERT_FF = 512

LANES = 128
LOG2E = 1.4426950408889634
NEG_BIG = -1e30
VMEM_LIMIT = 56 * 1024 * 1024

ROW_TILE = 512
ATT_TILE = 256
MOE_TILE = 256
PROJ_CHUNK = 256


def _cparams(n_axes):
    return pltpu.CompilerParams(dimension_semantics=("arbitrary",) * n_axes, vmem_limit_bytes=VMEM_LIMIT)


def _dot(a, b):
    return jnp.dot(a, b, preferred_element_type=F32)


def _dot_nt(a, b):
    return lax.dot_general(a, b, (((1,), (1,)), ((), ())), preferred_element_type=F32)


def _rms(x):
    return x * lax.rsqrt(jnp.mean(x * x, axis=-1, keepdims=True) + EPS)


def _rope_blocks(y, cos, sin_signed):
    out = []
    for j in range(y.shape[1] // LANES):
        yb = y[:, j * LANES:(j + 1) * LANES]
        out.append(yb * cos + pltpu.roll(yb, 64, 1) * sin_signed)
    return out[0] if len(out) == 1 else jnp.concatenate(out, axis=1)


def _rope_rows(y, cos_t, sin_t):
    out = []
    for j in range(y.shape[0] // LANES):
        yb = y[j * LANES:(j + 1) * LANES]
        out.append(yb * cos_t + jnp.concatenate([yb[64:], yb[:64]], axis=0) * sin_t)
    return out[0] if len(out) == 1 else jnp.concatenate(out, axis=0)


def _adaln_kernel(c_ref, w_ref, b_ref, o_ref):
    c = c_ref[...]
    ca = c / (1.0 + jnp.exp(-c))
    o_ref[0] = jnp.dot(ca, w_ref[0], preferred_element_type=F32, precision=lax.Precision.HIGHEST) + b_ref[0]


def _adaln(c, ada_w, ada_b):
    depth, d, n = ada_w.shape
    bsz = c.shape[0]
    rows = 8
    cp = jnp.zeros((rows, d), F32).at[:bsz].set(c)
    tn = 1536
    out = pl.pallas_call(
        _adaln_kernel,
        grid=(depth, n // tn),
        in_specs=[
            pl.BlockSpec((rows, d), lambda i, j: (0, 0)),
            pl.BlockSpec((1, d, tn), lambda i, j: (i, 0, j)),
            pl.BlockSpec((1, 1, tn), lambda i, j: (i, 0, j)),
        ],
        out_specs=pl.BlockSpec((1, rows, tn), lambda i, j: (i, 0, j)),
        out_shape=jax.ShapeDtypeStruct((depth, rows, n), F32),
        compiler_params=_cparams(2),
        name="adaln_mod",
    )(cp, ada_w, ada_b.reshape(depth, 1, n))
    return out[:, :bsz]


def _rope_table_kernel(pos_c_ref, pos_r_ref, invf_r_ref, sign_r_ref, invf_c_ref, sign_c_ref,
                       cos_ref, sin_ref, cost_ref, sint_ref):
    ang = pos_c_ref[...].astype(F32) * invf_r_ref[...]
    cos_ref[...] = jnp.cos(ang)
    sin_ref[...] = jnp.sin(ang) * sign_r_ref[...]
    ang_t = invf_c_ref[...] * pos_r_ref[...].astype(F32)
    cost_ref[...] = jnp.cos(ang_t)
    sint_ref[...] = jnp.sin(ang_t) * sign_c_ref[...]


def _rope_tables(positions, dim):
    t = positions.size
    half = dim // 2
    inv_freq = ROPE_THETA ** (-jnp.arange(0, dim, 2, dtype=F32) / dim)
    invf = jnp.tile(inv_freq, LANES // half)
    sign = jnp.where(jnp.arange(LANES) < 64, -1.0, 1.0).astype(F32)
    tm = 2048
    small_r = pl.BlockSpec((1, LANES), lambda i: (0, 0))
    small_c = pl.BlockSpec((LANES, 1), lambda i: (0, 0))
    return pl.pallas_call(
        _rope_table_kernel,
        grid=(t // tm,),
        in_specs=[pl.BlockSpec((tm, 1), lambda i: (i, 0)), pl.BlockSpec((1, tm), lambda i: (0, i)),
                  small_r, small_r, small_c, small_c],
        out_specs=[pl.BlockSpec((tm, LANES), lambda i: (i, 0))] * 2 + [pl.BlockSpec((LANES, tm), lambda i: (0, i))] * 2,
        out_shape=[jax.ShapeDtypeStruct((t, LANES), F32)] * 2 + [jax.ShapeDtypeStruct((LANES, t), F32)] * 2,
        compiler_params=_cparams(1),
        name="rope_tables",
    )(positions.reshape(t, 1), positions.reshape(1, t), invf.reshape(1, LANES), sign.reshape(1, LANES),
      invf.reshape(LANES, 1), sign.reshape(LANES, 1))


def _modulated_norm(x_ref, g_ref, sc_ref, sh_ref):
    h = _rms(x_ref[...]) * g_ref[...]
    return h * (1.0 + sc_ref[0]) + sh_ref[0]


def _store_token_blocks(o_ref, w_t_ref, hb, post=None):
    tb = o_ref.shape[2]
    for r in range(0, w_t_ref.shape[0], PROJ_CHUNK):
        y = _dot_nt(w_t_ref[r:r + PROJ_CHUNK, :], hb)
        if post is not None:
            y = post(y, r)
        y = y.astype(BF16)
        for c in range(o_ref.shape[0]):
            o_ref[c, r:r + PROJ_CHUNK, :] = y[:, c * tb:(c + 1) * tb]


def _qkv_kernel(*refs, rope, q_scale):
    if rope:
        x_ref, g_ref, sc_ref, sh_ref, wqt_ref, wk_ref, wvt_ref, cos_ref, sin_ref, cost_ref, sint_ref = refs[:11]
    else:
        x_ref, g_ref, sc_ref, sh_ref, wqt_ref, wk_ref, wvt_ref = refs[:7]
    q_ref, k_ref, v_ref = refs[-3:]
    hb = _modulated_norm(x_ref, g_ref, sc_ref, sh_ref).astype(BF16)

    def q_post(qt, r):
        if rope:
            qt = _rope_rows(qt, cost_ref[...], sint_ref[...])
        return qt * q_scale

    _store_token_blocks(q_ref, wqt_ref, hb, q_post)
    for r in range(0, wk_ref.shape[1], 512):
        k = _dot(hb, wk_ref[:, r:r + 512])
        if rope:
            k = _rope_blocks(k, cos_ref[...], sin_ref[...])
        k_ref[:, r:r + 512] = k.astype(BF16)
    _store_token_blocks(v_ref, wvt_ref, hb)


def _qkv_proj(x, g, sc, sh, wq_t, wk, wv_t, seq, rope=None):
    t, d = x.shape
    tm, tk = ROW_TILE, ATT_TILE
    per_batch = seq // tm
    row = pl.BlockSpec((tm, d), lambda i: (i, 0))
    mod = pl.BlockSpec((1, 1, d), lambda i: (i // per_batch, 0, 0))

    def full(a):
        return pl.BlockSpec(a.shape, lambda i: (0,) * a.ndim)

    g2 = g.reshape(1, d)
    in_specs = [row, full(g2), mod, mod, full(wq_t), full(wk), full(wv_t)]
    args = [x, g2, sc, sh, wq_t, wk, wv_t]
    if rope is not None:
        tab = pl.BlockSpec((tm, LANES), lambda i: (i, 0))
        tab_t = pl.BlockSpec((LANES, tm), lambda i: (0, i))
        in_specs += [tab, tab, tab_t, tab_t]
        args += list(rope)
    nq, nk, nv = wq_t.shape[0], wk.shape[1], wv_t.shape[0]
    return pl.pallas_call(
        functools.partial(_qkv_kernel, rope=rope is not None, q_scale=HEAD_DIM ** -0.5),
        grid=(t // tm,),
        in_specs=in_specs,
        out_specs=[pl.BlockSpec((tm // tk, nq, tk), lambda i: (i, 0, 0)), pl.BlockSpec((tm, nk), lambda i: (i, 0)),
                   pl.BlockSpec((tm // tk, nv, tk), lambda i: (i, 0, 0))],
        out_shape=[jax.ShapeDtypeStruct((t // tk, nq, tk), BF16), jax.ShapeDtypeStruct((t, nk), BF16),
                   jax.ShapeDtypeStruct((t // tk, nv, tk), BF16)],
        compiler_params=_cparams(1),
        name="qkv_proj",
    )(*args)


def _mla_proj_kernel(x_ref, g_ref, sc_ref, sh_ref, win_ref, gq_ref, gkv_ref, wqt_ref, wkn_ref, wvt_ref,
                     cos_ref, sin_ref, cost_ref, sint_ref, q_ref, kn_ref, kr_ref, v_ref):
    hb = _modulated_norm(x_ref, g_ref, sc_ref, sh_ref).astype(BF16)
    lat = _dot(hb, win_ref[...])
    cq = (_rms(lat[:, :MLA_Q_RANK]) * gq_ref[...]).astype(BF16)
    ckv = (_rms(lat[:, MLA_Q_RANK:MLA_Q_RANK + MLA_KV_RANK]) * gkv_ref[...]).astype(BF16)
    kr_ref[...] = _rope_blocks(lat[:, MLA_Q_RANK + MLA_KV_RANK:], cos_ref[...], sin_ref[...]).astype(BF16)
    n_nope = MLA_HEADS * MLA_NOPE
    _store_token_blocks(q_ref, wqt_ref, cq,
                        lambda qt, r: _rope_rows(qt, cost_ref[...], sint_ref[...]) if r >= n_nope else qt)
    kn_ref[...] = _dot(ckv, wkn_ref[...]).astype(BF16)
    _store_token_blocks(v_ref, wvt_ref, ckv)


def _mla_proj(x, g, sc, sh, w_in, g_q, g_kv, wq_t, wkn, wv_t, rope, seq):
    t, d = x.shape
    tm, tk = ROW_TILE, ATT_TILE
    per_batch = seq // tm
    row = pl.BlockSpec((tm, d), lambda i: (i, 0))
    mod = pl.BlockSpec((1, 1, d), lambda i: (i // per_batch, 0, 0))

    def full(a):
        return pl.BlockSpec(a.shape, lambda i: (0,) * a.ndim)

    tab = pl.BlockSpec((tm, LANES), lambda i: (i, 0))
    tab_t = pl.BlockSpec((LANES, tm), lambda i: (0, i))
    g2, gq2, gkv2 = g.reshape(1, d), g_q.reshape(1, -1), g_kv.reshape(1, -1)
    nq, nk, nv = wq_t.shape[0], wkn.shape[1], wv_t.shape[0]
    return pl.pallas_call(
        _mla_proj_kernel,
        grid=(t // tm,),
        in_specs=[row, full(g2), mod, mod, full(w_in), full(gq2), full(gkv2), full(wq_t), full(wkn), full(wv_t),
                  tab, tab, tab_t, tab_t],
        out_specs=[pl.BlockSpec((tm // tk, nq, tk), lambda i: (i, 0, 0)), pl.BlockSpec((tm, nk), lambda i: (i, 0)),
                   pl.BlockSpec((tm, LANES), lambda i: (i, 0)), pl.BlockSpec((tm // tk, nv, tk), lambda i: (i, 0, 0))],
        out_shape=[jax.ShapeDtypeStruct((t // tk, nq, tk), BF16), jax.ShapeDtypeStruct((t, nk), BF16),
                   jax.ShapeDtypeStruct((t, LANES), BF16), jax.ShapeDtypeStruct((t // tk, nv, tk), BF16)],
        compiler_params=_cparams(1),
        name="mla_proj",
    )(x, g2, sc, sh, w_in, gq2, gkv2, wq_t, wkn, wv_t, *rope)


def _keep_rows(x, keep):
    n_rows, n = x.shape
    parts, pos = [], 0
    for a, b in keep:
        if a > pos:
            parts.append(jnp.zeros((a - pos, n), x.dtype))
        parts.append(x[a:b])
        pos = b
    if pos < n_rows:
        parts.append(jnp.zeros((n_rows - pos, n), x.dtype))
    return jnp.concatenate(parts, axis=0)


def _causal_valid(t, strict):
    key = lax.broadcasted_iota(jnp.int32, (t, t), 0)
    query = lax.broadcasted_iota(jnp.int32, (t, t), 1)
    return key < query if strict else key <= query


def _rem3(step):
    return step % 3 if isinstance(step, int) else lax.rem(step, 3)


def _sb_flash_kernel(qt_ref, k_ref, vt_ref, lmat_ref, o_ref, z_scr, cs_scr, acc_ref, carry_ref):
    t = qt_ref.shape[1]
    qi = pl.program_id(2)
    qt = qt_ref[...]
    qt_heads = (_keep_rows(qt, [(0, HEAD_DIM)]), _keep_rows(qt, [(HEAD_DIM, 2 * HEAD_DIM)]))
    lmat = lmat_ref[...]

    acc_ref[...] = jnp.zeros_like(acc_ref)
    carry_ref[...] = jnp.zeros_like(carry_ref)
    valid = _causal_valid(t, True)

    def block_of(step):
        return jnp.maximum(qi - step, 0)

    def scores(step):
        kb = k_ref[pl.ds(pl.multiple_of(block_of(step) * t, t), t), :]
        for hd in range(2):
            z_scr[_rem3(step), hd] = _dot(kb, qt_heads[hd])

    def sums(step, masked=False):
        for hd in range(2):
            z = z_scr[_rem3(step), hd]
            sp = jnp.maximum(z, 0.0) + jnp.log(1.0 + jnp.exp2(jnp.abs(z) * -LOG2E))
            if masked:
                sp = jnp.where(valid, sp, 0.0)
            cs_scr[step & 1, hd] = _dot(lmat, sp.astype(BF16))

    def values(step, masked=False):
        vb = vt_ref[block_of(step)]
        for hd in range(2):
            cs = cs_scr[step & 1, hd]
            w = jnp.exp(z_scr[_rem3(step), hd] - cs - carry_ref[hd])
            if masked:
                w = jnp.where(valid, w, 0.0)
            acc_ref[hd] += _dot(vb, w.astype(BF16))
            carry_ref[hd] += cs[0:1, :]

    def iteration(i, first=False):
        if not first or i >= 2:
            values(i - 2, masked=first)
        if not first or i >= 1:
            sums(i - 1, masked=first and i == 1)
        scores(i)

    for i in range(3):
        iteration(i, first=True)

    def body(i, c):
        iteration(i)
        return c

    lax.fori_loop(3, qi + 3, body, 0)
    o_t = jnp.concatenate([acc_ref[0, :HEAD_DIM], acc_ref[1, HEAD_DIM:]], axis=0)
    o_ref[...] = o_t.T.astype(o_ref.dtype)


def _softmax_sweep(qi, t, qt_heads, k_ref, vt_ref, scr, c_log2):
    s_scr, p_scr, alpha_scr, m_ref, l_ref, acc_ref = scr
    n = qi + 1
    m_ref[...] = jnp.full_like(m_ref, NEG_BIG)
    l_ref[...] = jnp.zeros_like(l_ref)
    acc_ref[...] = jnp.zeros_like(acc_ref)
    p_scr[1] = jnp.zeros(p_scr.shape[1:], p_scr.dtype)
    alpha_scr[1] = jnp.ones(alpha_scr.shape[1:], alpha_scr.dtype)

    def scores(j):
        kb = k_ref[pl.ds(pl.multiple_of(j * t, t), t), :]
        for hd in range(2):
            s_scr[j & 1, hd] = _dot(kb, qt_heads[hd]) * c_log2

    rows = 32

    def softmax(j, masked):
        slot = j & 1
        for hd in range(2):
            def chunk(r):
                s = s_scr[slot, hd, r:r + rows, :]
                if masked:
                    key = lax.broadcasted_iota(jnp.int32, (rows, t), 0) + r
                    query = lax.broadcasted_iota(jnp.int32, (rows, t), 1)
                    s = jnp.where(key <= query, s, -jnp.inf)
                return s

            mx = chunk(0)
            for r in range(rows, t, rows):
                mx = jnp.maximum(mx, chunk(r))
            m_old = m_ref[hd]
            m_new = jnp.maximum(m_old, jnp.max(mx, axis=0, keepdims=True))
            alpha = jnp.exp2(m_old - m_new)
            psum = None
            for r in range(0, t, rows):
                p = jnp.exp2(chunk(r) - m_new)
                p_scr[slot, hd, r:r + rows, :] = p.astype(BF16)
                psum = p if psum is None else psum + p
            l_ref[hd] = alpha * l_ref[hd] + jnp.sum(psum, axis=0, keepdims=True)
            m_ref[hd] = m_new
            alpha_scr[slot, hd] = alpha

    def values(j):
        slot = j & 1
        vb = vt_ref[jnp.maximum(j, 0)]
        for hd in range(2):
            acc_ref[hd] = alpha_scr[slot, hd] * acc_ref[hd] + _dot(vb, p_scr[slot, hd])

    scores(0)

    def body(i, c):
        values(i - 1)
        softmax(i, False)
        scores(i + 1)
        return c

    lax.fori_loop(0, n - 1, body, 0)
    values(n - 2)
    softmax(n - 1, True)
    values(n - 1)


def _mla_flash_kernel(qn_ref, qr_ref, kn_ref, kr_ref, vt_ref, o_ref, kcat_ref, *scr, scale):
    l_ref, acc_ref = scr[-2:]
    t = qn_ref.shape[1]
    qi = pl.program_id(2)

    @pl.when(qi == 0)
    def _():
        kcat_ref[:, :LANES] = kn_ref[...]
        kcat_ref[:, LANES:] = kr_ref[...]

    qn = qn_ref[...]
    qr = qr_ref[...]
    half = MLA_ROPE // 2
    qt_heads = (
        jnp.concatenate([_keep_rows(qn, [(0, MLA_NOPE)]), _keep_rows(qr, [(0, half), (64, 64 + half)])], axis=0),
        jnp.concatenate([_keep_rows(qn, [(MLA_NOPE, 2 * MLA_NOPE)]),
                         _keep_rows(qr, [(half, 2 * half), (64 + half, 64 + 2 * half)])], axis=0),
    )
    _softmax_sweep(qi, t, qt_heads, kcat_ref, vt_ref, scr, scale * LOG2E)
    o_t = jnp.concatenate([acc_ref[0, :MLA_V] / l_ref[0], acc_ref[1, MLA_V:] / l_ref[1]], axis=0)
    o_ref[...] = o_t.T.astype(o_ref.dtype)


def _diff_flash_kernel(qt_ref, k_ref, vt_ref, lam_ref, gsub_ref, o_ref, *scr, lambda_init):
    l_ref, acc_ref = scr[-2:]
    t = qt_ref.shape[1]
    qi = pl.program_id(2)
    qt = qt_ref[...]
    qt_heads = (_keep_rows(qt, [(0, 32), (64, 96)]), _keep_rows(qt, [(32, 64), (96, 128)]))
    _softmax_sweep(qi, t, qt_heads, k_ref, vt_ref, scr, LOG2E)
    lp = lam_ref[...]
    lam = (jnp.exp(jnp.sum(lp[0:1] * lp[1:2], axis=1, keepdims=True))
           - jnp.exp(jnp.sum(lp[2:3] * lp[3:4], axis=1, keepdims=True)) + lambda_init)
    o_t = acc_ref[0] / l_ref[0] - lam * (acc_ref[1] / l_ref[1])
    o_t = o_t * lax.rsqrt(jnp.mean(o_t * o_t, axis=0, keepdims=True) + EPS)
    o_ref[...] = (o_t.T * gsub_ref[...] * (1.0 - lambda_init)).astype(o_ref.dtype)


def _att_scratch(t):
    return [pltpu.VMEM((2, 2, t, t), F32), pltpu.VMEM((2, 2, t, t), BF16), pltpu.VMEM((2, 2, 1, t), F32),
            pltpu.VMEM((2, 1, t), F32), pltpu.VMEM((2, 1, t), F32), pltpu.VMEM((2, LANES, t), F32)]


def _att_specs(t, nq, seq, q_blocks):
    q = [pl.BlockSpec((LANES, t), functools.partial(lambda b, p, i, off: (off + p, b * nq + i), off=off))
         for off in q_blocks]
    k = pl.BlockSpec((seq, LANES), lambda b, p, i: (b, p))
    v = pl.BlockSpec((seq // t, LANES, t), lambda b, p, i: (b, p, 0))
    return q, k, v


def _sb_attention(q_t, k, v_t, bsz, seq):
    t = ATT_TILE
    nq = seq // t
    pairs = SB_HEADS // 2
    lmat = (jnp.arange(t)[None, :] >= jnp.arange(t)[:, None]).astype(BF16)
    (q_spec,), k_spec, v_spec = _att_specs(t, nq, seq, [0])
    return pl.pallas_call(
        _sb_flash_kernel,
        grid=(bsz, pairs, nq),
        in_specs=[q_spec, k_spec, v_spec, pl.BlockSpec((t, t), lambda b, p, i: (0, 0))],
        out_specs=pl.BlockSpec((t, LANES), lambda b, p, i: (b * nq + i, p)),
        out_shape=jax.ShapeDtypeStruct((bsz * seq, pairs * LANES), BF16),
        scratch_shapes=[pltpu.VMEM((3, 2, t, t), F32), pltpu.VMEM((2, 2, t, t), F32),
                        pltpu.VMEM((2, LANES, t), F32), pltpu.VMEM((2, 1, t), F32)],
        compiler_params=_cparams(3),
        name="sb_attention",
    )(q_t, k, v_t, lmat)


def _mla_attention(q_t, kn, kr, v_t, bsz, seq):
    t = ATT_TILE
    nq = seq // t
    pairs = MLA_HEADS // 2
    (qn_spec, qr_spec), kn_spec, v_spec = _att_specs(t, nq, seq, [0, pairs])
    return pl.pallas_call(
        functools.partial(_mla_flash_kernel, scale=(MLA_NOPE + MLA_ROPE) ** -0.5),
        grid=(bsz, pairs, nq),
        in_specs=[qn_spec, qr_spec, kn_spec, pl.BlockSpec((seq, LANES), lambda b, p, i: (b, 0)), v_spec],
        out_specs=pl.BlockSpec((t, LANES), lambda b, p, i: (b * nq + i, p)),
        out_shape=jax.ShapeDtypeStruct((bsz * seq, pairs * LANES), BF16),
        scratch_shapes=[pltpu.VMEM((seq, 2 * LANES), BF16)] + _att_scratch(t),
        compiler_params=_cparams(3),
        name="mla_attention",
    )(q_t, q_t, kn, kr, v_t)


def _diff_attention(q_t, k, v_t, lam_params, g_sub, lambda_init, bsz, seq):
    t = ATT_TILE
    nq = seq // t
    (q_spec,), k_spec, v_spec = _att_specs(t, nq, seq, [0])
    return pl.pallas_call(
        functools.partial(_diff_flash_kernel, lambda_init=lambda_init),
        grid=(bsz, DIFF_HEADS, nq),
        in_specs=[q_spec, k_spec, v_spec, pl.BlockSpec(lam_params.shape, lambda b, p, i: (0, 0)),
                  pl.BlockSpec((1, LANES), lambda b, p, i: (0, 0))],
        out_specs=pl.BlockSpec((t, LANES), lambda b, p, i: (b * nq + i, p)),
        out_shape=jax.ShapeDtypeStruct((bsz * seq, DIFF_HEADS * LANES), BF16),
        scratch_shapes=_att_scratch(t),
        compiler_params=_cparams(3),
        name="diff_attention",
    )(q_t, k, v_t, lam_params, g_sub.reshape(1, LANES))


def _attention_items(nq, depth):
    rows = [(qi, qi - s, int(s == 0), int(s == qi), qi & 1) for qi in range(nq) for s in range(qi + 1)]
    rows += [(0, 0, 1, 0, nq & 1)] * (depth - 1)
    return jnp.asarray(list(zip(*rows)), jnp.int32)


def _diag_bias(t, strict):
    key = jnp.arange(t)[:, None]
    query = jnp.arange(t)[None, :]
    masked = key >= query if strict else key > query
    return jnp.stack([jnp.zeros((t, t), F32), jnp.where(masked, NEG_BIG, 0.0).astype(F32)])


def _run_pipeline(stages, n_items, finalize):
    depth = len(stages)

    def iteration(it, static):
        for k in reversed(range(depth)):
            if static and it < k:
                continue
            stages[k](it - k)

    for it in range(depth - 1):
        iteration(it, True)

    def body(it, c):
        iteration(it, False)
        finalize(it - (depth - 1))
        return c

    lax.fori_loop(depth - 1, n_items, body, 0)


def _store_tile(o_ref, qi, o_t):
    t = o_t.shape[1]
    o_ref[pl.ds(pl.multiple_of(qi * t, t), t), :] = o_t.T.astype(o_ref.dtype)


def _sb_pipe_kernel(items_ref, q_ref, k_ref, vt_ref, lmat_ref, bias_ref, o_ref,
                    z_scr, sp_scr, cs_scr, w_scr, acc_ref, carry_ref):
    t = z_scr.shape[-1]
    lmat = lmat_ref[...]
    acc_ref[...] = jnp.zeros_like(acc_ref)
    carry_ref[...] = jnp.zeros_like(carry_ref)

    def scores(w):
        qi, j, first = items_ref[0, w], items_ref[1, w], items_ref[2, w]
        kb = k_ref[pl.ds(pl.multiple_of(j * t, t), t), :]
        qt = q_ref[qi]
        bias = bias_ref[first]
        for hd, rows in enumerate(([(0, HEAD_DIM)], [(HEAD_DIM, 2 * HEAD_DIM)])):
            z_scr[w & 3, hd] = _dot(kb, _keep_rows(qt, rows)) + bias

    def softplus(w):
        for hd in range(2):
            z = z_scr[w & 3, hd]
            sp = jnp.maximum(z, 0.0) + jnp.log(1.0 + jnp.exp2(jnp.abs(z) * -LOG2E))
            sp_scr[w & 1, hd] = sp.astype(BF16)

    def sums(w):
        for hd in range(2):
            cs_scr[w & 1, hd] = _dot(lmat, sp_scr[w & 1, hd])

    def weights(w):
        first = items_ref[2, w] == 1
        for hd in range(2):
            cs = cs_scr[w & 1, hd]
            carry = jnp.where(first, 0.0, carry_ref[hd])
            w_scr[w & 1, hd] = jnp.exp(z_scr[w & 3, hd] - cs - carry).astype(BF16)
            carry_ref[hd] = carry + cs[0:1, :]

    def values(w):
        vb = vt_ref[items_ref[1, w]]
        keep = jnp.where(items_ref[2, w] == 1, 0.0, 1.0)
        for hd in range(2):
            acc_ref[hd] = keep * acc_ref[hd] + _dot(vb, w_scr[w & 1, hd])

    def finalize(w):
        @pl.when(items_ref[3, w] == 1)
        def _():
            o_t = jnp.concatenate([acc_ref[0, :HEAD_DIM], acc_ref[1, HEAD_DIM:]], axis=0)
            _store_tile(o_ref, items_ref[0, w], o_t)

    _run_pipeline([scores, softplus, sums, weights, values], items_ref.shape[1], finalize)


SOFTMAX_ROWS = 32


def _softmax_pipeline(items_ref, q_heads_of, k_ref, vt_ref, bias_ref, scr, c_log2, finalize_tile):
    s_scr, p_scr, alpha_scr, lfin_scr, m_ref, l_ref, acc_ref = scr
    t = s_scr.shape[-1]
    m_ref[...] = jnp.zeros_like(m_ref)
    l_ref[...] = jnp.zeros_like(l_ref)
    acc_ref[...] = jnp.zeros_like(acc_ref)

    def scores(w):
        qi, j, first = items_ref[0, w], items_ref[1, w], items_ref[2, w]
        kb = k_ref[pl.ds(pl.multiple_of(j * t, t), t), :]
        bias = bias_ref[first]
        for hd, qt in enumerate(q_heads_of(qi)):
            s_scr[w & 1, hd] = _dot(kb, qt) * c_log2 + bias

    def softmax(w):
        slot = w & 1
        first = items_ref[2, w] == 1
        for hd in range(2):
            mx = s_scr[slot, hd, 0:SOFTMAX_ROWS, :]
            for r in range(SOFTMAX_ROWS, t, SOFTMAX_ROWS):
                mx = jnp.maximum(mx, s_scr[slot, hd, r:r + SOFTMAX_ROWS, :])
            m_old = jnp.where(first, NEG_BIG, m_ref[hd])
            m_new = jnp.maximum(m_old, jnp.max(mx, axis=0, keepdims=True))
            alpha = jnp.where(first, 0.0, jnp.exp2(m_old - m_new))
            psum = None
            for r in range(0, t, SOFTMAX_ROWS):
                p = jnp.exp2(s_scr[slot, hd, r:r + SOFTMAX_ROWS, :] - m_new)
                p_scr[slot, hd, r:r + SOFTMAX_ROWS, :] = p.astype(BF16)
                psum = p if psum is None else psum + p
            l_new = alpha * l_ref[hd] + jnp.sum(psum, axis=0, keepdims=True)
            l_ref[hd] = l_new
            m_ref[hd] = m_new
            alpha_scr[slot, hd] = alpha
            lfin_scr[items_ref[4, w], hd] = l_new

    def values(w):
        slot = w & 1
        vb = vt_ref[items_ref[1, w]]
        for hd in range(2):
            acc_ref[hd] = alpha_scr[slot, hd] * acc_ref[hd] + _dot(vb, p_scr[slot, hd])

    def finalize(w):
        @pl.when(items_ref[3, w] == 1)
        def _():
            finalize_tile(items_ref[0, w], acc_ref, lfin_scr[items_ref[4, w]])

    _run_pipeline([scores, softmax, values], items_ref.shape[1], finalize)


def _mla_pipe_kernel(items_ref, qn_ref, qr_ref, kn_ref, kr_ref, vt_ref, bias_ref, o_ref, kcat_ref, *scr, scale):
    kcat_ref[:, :LANES] = kn_ref[...]
    kcat_ref[:, LANES:] = kr_ref[...]
    half = MLA_ROPE // 2

    def q_heads_of(qi):
        qn = qn_ref[qi]
        qr = qr_ref[qi]
        return (
            jnp.concatenate([_keep_rows(qn, [(0, MLA_NOPE)]), _keep_rows(qr, [(0, half), (64, 64 + half)])], axis=0),
            jnp.concatenate([_keep_rows(qn, [(MLA_NOPE, 2 * MLA_NOPE)]),
                             _keep_rows(qr, [(half, 2 * half), (64 + half, 64 + 2 * half)])], axis=0),
        )

    def finalize_tile(qi, acc_ref, l_fin):
        o_t = jnp.concatenate([acc_ref[0, :MLA_V] / l_fin[0], acc_ref[1, MLA_V:] / l_fin[1]], axis=0)
        _store_tile(o_ref, qi, o_t)

    _softmax_pipeline(items_ref, q_heads_of, kcat_ref, vt_ref, bias_ref, scr, scale * LOG2E, finalize_tile)


def _diff_pipe_kernel(items_ref, q_ref, k_ref, vt_ref, bias_ref, lam_ref, gsub_ref, o_ref, *scr, lambda_init):
    def q_heads_of(qi):
        qt = q_ref[qi]
        return _keep_rows(qt, [(0, 32), (64, 96)]), _keep_rows(qt, [(32, 64), (96, 128)])

    def finalize_tile(qi, acc_ref, l_fin):
        lp = lam_ref[...]
        lam = (jnp.exp(jnp.sum(lp[0:1] * lp[1:2], axis=1, keepdims=True))
               - jnp.exp(jnp.sum(lp[2:3] * lp[3:4], axis=1, keepdims=True)) + lambda_init)
        o_t = acc_ref[0] / l_fin[0] - lam * (acc_ref[1] / l_fin[1])
        o_t = o_t * lax.rsqrt(jnp.mean(o_t * o_t, axis=0, keepdims=True) + EPS)
        t = o_t.shape[1]
        o = o_t.T * gsub_ref[...] * (1.0 - lambda_init)
        o_ref[pl.ds(pl.multiple_of(qi * t, t), t), :] = o.astype(o_ref.dtype)

    _softmax_pipeline(items_ref, q_heads_of, k_ref, vt_ref, bias_ref, scr, LOG2E, finalize_tile)


def _softmax_scratch(t):
    return [pltpu.VMEM((2, 2, t, t), F32), pltpu.VMEM((2, 2, t, t), BF16), pltpu.VMEM((2, 2, 1, t), F32),
            pltpu.VMEM((2, 2, 1, t), F32), pltpu.VMEM((2, 1, t), F32), pltpu.VMEM((2, 1, t), F32),
            pltpu.VMEM((2, LANES, t), F32)]


def _pipe_call(kernel_fn, name, depth, bsz, seq, pairs, in_specs, args, scratch, strict):
    t = ATT_TILE
    items = _attention_items(seq // t, depth)
    grid_spec = pltpu.PrefetchScalarGridSpec(
        num_scalar_prefetch=1,
        grid=(bsz, pairs),
        in_specs=in_specs,
        out_specs=pl.BlockSpec((seq, LANES), lambda b, p, it: (b, p)),
        scratch_shapes=scratch,
    )
    return pl.pallas_call(
        kernel_fn,
        grid_spec=grid_spec,
        out_shape=jax.ShapeDtypeStruct((bsz * seq, pairs * LANES), BF16),
        compiler_params=_cparams(2),
        name=name,
    )(items, *args)


def _blocked_spec(seq, t, offset=0):
    return pl.BlockSpec((seq // t, LANES, t), lambda b, p, it: (b, offset + p, 0))


def _const_spec(a):
    return pl.BlockSpec(a.shape, lambda b, p, it: (0,) * a.ndim)


def _sb_attention(q_t, k, v_t, bsz, seq):
    t = ATT_TILE
    lmat = (jnp.arange(t)[None, :] >= jnp.arange(t)[:, None]).astype(BF16)
    bias = _diag_bias(t, True)
    in_specs = [_blocked_spec(seq, t), pl.BlockSpec((seq, LANES), lambda b, p, it: (b, p)), _blocked_spec(seq, t),
                _const_spec(lmat), _const_spec(bias)]
    scratch = [pltpu.VMEM((4, 2, t, t), F32), pltpu.VMEM((2, 2, t, t), BF16), pltpu.VMEM((2, 2, t, t), F32),
               pltpu.VMEM((2, 2, t, t), BF16), pltpu.VMEM((2, LANES, t), F32), pltpu.VMEM((2, 1, t), F32)]
    return _pipe_call(_sb_pipe_kernel, "sb_attention", 5, bsz, seq, SB_HEADS // 2, in_specs,
                      (q_t, k, v_t, lmat, bias), scratch, True)


def _mla_attention(q_t, kn, kr, v_t, bsz, seq):
    t = ATT_TILE
    pairs = MLA_HEADS // 2
    bias = _diag_bias(t, False)
    in_specs = [_blocked_spec(seq, t), _blocked_spec(seq, t, pairs),
                pl.BlockSpec((seq, LANES), lambda b, p, it: (b, p)), pl.BlockSpec((seq, LANES), lambda b, p, it: (b, 0)),
                _blocked_spec(seq, t), _const_spec(bias)]
    scratch = [pltpu.VMEM((seq, 2 * LANES), BF16)] + _softmax_scratch(t)
    kernel_fn = functools.partial(_mla_pipe_kernel, scale=(MLA_NOPE + MLA_ROPE) ** -0.5)
    return _pipe_call(kernel_fn, "mla_attention", 3, bsz, seq, pairs, in_specs, (q_t, q_t, kn, kr, v_t, bias),
                      scratch, False)


def _diff_attention(q_t, k, v_t, lam_params, g_sub, lambda_init, bsz, seq):
    t = ATT_TILE
    bias = _diag_bias(t, False)
    g2 = g_sub.reshape(1, LANES)
    in_specs = [_blocked_spec(seq, t), pl.BlockSpec((seq, LANES), lambda b, p, it: (b, p)), _blocked_spec(seq, t),
                _const_spec(bias), _const_spec(lam_params), _const_spec(g2)]
    kernel_fn = functools.partial(_diff_pipe_kernel, lambda_init=lambda_init)
    return _pipe_call(kernel_fn, "diff_attention", 3, bsz, seq, DIFF_HEADS, in_specs,
                      (q_t, k, v_t, bias, lam_params, g2), _softmax_scratch(t), False)


PAIRS_PER_GROUP = EXPERTS_PER_GROUP * (EXPERTS_PER_GROUP - 1) // 2
N_CLASSES = N_GROUPS * PAIRS_PER_GROUP


def _class_experts():
    lo, hi = [], []
    for g in range(N_GROUPS):
        for a in range(EXPERTS_PER_GROUP):
            for b in range(a + 1, EXPERTS_PER_GROUP):
                lo.append(g * EXPERTS_PER_GROUP + a)
                hi.append(g * EXPERTS_PER_GROUP + b)
    return lo, hi


def _route(logits):
    lane = lax.broadcasted_iota(jnp.int32, logits.shape, 1).astype(F32)
    big = jnp.float32(1 << 20)

    def top1(vals):
        v = jnp.max(vals, axis=1, keepdims=True)
        i = jnp.min(jnp.where(vals == v, lane, big), axis=1, keepdims=True)
        return v, i

    is_group = (lane >= N_EXPERTS) & (lane < N_EXPERTS + N_GROUPS)
    gl = jnp.where(is_group, logits, -jnp.inf)
    gmax, gidx = top1(gl)
    g_w = 1.0 / jnp.sum(jnp.exp(gl - gmax), axis=1, keepdims=True)
    first = (gidx - N_EXPERTS) * EXPERTS_PER_GROUP
    el = jnp.where((lane >= first) & (lane < first + EXPERTS_PER_GROUP), logits, -jnp.inf)
    v1, i1 = top1(el)
    v2, i2 = top1(jnp.where(lane == i1, -jnp.inf, el))
    e2 = jnp.exp(v2 - v1)
    w1 = g_w / (1.0 + e2)
    w2 = g_w * e2 / (1.0 + e2)
    group = gidx - N_EXPERTS
    a = jnp.minimum(i1, i2) - first
    b = jnp.maximum(i1, i2) - first
    pair = a * (2 * EXPERTS_PER_GROUP - 1 - a) * 0.5 + (b - a - 1.0)
    cls = group * PAIRS_PER_GROUP + pair
    w_lo = jnp.where(i1 < i2, w1, w2)
    w_hi = jnp.where(i1 < i2, w2, w1)
    return jnp.where(lane == 0, cls, jnp.where(lane == 1, w_lo, jnp.where(lane == 2, w_hi, 0.0)))


def _oproj_kernel(a_ref, wo_ref, x_ref, gt_ref, g_ref, sc_ref, sh_ref, wr_ref, br_ref, xo_ref, hx_ref):
    d = x_ref.shape[1]
    xn = x_ref[...] + gt_ref[0] * _dot(a_ref[...], wo_ref[...])
    xo_ref[...] = xn
    h = _rms(xn) * g_ref[...]
    h = h * (1.0 + sc_ref[0]) + sh_ref[0]
    hx_ref[:, :d] = h
    logits = jnp.dot(h, wr_ref[...], preferred_element_type=F32, precision=lax.Precision.HIGHEST) + br_ref[...]
    hx_ref[:, d:] = _route(logits)


def _oproj(attn, w_o, x, gt, g, sc, sh, w_route, b_route, seq):
    t, d = x.shape
    tm = ROW_TILE
    per_batch = seq // tm
    row = pl.BlockSpec((tm, d), lambda i: (i, 0))
    mod = pl.BlockSpec((1, 1, d), lambda i: (i // per_batch, 0, 0))

    def full(a):
        return pl.BlockSpec(a.shape, lambda i: (0,) * a.ndim)

    g2 = g.reshape(1, d)
    return pl.pallas_call(
        _oproj_kernel,
        grid=(t // tm,),
        in_specs=[pl.BlockSpec((tm, attn.shape[1]), lambda i: (i, 0)), full(w_o), row, mod, full(g2), mod, mod,
                  full(w_route), full(b_route)],
        out_specs=[row, pl.BlockSpec((tm, d + LANES), lambda i: (i, 0))],
        out_shape=[jax.ShapeDtypeStruct((t, d), F32), jax.ShapeDtypeStruct((t, d + LANES), F32)],
        compiler_params=_cparams(1),
        name="oproj_router",
    )(attn, w_o, x, gt, g2, sc, sh, w_route, b_route)


def _sort_kernel(route_ref, ltri_ref, utri_ref, pos_ref, counts_ref, cnt_scr, off_scr, *, tile):
    phase, i = pl.program_id(0), pl.program_id(1)
    lane = lax.broadcasted_iota(jnp.int32, route_ref.shape, 1).astype(F32)
    onehot = jnp.where(lane == route_ref[:, 0:1], 1.0, 0.0)

    @pl.when((phase == 0) & (i == 0))
    def _():
        cnt_scr[...] = jnp.zeros_like(cnt_scr)

    @pl.when((phase == 1) & (i == 0))
    def _():
        cnt = cnt_scr[...]
        counts_ref[...] = cnt
        n_tiles = jnp.floor((cnt + (tile - 1.0)) * (1.0 / tile))
        before = _dot(jnp.broadcast_to(n_tiles, (8, LANES)).astype(BF16), utri_ref[...])
        off_scr[...] = before[0:1] * tile
        cnt_scr[...] = jnp.zeros_like(cnt_scr)

    @pl.when(phase == 1)
    def _():
        earlier = _dot(ltri_ref[...], onehot.astype(BF16))
        row = jnp.sum(onehot * (earlier + off_scr[...] + cnt_scr[...]), axis=1, keepdims=True)
        pos_ref[...] = row.astype(jnp.int32)

    cnt_scr[...] += jnp.sum(onehot, axis=0, keepdims=True)


def _sorted_rows(hx, tile):
    t, w = hx.shape
    tm = ROW_TILE
    ltri = (jnp.arange(tm)[:, None] > jnp.arange(tm)[None, :]).astype(BF16)
    utri = (jnp.arange(LANES)[:, None] < jnp.arange(LANES)[None, :]).astype(BF16)
    pos, counts = pl.pallas_call(
        functools.partial(_sort_kernel, tile=tile),
        grid=(2, t // tm),
        in_specs=[pl.BlockSpec((tm, LANES), lambda ph, i: (i, w // LANES - 1)),
                  pl.BlockSpec((tm, tm), lambda ph, i: (0, 0)), pl.BlockSpec((LANES, LANES), lambda ph, i: (0, 0))],
        out_specs=[pl.BlockSpec((tm, 1), lambda ph, i: (i * ph, 0)), pl.BlockSpec((1, LANES), lambda ph, i: (0, 0))],
        out_shape=[jax.ShapeDtypeStruct((t, 1), jnp.int32), jax.ShapeDtypeStruct((1, LANES), F32)],
        scratch_shapes=[pltpu.VMEM((1, LANES), F32), pltpu.VMEM((1, LANES), F32)],
        compiler_params=_cparams(2),
        name="moe_sort",
    )(hx, ltri, utri)
    return pos.reshape(t), counts.reshape(LANES)


ROWS_PER_STEP = 2048


def _dispatch_kernel(pos_ref, hx_hbm, zeros_hbm, xs_hbm, sem):
    del zeros_hbm
    base = pl.program_id(0) * ROWS_PER_STEP

    def issue(r, c):
        tok = base + r
        pltpu.make_async_copy(hx_hbm.at[pl.ds(tok, 1)], xs_hbm.at[pl.ds(pos_ref[tok], 1)], sem).start()
        return c

    lax.fori_loop(0, ROWS_PER_STEP, issue, 0, unroll=8)
    pltpu.make_async_copy(hx_hbm.at[pl.ds(0, ROWS_PER_STEP)], xs_hbm.at[pl.ds(0, ROWS_PER_STEP)], sem).wait()


def _dispatch(pos, hx, n_rows):
    t, w = hx.shape
    any_spec = pl.BlockSpec(memory_space=pl.ANY)
    grid_spec = pltpu.PrefetchScalarGridSpec(
        num_scalar_prefetch=1, grid=(t // ROWS_PER_STEP,), in_specs=[any_spec, any_spec], out_specs=any_spec,
        scratch_shapes=[pltpu.SemaphoreType.DMA])
    return pl.pallas_call(
        _dispatch_kernel,
        grid_spec=grid_spec,
        out_shape=jax.ShapeDtypeStruct((n_rows, w), F32),
        input_output_aliases={2: 0},
        compiler_params=_cparams(1),
        name="moe_dispatch",
    )(pos, hx, jnp.zeros((n_rows, w), F32))


def _experts_kernel(lo_ref, hi_ref, used_ref, xs_ref, wgu_lo_ref, wgu_hi_ref, wd_lo_ref, wd_hi_ref, ys_ref):
    del lo_ref, hi_ref
    d = ys_ref.shape[1]

    @pl.when(used_ref[pl.program_id(0)] == 0)
    def _():
        ys_ref[...] = jnp.zeros_like(ys_ref)

    @pl.when(used_ref[pl.program_id(0)] == 1)
    def _():
        xb = xs_ref[:, :d].astype(BF16)
        route = xs_ref[:, d:]
        route_lane = lax.broadcasted_iota(jnp.int32, route.shape, 1)
        y = None
        for lane, wgu_ref, wd_ref in ((1, wgu_lo_ref, wd_lo_ref), (2, wgu_hi_ref, wd_hi_ref)):
            gu = _dot(xb, wgu_ref[0].astype(BF16))
            gate, up = gu[:, :EXPERT_FF], gu[:, EXPERT_FF:]
            act = (gate / (1.0 + jnp.exp(-gate)) * up).astype(BF16)
            gate_w = jnp.sum(jnp.where(route_lane == lane, route, 0.0), axis=1, keepdims=True)
            term = gate_w * _dot(act, wd_ref[0].astype(BF16))
            y = term if y is None else y + term
        ys_ref[...] = y


def _experts(xs, tile_lo, tile_hi, tile_used, w_gu, w_d, tile):
    n_rows, w = xs.shape
    d = w - LANES
    gu_block, d_block = (1,) + w_gu.shape[1:], (1,) + w_d.shape[1:]
    grid_spec = pltpu.PrefetchScalarGridSpec(
        num_scalar_prefetch=3,
        grid=(n_rows // tile,),
        in_specs=[pl.BlockSpec((tile, w), lambda i, lo, hi, used: (i, 0)),
                  pl.BlockSpec(gu_block, lambda i, lo, hi, used: (lo[i], 0, 0)),
                  pl.BlockSpec(gu_block, lambda i, lo, hi, used: (hi[i], 0, 0)),
                  pl.BlockSpec(d_block, lambda i, lo, hi, used: (lo[i], 0, 0)),
                  pl.BlockSpec(d_block, lambda i, lo, hi, used: (hi[i], 0, 0))],
        out_specs=pl.BlockSpec((tile, d), lambda i, lo, hi, used: (i, 0)),
    )
    return pl.pallas_call(
        _experts_kernel,
        grid_spec=grid_spec,
        out_shape=jax.ShapeDtypeStruct((n_rows, d), F32),
        compiler_params=_cparams(1),
        name="moe_experts",
    )(tile_lo, tile_hi, tile_used, xs, w_gu, w_gu, w_d, w_d)


def _combine_kernel(pos_ref, ys_hbm, x_ref, gt_ref, g_ref, o_ref, buf, sem, *, final):
    tm = x_ref.shape[0]
    base = pl.program_id(0) * tm

    def issue(r, c):
        pltpu.make_async_copy(ys_hbm.at[pl.ds(pos_ref[base + r], 1)], buf.at[pl.ds(r, 1)], sem).start()
        return c

    lax.fori_loop(0, tm, issue, 0, unroll=8)
    pltpu.make_async_copy(ys_hbm.at[pl.ds(0, tm)], buf, sem).wait()
    y = x_ref[...] + gt_ref[0] * buf[...]
    o_ref[...] = _rms(y) * g_ref[...] if final else y


def _combine(pos, ys, x, gt, g_final, seq, final):
    t, d = x.shape
    tm = ROW_TILE
    per_batch = seq // tm
    grid_spec = pltpu.PrefetchScalarGridSpec(
        num_scalar_prefetch=1,
        grid=(t // tm,),
        in_specs=[pl.BlockSpec(memory_space=pl.ANY), pl.BlockSpec((tm, d), lambda i, pos: (i, 0)),
                  pl.BlockSpec((1, 1, d), lambda i, pos: (i // per_batch, 0, 0)),
                  pl.BlockSpec((1, d), lambda i, pos: (0, 0))],
        out_specs=pl.BlockSpec((tm, d), lambda i, pos: (i, 0)),
        scratch_shapes=[pltpu.VMEM((tm, d), F32), pltpu.SemaphoreType.DMA],
    )
    return pl.pallas_call(
        functools.partial(_combine_kernel, final=final),
        grid_spec=grid_spec,
        out_shape=jax.ShapeDtypeStruct((t, d), F32),
        compiler_params=_cparams(1),
        name="moe_combine",
    )(pos, ys, x, gt, g_final.reshape(1, d))


def _moe(hx, w_gu, w_d, x, gt, g_final, seq, final):
    t = x.shape[0]
    tile = MOE_TILE
    n_tiles = t // tile + N_CLASSES
    pos, counts = _sorted_rows(hx, tile)
    tiles_per_class = jnp.ceil(counts[:N_CLASSES] / tile).astype(jnp.int32)
    ends = jnp.cumsum(tiles_per_class)
    tile_ids = jnp.arange(n_tiles, dtype=jnp.int32)
    tile_class = jnp.minimum(jnp.searchsorted(ends, tile_ids, side="right"), N_CLASSES - 1)
    tile_used = (tile_ids < ends[-1]).astype(jnp.int32)
    lo, hi = _class_experts()
    tile_lo = jnp.asarray(lo, jnp.int32)[tile_class]
    tile_hi = jnp.asarray(hi, jnp.int32)[tile_class]
    xs = _dispatch(pos, hx, n_tiles * tile)
    ys = _experts(xs, tile_lo, tile_hi, tile_used, w_gu, w_d, tile)
    return _combine(pos, ys, x, gt, g_final, seq, final)


def _final_norm_kernel(x_ref, g_ref, o_ref):
    o_ref[...] = _rms(x_ref[...]) * g_ref[...]


def _final_norm(x, g):
    t, d = x.shape
    tm = 1024
    return pl.pallas_call(
        _final_norm_kernel,
        grid=(t // tm,),
        in_specs=[pl.BlockSpec((tm, d), lambda i: (i, 0)), pl.BlockSpec((1, d), lambda i: (0, 0))],
        out_specs=pl.BlockSpec((tm, d), lambda i: (i, 0)),
        out_shape=jax.ShapeDtypeStruct((t, d), F32),
        compiler_params=_cparams(1),
        name="final_norm",
    )(x, g.reshape(1, d))


def _diff_qk_layout(w):
    d = w.shape[0]
    w = w.reshape(d, DIFF_HEADS, 2, 2, HEAD_DIM // 2)
    return w.transpose(0, 1, 3, 2, 4).reshape(d, -1)


def _mla_layouts(w_in, w_q_up, w_kv_up):
    d = w_in.shape[0]
    half = MLA_ROPE // 2
    base = MLA_Q_RANK + MLA_KV_RANK
    r1, r2 = w_in[:, base:base + half], w_in[:, base + half:]
    z = jnp.zeros((d, 64 - 2 * half), w_in.dtype)
    w_in_l = jnp.concatenate([w_in[:, :base], r1, r1, z, r2, r2, z], axis=1)

    r = w_q_up.shape[0]
    wq = w_q_up.reshape(r, MLA_HEADS, MLA_NOPE + MLA_ROPE)
    nope = wq[:, :, :MLA_NOPE].reshape(r, -1)
    q1 = wq[:, :, MLA_NOPE:MLA_NOPE + half].reshape(r, MLA_HEADS // 2, 2 * half)
    q2 = wq[:, :, MLA_NOPE + half:].reshape(r, MLA_HEADS // 2, 2 * half)
    zq = jnp.zeros((r, MLA_HEADS // 2, 64 - 2 * half), w_q_up.dtype)
    rope = jnp.concatenate([q1, zq, q2, zq], axis=2).reshape(r, -1)
    w_q_l = jnp.concatenate([nope, rope], axis=1)

    rk = w_kv_up.shape[0]
    wkv = w_kv_up.reshape(rk, MLA_HEADS, MLA_NOPE + MLA_V)
    w_kn = wkv[:, :, :MLA_NOPE].reshape(rk, -1)
    w_v = wkv[:, :, MLA_NOPE:].reshape(rk, -1)
    return w_in_l.astype(BF16), w_q_l.T.astype(BF16), w_kn.astype(BF16), w_v.T.astype(BF16)


def _router_layout(w_group, b_group, w_router, b_router):
    d = w_group.shape[0]
    pad = LANES - N_EXPERTS - N_GROUPS
    w = jnp.concatenate([w_router, w_group, jnp.zeros((d, pad), F32)], axis=1)
    b = jnp.concatenate([b_router, b_group, jnp.zeros((pad,), F32)]).reshape(1, LANES)
    return w, b


def kernel(x, c, positions, norm_mix_g, norm_ffn_g, ada_w, ada_b, sb_w_qkv, sb_w_o, mla_w_in, mla_g_q, mla_g_kv, mla_w_q_up, mla_w_kv_up, mla_w_o, diff_w_qkv, diff_lam_q1, diff_lam_k1, diff_lam_q2, diff_lam_k2, diff_g_sub, diff_w_o, moe_w_group, moe_b_group, moe_w_router, moe_b_router, moe_w_gate_up, moe_w_down, final_g):
    bsz, seq, d = x.shape
    depth = ada_w.shape[0]
    xt = x.reshape(bsz * seq, d)
    mod = _adaln(c, ada_w, ada_b)
    rope_mla = _rope_tables(positions, MLA_ROPE) if depth > 1 else None
    rope_diff = _rope_tables(positions, HEAD_DIM) if depth > 2 else None

    for i in range(depth):
        sh_m, sc_m, gt_m, sh_f, sc_f, gt_f = (mod[i, :, k * d:(k + 1) * d].reshape(bsz, 1, d) for k in range(6))
        kind, j = i % N_MIXERS, i // N_MIXERS
        if kind == 0:
            w = sb_w_qkv[j].astype(BF16)
            q_t, k, v_t = _qkv_proj(xt, norm_mix_g[i], sc_m, sh_m, w[:, :d].T, w[:, d:2 * d], w[:, 2 * d:].T, seq)
            attn = _sb_attention(q_t, k, v_t, bsz, seq)
            w_o = sb_w_o[j]
        elif kind == 1:
            w_in_l, wq_t, w_kn, wv_t = _mla_layouts(mla_w_in[j], mla_w_q_up[j], mla_w_kv_up[j])
            q_t, kn, kr, v_t = _mla_proj(xt, norm_mix_g[i], sc_m, sh_m, w_in_l, mla_g_q[j], mla_g_kv[j], wq_t, w_kn,
                                         wv_t, rope_mla, seq)
            attn = _mla_attention(q_t, kn, kr, v_t, bsz, seq)
            w_o = mla_w_o[j]
        else:
            w = diff_w_qkv[j].astype(BF16)
            q_t, k, v_t = _qkv_proj(xt, norm_mix_g[i], sc_m, sh_m, _diff_qk_layout(w[:, :d]).T,
                                    _diff_qk_layout(w[:, d:2 * d]), w[:, 2 * d:].T, seq, rope=rope_diff)
            lam_params = jnp.stack([diff_lam_q1[j], diff_lam_k1[j], diff_lam_q2[j], diff_lam_k2[j]])
            lambda_init = 0.8 - 0.6 * math.exp(-0.3 * i)
            attn = _diff_attention(q_t, k, v_t, lam_params, diff_g_sub[j], lambda_init, bsz, seq)
            w_o = diff_w_o[j]
        w_route, b_route = _router_layout(moe_w_group[i], moe_b_group[i], moe_w_router[i], moe_b_router[i])
        xt, hx = _oproj(attn, w_o.astype(BF16), xt, gt_m, norm_ffn_g[i], sc_f, sh_f, w_route, b_route, seq)
        xt = _moe(hx, moe_w_gate_up[i], moe_w_down[i], xt, gt_f, final_g, seq, final=i == depth - 1)
    return xt.reshape(bsz, seq, d)
```

```python
import functools
import math

import jax
import jax.numpy as jnp
from jax import lax
from jax.experimental import pallas as pl
from jax.experimental.pallas import tpu as pltpu

F32 = jnp.float32
BF16 = jnp.bfloat16

N_MIXERS = 3
ROPE_THETA = 10000.0
EPS = 1e-6
HEAD_DIM = 64
SB_HEADS = 16
MLA_HEADS = 16
MLA_Q_RANK = 384
MLA_KV_RANK = 256
MLA_NOPE = 64
MLA_ROPE = 32
MLA_V = 64
DIFF_HEADS = 8
N_GROUPS = 4
EXPERTS_PER_GROUP = 4
N_EXPERTS = N_GROUPS * EXPERTS_PER_GROUP
EXPERT_FF = 512

LANES = 128
LOG2E = 1.4426950408889634
NEG_BIG = -1e30
VMEM_LIMIT = 56 * 1024 * 1024

ROW_TILE = 512
ATT_TILE = 256
MOE_TILE = 256
PROJ_CHUNK = 256


def _cparams(n_axes):
    return pltpu.CompilerParams(dimension_semantics=("arbitrary",) * n_axes, vmem_limit_bytes=VMEM_LIMIT)


def _dot(a, b):
    return jnp.dot(a, b, preferred_element_type=F32)


def _dot_nt(a, b):
    return lax.dot_general(a, b, (((1,), (1,)), ((), ())), preferred_element_type=F32)


def _rms(x):
    return x * lax.rsqrt(jnp.mean(x * x, axis=-1, keepdims=True) + EPS)


def _rope_blocks(y, cos, sin_signed):
    out = []
    for j in range(y.shape[1] // LANES):
        yb = y[:, j * LANES:(j + 1) * LANES]
        out.append(yb * cos + pltpu.roll(yb, 64, 1) * sin_signed)
    return out[0] if len(out) == 1 else jnp.concatenate(out, axis=1)


def _rope_rows(y, cos_t, sin_t):
    out = []
    for j in range(y.shape[0] // LANES):
        yb = y[j * LANES:(j + 1) * LANES]
        out.append(yb * cos_t + jnp.concatenate([yb[64:], yb[:64]], axis=0) * sin_t)
    return out[0] if len(out) == 1 else jnp.concatenate(out, axis=0)


def _adaln_kernel(c_ref, w_ref, b_ref, o_ref):
    c = c_ref[...]
    ca = c / (1.0 + jnp.exp(-c))
    o_ref[0] = jnp.dot(ca, w_ref[0], preferred_element_type=F32, precision=lax.Precision.HIGHEST) + b_ref[0]


def _adaln(c, ada_w, ada_b):
    depth, d, n = ada_w.shape
    bsz = c.shape[0]
    rows = 8
    cp = jnp.zeros((rows, d), F32).at[:bsz].set(c)
    tn = 1536
    out = pl.pallas_call(
        _adaln_kernel,
        grid=(depth, n // tn),
        in_specs=[
            pl.BlockSpec((rows, d), lambda i, j: (0, 0)),
            pl.BlockSpec((1, d, tn), lambda i, j: (i, 0, j)),
            pl.BlockSpec((1, 1, tn), lambda i, j: (i, 0, j)),
        ],
        out_specs=pl.BlockSpec((1, rows, tn), lambda i, j: (i, 0, j)),
        out_shape=jax.ShapeDtypeStruct((depth, rows, n), F32),
        compiler_params=_cparams(2),
        name="adaln_mod",
    )(cp, ada_w, ada_b.reshape(depth, 1, n))
    return out[:, :bsz]


def _rope_table_kernel(pos_c_ref, pos_r_ref, invf_r_ref, sign_r_ref, invf_c_ref, sign_c_ref,
                       cos_ref, sin_ref, cost_ref, sint_ref):
    ang = pos_c_ref[...].astype(F32) * invf_r_ref[...]
    cos_ref[...] = jnp.cos(ang)
    sin_ref[...] = jnp.sin(ang) * sign_r_ref[...]
    ang_t = invf_c_ref[...] * pos_r_ref[...].astype(F32)
    cost_ref[...] = jnp.cos(ang_t)
    sint_ref[...] = jnp.sin(ang_t) * sign_c_ref[...]


def _rope_tables(positions, dim):
    t = positions.size
    half = dim // 2
    inv_freq = ROPE_THETA ** (-jnp.arange(0, dim, 2, dtype=F32) / dim)
    invf = jnp.tile(inv_freq, LANES // half)
    sign = jnp.where(jnp.arange(LANES) < 64, -1.0, 1.0).astype(F32)
    tm = 2048
    small_r = pl.BlockSpec((1, LANES), lambda i: (0, 0))
    small_c = pl.BlockSpec((LANES, 1), lambda i: (0, 0))
    return pl.pallas_call(
        _rope_table_kernel,
        grid=(t // tm,),
        in_specs=[pl.BlockSpec((tm, 1), lambda i: (i, 0)), pl.BlockSpec((1, tm), lambda i: (0, i)),
                  small_r, small_r, small_c, small_c],
        out_specs=[pl.BlockSpec((tm, LANES), lambda i: (i, 0))] * 2 + [pl.BlockSpec((LANES, tm), lambda i: (0, i))] * 2,
        out_shape=[jax.ShapeDtypeStruct((t, LANES), F32)] * 2 + [jax.ShapeDtypeStruct((LANES, t), F32)] * 2,
        compiler_params=_cparams(1),
        name="rope_tables",
    )(positions.reshape(t, 1), positions.reshape(1, t), invf.reshape(1, LANES), sign.reshape(1, LANES),
      invf.reshape(LANES, 1), sign.reshape(LANES, 1))


def _modulated_norm(x_ref, g_ref, sc_ref, sh_ref):
    h = _rms(x_ref[...]) * g_ref[...]
    return h * (1.0 + sc_ref[0]) + sh_ref[0]


def _store_token_blocks(o_ref, w_t_ref, hb, post=None):
    tb = o_ref.shape[2]
    for r in range(0, w_t_ref.shape[0], PROJ_CHUNK):
        y = _dot_nt(w_t_ref[r:r + PROJ_CHUNK, :], hb)
        if post is not None:
            y = post(y, r)
        y = y.astype(BF16)
        for c in range(o_ref.shape[0]):
            o_ref[c, r:r + PROJ_CHUNK, :] = y[:, c * tb:(c + 1) * tb]


def _qkv_kernel(*refs, rope, q_scale):
    if rope:
        x_ref, g_ref, sc_ref, sh_ref, wqt_ref, wk_ref, wvt_ref, cos_ref, sin_ref, cost_ref, sint_ref = refs[:11]
    else:
        x_ref, g_ref, sc_ref, sh_ref, wqt_ref, wk_ref, wvt_ref = refs[:7]
    q_ref, k_ref, v_ref = refs[-3:]
    hb = _modulated_norm(x_ref, g_ref, sc_ref, sh_ref).astype(BF16)

    def q_post(qt, r):
        if rope:
            qt = _rope_rows(qt, cost_ref[...], sint_ref[...])
        return qt * q_scale

    _store_token_blocks(q_ref, wqt_ref, hb, q_post)
    for r in range(0, wk_ref.shape[1], 512):
        k = _dot(hb, wk_ref[:, r:r + 512])
        if rope:
            k = _rope_blocks(k, cos_ref[...], sin_ref[...])
        k_ref[:, r:r + 512] = k.astype(BF16)
    _store_token_blocks(v_ref, wvt_ref, hb)


def _qkv_proj(x, g, sc, sh, wq_t, wk, wv_t, seq, rope=None):
    t, d = x.shape
    tm, tk = ROW_TILE, ATT_TILE
    per_batch = seq // tm
    row = pl.BlockSpec((tm, d), lambda i: (i, 0))
    mod = pl.BlockSpec((1, 1, d), lambda i: (i // per_batch, 0, 0))

    def full(a):
        return pl.BlockSpec(a.shape, lambda i: (0,) * a.ndim)

    g2 = g.reshape(1, d)
    in_specs = [row, full(g2), mod, mod, full(wq_t), full(wk), full(wv_t)]
    args = [x, g2, sc, sh, wq_t, wk, wv_t]
    if rope is not None:
        tab = pl.BlockSpec((tm, LANES), lambda i: (i, 0))
        tab_t = pl.BlockSpec((LANES, tm), lambda i: (0, i))
        in_specs += [tab, tab, tab_t, tab_t]
        args += list(rope)
    nq, nk, nv = wq_t.shape[0], wk.shape[1], wv_t.shape[0]
    return pl.pallas_call(
        functools.partial(_qkv_kernel, rope=rope is not None, q_scale=HEAD_DIM ** -0.5),
        grid=(t // tm,),
        in_specs=in_specs,
        out_specs=[pl.BlockSpec((tm // tk, nq, tk), lambda i: (i, 0, 0)), pl.BlockSpec((tm, nk), lambda i: (i, 0)),
                   pl.BlockSpec((tm // tk, nv, tk), lambda i: (i, 0, 0))],
        out_shape=[jax.ShapeDtypeStruct((t // tk, nq, tk), BF16), jax.ShapeDtypeStruct((t, nk), BF16),
                   jax.ShapeDtypeStruct((t // tk, nv, tk), BF16)],
        compiler_params=_cparams(1),
        name="qkv_proj",
    )(*args)


def _mla_proj_kernel(x_ref, g_ref, sc_ref, sh_ref, win_ref, gq_ref, gkv_ref, wqt_ref, wkn_ref, wvt_ref,
                     cos_ref, sin_ref, cost_ref, sint_ref, q_ref, kn_ref, kr_ref, v_ref):
    hb = _modulated_norm(x_ref, g_ref, sc_ref, sh_ref).astype(BF16)
    lat = _dot(hb, win_ref[...])
    cq = (_rms(lat[:, :MLA_Q_RANK]) * gq_ref[...]).astype(BF16)
    ckv = (_rms(lat[:, MLA_Q_RANK:MLA_Q_RANK + MLA_KV_RANK]) * gkv_ref[...]).astype(BF16)
    kr_ref[...] = _rope_blocks(lat[:, MLA_Q_RANK + MLA_KV_RANK:], cos_ref[...], sin_ref[...]).astype(BF16)
    n_nope = MLA_HEADS * MLA_NOPE
    _store_token_blocks(q_ref, wqt_ref, cq,
                        lambda qt, r: _rope_rows(qt, cost_ref[...], sint_ref[...]) if r >= n_nope else qt)
    kn_ref[...] = _dot(ckv, wkn_ref[...]).astype(BF16)
    _store_token_blocks(v_ref, wvt_ref, ckv)


def _mla_proj(x, g, sc, sh, w_in, g_q, g_kv, wq_t, wkn, wv_t, rope, seq):
    t, d = x.shape
    tm, tk = ROW_TILE, ATT_TILE
    per_batch = seq // tm
    row = pl.BlockSpec((tm, d), lambda i: (i, 0))
    mod = pl.BlockSpec((1, 1, d), lambda i: (i // per_batch, 0, 0))

    def full(a):
        return pl.BlockSpec(a.shape, lambda i: (0,) * a.ndim)

    tab = pl.BlockSpec((tm, LANES), lambda i: (i, 0))
    tab_t = pl.BlockSpec((LANES, tm), lambda i: (0, i))
    g2, gq2, gkv2 = g.reshape(1, d), g_q.reshape(1, -1), g_kv.reshape(1, -1)
    nq, nk, nv = wq_t.shape[0], wkn.shape[1], wv_t.shape[0]
    return pl.pallas_call(
        _mla_proj_kernel,
        grid=(t // tm,),
        in_specs=[row, full(g2), mod, mod, full(w_in), full(gq2), full(gkv2), full(wq_t), full(wkn), full(wv_t),
                  tab, tab, tab_t, tab_t],
        out_specs=[pl.BlockSpec((tm // tk, nq, tk), lambda i: (i, 0, 0)), pl.BlockSpec((tm, nk), lambda i: (i, 0)),
                   pl.BlockSpec((tm, LANES), lambda i: (i, 0)), pl.BlockSpec((tm // tk, nv, tk), lambda i: (i, 0, 0))],
        out_shape=[jax.ShapeDtypeStruct((t // tk, nq, tk), BF16), jax.ShapeDtypeStruct((t, nk), BF16),
                   jax.ShapeDtypeStruct((t, LANES), BF16), jax.ShapeDtypeStruct((t // tk, nv, tk), BF16)],
        compiler_params=_cparams(1),
        name="mla_proj",
    )(x, g2, sc, sh, w_in, gq2, gkv2, wq_t, wkn, wv_t, *rope)


def _keep_rows(x, keep):
    n_rows, n = x.shape
    parts, pos = [], 0
    for a, b in keep:
        if a > pos:
            parts.append(jnp.zeros((a - pos, n), x.dtype))
        parts.append(x[a:b])
        pos = b
    if pos < n_rows:
        parts.append(jnp.zeros((n_rows - pos, n), x.dtype))
    return jnp.concatenate(parts, axis=0)


def _causal_valid(t, strict):
    key = lax.broadcasted_iota(jnp.int32, (t, t), 0)
    query = lax.broadcasted_iota(jnp.int32, (t, t), 1)
    return key < query if strict else key <= query


def _rem3(step):
    return step % 3 if isinstance(step, int) else lax.rem(step, 3)


def _sb_flash_kernel(qt_ref, k_ref, vt_ref, lmat_ref, o_ref, z_scr, cs_scr, acc_ref, carry_ref):
    t = qt_ref.shape[1]
    qi = pl.program_id(2)
    qt = qt_ref[...]
    qt_heads = (_keep_rows(qt, [(0, HEAD_DIM)]), _keep_rows(qt, [(HEAD_DIM, 2 * HEAD_DIM)]))
    lmat = lmat_ref[...]

    acc_ref[...] = jnp.zeros_like(acc_ref)
    carry_ref[...] = jnp.zeros_like(carry_ref)
    valid = _causal_valid(t, True)

    def block_of(step):
        return jnp.maximum(qi - step, 0)

    def scores(step):
        kb = k_ref[pl.ds(pl.multiple_of(block_of(step) * t, t), t), :]
        for hd in range(2):
            z_scr[_rem3(step), hd] = _dot(kb, qt_heads[hd])

    def sums(step, masked=False):
        for hd in range(2):
            z = z_scr[_rem3(step), hd]
            sp = jnp.maximum(z, 0.0) + jnp.log(1.0 + jnp.exp2(jnp.abs(z) * -LOG2E))
            if masked:
                sp = jnp.where(valid, sp, 0.0)
            cs_scr[step & 1, hd] = _dot(lmat, sp.astype(BF16))

    def values(step, masked=False):
        vb = vt_ref[block_of(step)]
        for hd in range(2):
            cs = cs_scr[step & 1, hd]
            w = jnp.exp(z_scr[_rem3(step), hd] - cs - carry_ref[hd])
            if masked:
                w = jnp.where(valid, w, 0.0)
            acc_ref[hd] += _dot(vb, w.astype(BF16))
            carry_ref[hd] += cs[0:1, :]

    def iteration(i, first=False):
        if not first or i >= 2:
            values(i - 2, masked=first)
        if not first or i >= 1:
            sums(i - 1, masked=first and i == 1)
        scores(i)

    for i in range(3):
        iteration(i, first=True)

    def body(i, c):
        iteration(i)
        return c

    lax.fori_loop(3, qi + 3, body, 0)
    o_t = jnp.concatenate([acc_ref[0, :HEAD_DIM], acc_ref[1, HEAD_DIM:]], axis=0)
    o_ref[...] = o_t.T.astype(o_ref.dtype)


def _softmax_sweep(qi, t, qt_heads, k_ref, vt_ref, scr, c_log2):
    s_scr, p_scr, alpha_scr, m_ref, l_ref, acc_ref = scr
    n = qi + 1
    m_ref[...] = jnp.full_like(m_ref, NEG_BIG)
    l_ref[...] = jnp.zeros_like(l_ref)
    acc_ref[...] = jnp.zeros_like(acc_ref)
    p_scr[1] = jnp.zeros(p_scr.shape[1:], p_scr.dtype)
    alpha_scr[1] = jnp.ones(alpha_scr.shape[1:], alpha_scr.dtype)

    def scores(j):
        kb = k_ref[pl.ds(pl.multiple_of(j * t, t), t), :]
        for hd in range(2):
            s_scr[j & 1, hd] = _dot(kb, qt_heads[hd]) * c_log2

    rows = 32

    def softmax(j, masked):
        slot = j & 1
        for hd in range(2):
            def chunk(r):
                s = s_scr[slot, hd, r:r + rows, :]
                if masked:
                    key = lax.broadcasted_iota(jnp.int32, (rows, t), 0) + r
                    query = lax.broadcasted_iota(jnp.int32, (rows, t), 1)
                    s = jnp.where(key <= query, s, -jnp.inf)
                return s

            mx = chunk(0)
            for r in range(rows, t, rows):
                mx = jnp.maximum(mx, chunk(r))
            m_old = m_ref[hd]
            m_new = jnp.maximum(m_old, jnp.max(mx, axis=0, keepdims=True))
            alpha = jnp.exp2(m_old - m_new)
            psum = None
            for r in range(0, t, rows):
                p = jnp.exp2(chunk(r) - m_new)
                p_scr[slot, hd, r:r + rows, :] = p.astype(BF16)
                psum = p if psum is None else psum + p
            l_ref[hd] = alpha * l_ref[hd] + jnp.sum(psum, axis=0, keepdims=True)
            m_ref[hd] = m_new
            alpha_scr[slot, hd] = alpha

    def values(j):
        slot = j & 1
        vb = vt_ref[jnp.maximum(j, 0)]
        for hd in range(2):
            acc_ref[hd] = alpha_scr[slot, hd] * acc_ref[hd] + _dot(vb, p_scr[slot, hd])

    scores(0)

    def body(i, c):
        values(i - 1)
        softmax(i, False)
        scores(i + 1)
        return c

    lax.fori_loop(0, n - 1, body, 0)
    values(n - 2)
    softmax(n - 1, True)
    values(n - 1)


def _mla_flash_kernel(qn_ref, qr_ref, kn_ref, kr_ref, vt_ref, o_ref, kcat_ref, *scr, scale):
    l_ref, acc_ref = scr[-2:]
    t = qn_ref.shape[1]
    qi = pl.program_id(2)

    @pl.when(qi == 0)
    def _():
        kcat_ref[:, :LANES] = kn_ref[...]
        kcat_ref[:, LANES:] = kr_ref[...]

    qn = qn_ref[...]
    qr = qr_ref[...]
    half = MLA_ROPE // 2
    qt_heads = (
        jnp.concatenate([_keep_rows(qn, [(0, MLA_NOPE)]), _keep_rows(qr, [(0, half), (64, 64 + half)])], axis=0),
        jnp.concatenate([_keep_rows(qn, [(MLA_NOPE, 2 * MLA_NOPE)]),
                         _keep_rows(qr, [(half, 2 * half), (64 + half, 64 + 2 * half)])], axis=0),
    )
    _softmax_sweep(qi, t, qt_heads, kcat_ref, vt_ref, scr, scale * LOG2E)
    o_t = jnp.concatenate([acc_ref[0, :MLA_V] / l_ref[0], acc_ref[1, MLA_V:] / l_ref[1]], axis=0)
    o_ref[...] = o_t.T.astype(o_ref.dtype)


def _diff_flash_kernel(qt_ref, k_ref, vt_ref, lam_ref, gsub_ref, o_ref, *scr, lambda_init):
    l_ref, acc_ref = scr[-2:]
    t = qt_ref.shape[1]
    qi = pl.program_id(2)
    qt = qt_ref[...]
    qt_heads = (_keep_rows(qt, [(0, 32), (64, 96)]), _keep_rows(qt, [(32, 64), (96, 128)]))
    _softmax_sweep(qi, t, qt_heads, k_ref, vt_ref, scr, LOG2E)
    lp = lam_ref[...]
    lam = (jnp.exp(jnp.sum(lp[0:1] * lp[1:2], axis=1, keepdims=True))
           - jnp.exp(jnp.sum(lp[2:3] * lp[3:4], axis=1, keepdims=True)) + lambda_init)
    o_t = acc_ref[0] / l_ref[0] - lam * (acc_ref[1] / l_ref[1])
    o_t = o_t * lax.rsqrt(jnp.mean(o_t * o_t, axis=0, keepdims=True) + EPS)
    o_ref[...] = (o_t.T * gsub_ref[...] * (1.0 - lambda_init)).astype(o_ref.dtype)


def _att_scratch(t):
    return [pltpu.VMEM((2, 2, t, t), F32), pltpu.VMEM((2, 2, t, t), BF16), pltpu.VMEM((2, 2, 1, t), F32),
            pltpu.VMEM((2, 1, t), F32), pltpu.VMEM((2, 1, t), F32), pltpu.VMEM((2, LANES, t), F32)]


def _att_specs(t, nq, seq, q_blocks):
    q = [pl.BlockSpec((LANES, t), functools.partial(lambda b, p, i, off: (off + p, b * nq + i), off=off))
         for off in q_blocks]
    k = pl.BlockSpec((seq, LANES), lambda b, p, i: (b, p))
    v = pl.BlockSpec((seq // t, LANES, t), lambda b, p, i: (b, p, 0))
    return q, k, v


def _sb_attention(q_t, k, v_t, bsz, seq):
    t = ATT_TILE
    nq = seq // t
    pairs = SB_HEADS // 2
    lmat = (jnp.arange(t)[None, :] >= jnp.arange(t)[:, None]).astype(BF16)
    (q_spec,), k_spec, v_spec = _att_specs(t, nq, seq, [0])
    return pl.pallas_call(
        _sb_flash_kernel,
        grid=(bsz, pairs, nq),
        in_specs=[q_spec, k_spec, v_spec, pl.BlockSpec((t, t), lambda b, p, i: (0, 0))],
        out_specs=pl.BlockSpec((t, LANES), lambda b, p, i: (b * nq + i, p)),
        out_shape=jax.ShapeDtypeStruct((bsz * seq, pairs * LANES), BF16),
        scratch_shapes=[pltpu.VMEM((3, 2, t, t), F32), pltpu.VMEM((2, 2, t, t), F32),
                        pltpu.VMEM((2, LANES, t), F32), pltpu.VMEM((2, 1, t), F32)],
        compiler_params=_cparams(3),
        name="sb_attention",
    )(q_t, k, v_t, lmat)


def _mla_attention(q_t, kn, kr, v_t, bsz, seq):
    t = ATT_TILE
    nq = seq // t
    pairs = MLA_HEADS // 2
    (qn_spec, qr_spec), kn_spec, v_spec = _att_specs(t, nq, seq, [0, pairs])
    return pl.pallas_call(
        functools.partial(_mla_flash_kernel, scale=(MLA_NOPE + MLA_ROPE) ** -0.5),
        grid=(bsz, pairs, nq),
        in_specs=[qn_spec, qr_spec, kn_spec, pl.BlockSpec((seq, LANES), lambda b, p, i: (b, 0)), v_spec],
        out_specs=pl.BlockSpec((t, LANES), lambda b, p, i: (b * nq + i, p)),
        out_shape=jax.ShapeDtypeStruct((bsz * seq, pairs * LANES), BF16),
        scratch_shapes=[pltpu.VMEM((seq, 2 * LANES), BF16)] + _att_scratch(t),
        compiler_params=_cparams(3),
        name="mla_attention",
    )(q_t, q_t, kn, kr, v_t)


def _diff_attention(q_t, k, v_t, lam_params, g_sub, lambda_init, bsz, seq):
    t = ATT_TILE
    nq = seq // t
    (q_spec,), k_spec, v_spec = _att_specs(t, nq, seq, [0])
    return pl.pallas_call(
        functools.partial(_diff_flash_kernel, lambda_init=lambda_init),
        grid=(bsz, DIFF_HEADS, nq),
        in_specs=[q_spec, k_spec, v_spec, pl.BlockSpec(lam_params.shape, lambda b, p, i: (0, 0)),
                  pl.BlockSpec((1, LANES), lambda b, p, i: (0, 0))],
        out_specs=pl.BlockSpec((t, LANES), lambda b, p, i: (b * nq + i, p)),
        out_shape=jax.ShapeDtypeStruct((bsz * seq, DIFF_HEADS * LANES), BF16),
        scratch_shapes=_att_scratch(t),
        compiler_params=_cparams(3),
        name="diff_attention",
    )(q_t, k, v_t, lam_params, g_sub.reshape(1, LANES))


def _attention_items(nq, depth, per_iter):
    rows = [(qi, qi - s, int(s == 0), int(s == qi), qi & 1, qi & 3) for qi in range(nq) for s in range(qi + 1)]
    rows += [(0, 0, 1, 0, nq & 1, nq & 3)] * (-len(rows) % per_iter + (depth - 1) * per_iter)
    return jnp.asarray(list(zip(*rows)), jnp.int32)


def _diag_bias(t, strict):
    key = jnp.arange(t)[:, None]
    query = jnp.arange(t)[None, :]
    masked = key >= query if strict else key > query
    return jnp.stack([jnp.zeros((t, t), F32), jnp.where(masked, NEG_BIG, 0.0).astype(F32)])


def _run_pipeline(stages, n_items, finalize, per_iter):
    depth = len(stages)
    ITEMS_PER_ITER = per_iter

    def iteration(it, static):
        for k in reversed(range(depth)):
            if static and it < k:
                continue
            for u in range(ITEMS_PER_ITER):
                stages[k]((it - k) * ITEMS_PER_ITER + u)

    for it in range(depth - 1):
        iteration(it, True)

    def body(it, c):
        iteration(it, False)
        for u in range(ITEMS_PER_ITER):
            finalize((it - (depth - 1)) * ITEMS_PER_ITER + u)
        return c

    lax.fori_loop(depth - 1, n_items // ITEMS_PER_ITER, body, 0)


def _store_tile(o_ref, qi, o_t):
    t = o_t.shape[1]
    o_ref[pl.ds(pl.multiple_of(qi * t, t), t), :] = o_t.T.astype(o_ref.dtype)


def _sb_pipe_kernel(items_ref, q_ref, k_ref, vt_ref, lmat_ref, bias_ref, o_ref,
                    z_scr, sp_scr, cs_scr, w_scr, acc_ref, carry_ref):
    t = z_scr.shape[-1]
    lmat = lmat_ref[...]
    acc_ref[...] = jnp.zeros_like(acc_ref)
    carry_ref[...] = jnp.zeros_like(carry_ref)
    STAGING_SLOTS = sp_scr.shape[0]

    def scores(w):
        qi, j, first = items_ref[0, w], items_ref[1, w], items_ref[2, w]
        kb = k_ref[pl.ds(pl.multiple_of(j * t, t), t), :]
        qt = q_ref[qi]
        bias = bias_ref[first]
        for hd, rows in enumerate(([(0, HEAD_DIM)], [(HEAD_DIM, 2 * HEAD_DIM)])):
            z_scr[w & (2 * STAGING_SLOTS - 1), hd] = _dot(kb, _keep_rows(qt, rows)) + bias

    def softplus(w):
        for hd in range(2):
            z = z_scr[w & (2 * STAGING_SLOTS - 1), hd]
            sp = jnp.maximum(z, 0.0) + jnp.log(1.0 + jnp.exp2(jnp.abs(z) * -LOG2E))
            sp_scr[w & (STAGING_SLOTS - 1), hd] = sp.astype(BF16)

    def sums(w):
        for hd in range(2):
            cs_scr[w & (STAGING_SLOTS - 1), hd] = _dot(lmat, sp_scr[w & (STAGING_SLOTS - 1), hd])

    def weights(w):
        first = items_ref[2, w] == 1
        for hd in range(2):
            cs = cs_scr[w & (STAGING_SLOTS - 1), hd]
            carry = jnp.where(first, 0.0, carry_ref[hd])
            w_scr[w & (STAGING_SLOTS - 1), hd] = jnp.exp(z_scr[w & (2 * STAGING_SLOTS - 1), hd] - cs - carry).astype(BF16)
            carry_ref[hd] = carry + cs[0:1, :]

    def values(w):
        vb = vt_ref[items_ref[1, w]]
        keep = jnp.where(items_ref[2, w] == 1, 0.0, 1.0)
        par = items_ref[4, w]
        for hd in range(2):
            acc_ref[par, hd] = keep * acc_ref[par, hd] + _dot(vb, w_scr[w & (STAGING_SLOTS - 1), hd])

    def finalize(w):
        @pl.when(items_ref[3, w] == 1)
        def _():
            par = items_ref[4, w]
            o_t = jnp.concatenate([acc_ref[par, 0, :HEAD_DIM], acc_ref[par, 1, HEAD_DIM:]], axis=0)
            _store_tile(o_ref, items_ref[0, w], o_t)

    _run_pipeline([scores, softplus, sums, weights, values], items_ref.shape[1], finalize, STAGING_SLOTS // 2)


SOFTMAX_ROWS = 32


def _softmax_pipeline(items_ref, q_heads_of, k_ref, vt_ref, bias_ref, scr, c_log2, finalize_tile):
    s_scr, p_scr, alpha_scr, lfin_scr, m_ref, l_ref, acc_ref = scr
    t = s_scr.shape[-1]
    STAGING_SLOTS = s_scr.shape[0]
    m_ref[...] = jnp.zeros_like(m_ref)
    l_ref[...] = jnp.zeros_like(l_ref)
    acc_ref[...] = jnp.zeros_like(acc_ref)

    def scores(w):
        qi, j, first = items_ref[0, w], items_ref[1, w], items_ref[2, w]
        kb = k_ref[pl.ds(pl.multiple_of(j * t, t), t), :]
        bias = bias_ref[first]
        for hd, qt in enumerate(q_heads_of(qi)):
            s_scr[w & (STAGING_SLOTS - 1), hd] = _dot(kb, qt) * c_log2 + bias

    def softmax(w):
        slot = w & (STAGING_SLOTS - 1)
        first = items_ref[2, w] == 1
        for hd in range(2):
            mx = s_scr[slot, hd, 0:SOFTMAX_ROWS, :]
            for r in range(SOFTMAX_ROWS, t, SOFTMAX_ROWS):
                mx = jnp.maximum(mx, s_scr[slot, hd, r:r + SOFTMAX_ROWS, :])
            m_old = jnp.where(first, NEG_BIG, m_ref[hd])
            m_new = jnp.maximum(m_old, jnp.max(mx, axis=0, keepdims=True))
            alpha = jnp.where(first, 0.0, jnp.exp2(m_old - m_new))
            psum = None
            for r in range(0, t, SOFTMAX_ROWS):
                p = jnp.exp2(s_scr[slot, hd, r:r + SOFTMAX_ROWS, :] - m_new)
                p_scr[slot, hd, r:r + SOFTMAX_ROWS, :] = p.astype(BF16)
                psum = p if psum is None else psum + p
            l_new = alpha * l_ref[hd] + jnp.sum(psum, axis=0, keepdims=True)
            l_ref[hd] = l_new
            m_ref[hd] = m_new
            alpha_scr[slot, hd] = alpha
            lfin_scr[items_ref[5, w], hd] = l_new

    def values(w):
        slot = w & (STAGING_SLOTS - 1)
        vb = vt_ref[items_ref[1, w]]
        par = items_ref[4, w]
        for hd in range(2):
            acc_ref[par, hd] = alpha_scr[slot, hd] * acc_ref[par, hd] + _dot(vb, p_scr[slot, hd])

    def finalize(w):
        @pl.when(items_ref[3, w] == 1)
        def _():
            finalize_tile(items_ref[0, w], acc_ref.at[items_ref[4, w]], lfin_scr[items_ref[5, w]])

    _run_pipeline([scores, softmax, values], items_ref.shape[1], finalize, STAGING_SLOTS // 2)


def _mla_pipe_kernel(items_ref, qn_ref, qr_ref, kn_ref, kr_ref, vt_ref, bias_ref, o_ref, kcat_ref, *scr, scale):
    kcat_ref[:, :LANES] = kn_ref[...]
    kcat_ref[:, LANES:] = kr_ref[...]
    half = MLA_ROPE // 2

    def q_heads_of(qi):
        qn = qn_ref[qi]
        qr = qr_ref[qi]
        return (
            jnp.concatenate([_keep_rows(qn, [(0, MLA_NOPE)]), _keep_rows(qr, [(0, half), (64, 64 + half)])], axis=0),
            jnp.concatenate([_keep_rows(qn, [(MLA_NOPE, 2 * MLA_NOPE)]),
                             _keep_rows(qr, [(half, 2 * half), (64 + half, 64 + 2 * half)])], axis=0),
        )

    def finalize_tile(qi, acc_ref, l_fin):
        o_t = jnp.concatenate([acc_ref[0, :MLA_V] / l_fin[0], acc_ref[1, MLA_V:] / l_fin[1]], axis=0)
        _store_tile(o_ref, qi, o_t)

    _softmax_pipeline(items_ref, q_heads_of, kcat_ref, vt_ref, bias_ref, scr, scale * LOG2E, finalize_tile)


def _diff_pipe_kernel(items_ref, q_ref, k_ref, vt_ref, bias_ref, lam_ref, gsub_ref, o_ref, *scr, lambda_init):
    def q_heads_of(qi):
        qt = q_ref[qi]
        return _keep_rows(qt, [(0, 32), (64, 96)]), _keep_rows(qt, [(32, 64), (96, 128)])

    def finalize_tile(qi, acc_ref, l_fin):
        lp = lam_ref[...]
        lam = (jnp.exp(jnp.sum(lp[0:1] * lp[1:2], axis=1, keepdims=True))
               - jnp.exp(jnp.sum(lp[2:3] * lp[3:4], axis=1, keepdims=True)) + lambda_init)
        o_t = acc_ref[0] / l_fin[0] - lam * (acc_ref[1] / l_fin[1])
        o_t = o_t * lax.rsqrt(jnp.mean(o_t * o_t, axis=0, keepdims=True) + EPS)
        t = o_t.shape[1]
        o = o_t.T * gsub_ref[...] * (1.0 - lambda_init)
        o_ref[pl.ds(pl.multiple_of(qi * t, t), t), :] = o.astype(o_ref.dtype)

    _softmax_pipeline(items_ref, q_heads_of, k_ref, vt_ref, bias_ref, scr, LOG2E, finalize_tile)


SOFTMAX_ITEMS_PER_ITER = 2
SB_ITEMS_PER_ITER = 1


def _softmax_scratch(t):
    n = 2 * SOFTMAX_ITEMS_PER_ITER
    return [pltpu.VMEM((n, 2, t, t), F32), pltpu.VMEM((n, 2, t, t), BF16), pltpu.VMEM((n, 2, 1, t), F32),
            pltpu.VMEM((4, 2, 1, t), F32), pltpu.VMEM((2, 1, t), F32), pltpu.VMEM((2, 1, t), F32),
            pltpu.VMEM((2, 2, LANES, t), F32)]


def _pipe_call(kernel_fn, name, depth, bsz, seq, pairs, in_specs, args, scratch, per_iter):
    t = ATT_TILE
    items = _attention_items(seq // t, depth, per_iter)
    grid_spec = pltpu.PrefetchScalarGridSpec(
        num_scalar_prefetch=1,
        grid=(bsz, pairs),
        in_specs=in_specs,
        out_specs=pl.BlockSpec((seq, LANES), lambda b, p, it: (b, p)),
        scratch_shapes=scratch,
    )
    return pl.pallas_call(
        kernel_fn,
        grid_spec=grid_spec,
        out_shape=jax.ShapeDtypeStruct((bsz * seq, pairs * LANES), BF16),
        compiler_params=_cparams(2),
        name=name,
    )(items, *args)


def _blocked_spec(seq, t, offset=0):
    return pl.BlockSpec((seq // t, LANES, t), lambda b, p, it: (b, offset + p, 0))


def _const_spec(a):
    return pl.BlockSpec(a.shape, lambda b, p, it: (0,) * a.ndim)


def _sb_attention(q_t, k, v_t, bsz, seq):
    t = ATT_TILE
    lmat = (jnp.arange(t)[None, :] >= jnp.arange(t)[:, None]).astype(BF16)
    bias = _diag_bias(t, True)
    in_specs = [_blocked_spec(seq, t), pl.BlockSpec((seq, LANES), lambda b, p, it: (b, p)), _blocked_spec(seq, t),
                _const_spec(lmat), _const_spec(bias)]
    n = 2 * SB_ITEMS_PER_ITER
    scratch = [pltpu.VMEM((2 * n, 2, t, t), F32), pltpu.VMEM((n, 2, t, t), BF16), pltpu.VMEM((n, 2, t, t), F32),
               pltpu.VMEM((n, 2, t, t), BF16), pltpu.VMEM((2, 2, LANES, t), F32), pltpu.VMEM((2, 1, t), F32)]
    return _pipe_call(_sb_pipe_kernel, "sb_attention", 5, bsz, seq, SB_HEADS // 2, in_specs,
                      (q_t, k, v_t, lmat, bias), scratch, SB_ITEMS_PER_ITER)


def _mla_attention(q_t, kn, kr, v_t, bsz, seq):
    t = ATT_TILE
    pairs = MLA_HEADS // 2
    bias = _diag_bias(t, False)
    in_specs = [_blocked_spec(seq, t), _blocked_spec(seq, t, pairs),
                pl.BlockSpec((seq, LANES), lambda b, p, it: (b, p)), pl.BlockSpec((seq, LANES), lambda b, p, it: (b, 0)),
                _blocked_spec(seq, t), _const_spec(bias)]
    scratch = [pltpu.VMEM((seq, 2 * LANES), BF16)] + _softmax_scratch(t)
    kernel_fn = functools.partial(_mla_pipe_kernel, scale=(MLA_NOPE + MLA_ROPE) ** -0.5)
    return _pipe_call(kernel_fn, "mla_attention", 3, bsz, seq, pairs, in_specs, (q_t, q_t, kn, kr, v_t, bias),
                      scratch, SOFTMAX_ITEMS_PER_ITER)


def _diff_attention(q_t, k, v_t, lam_params, g_sub, lambda_init, bsz, seq):
    t = ATT_TILE
    bias = _diag_bias(t, False)
    g2 = g_sub.reshape(1, LANES)
    in_specs = [_blocked_spec(seq, t), pl.BlockSpec((seq, LANES), lambda b, p, it: (b, p)), _blocked_spec(seq, t),
                _const_spec(bias), _const_spec(lam_params), _const_spec(g2)]
    kernel_fn = functools.partial(_diff_pipe_kernel, lambda_init=lambda_init)
    return _pipe_call(kernel_fn, "diff_attention", 3, bsz, seq, DIFF_HEADS, in_specs,
                      (q_t, k, v_t, bias, lam_params, g2), _softmax_scratch(t), SOFTMAX_ITEMS_PER_ITER)


PAIRS_PER_GROUP = EXPERTS_PER_GROUP * (EXPERTS_PER_GROUP - 1) // 2
N_CLASSES = N_GROUPS * PAIRS_PER_GROUP


def _class_experts():
    lo, hi = [], []
    for g in range(N_GROUPS):
        for a in range(EXPERTS_PER_GROUP):
            for b in range(a + 1, EXPERTS_PER_GROUP):
                lo.append(g * EXPERTS_PER_GROUP + a)
                hi.append(g * EXPERTS_PER_GROUP + b)
    return lo, hi


def _route(logits):
    lane = lax.broadcasted_iota(jnp.int32, logits.shape, 1).astype(F32)
    big = jnp.float32(1 << 20)

    def top1(vals):
        v = jnp.max(vals, axis=1, keepdims=True)
        i = jnp.min(jnp.where(vals == v, lane, big), axis=1, keepdims=True)
        return v, i

    is_group = (lane >= N_EXPERTS) & (lane < N_EXPERTS + N_GROUPS)
    gl = jnp.where(is_group, logits, -jnp.inf)
    gmax, gidx = top1(gl)
    g_w = 1.0 / jnp.sum(jnp.exp(gl - gmax), axis=1, keepdims=True)
    first = (gidx - N_EXPERTS) * EXPERTS_PER_GROUP
    el = jnp.where((lane >= first) & (lane < first + EXPERTS_PER_GROUP), logits, -jnp.inf)
    v1, i1 = top1(el)
    v2, i2 = top1(jnp.where(lane == i1, -jnp.inf, el))
    e2 = jnp.exp(v2 - v1)
    w1 = g_w / (1.0 + e2)
    w2 = g_w * e2 / (1.0 + e2)
    group = gidx - N_EXPERTS
    a = jnp.minimum(i1, i2) - first
    b = jnp.maximum(i1, i2) - first
    pair = a * (2 * EXPERTS_PER_GROUP - 1 - a) * 0.5 + (b - a - 1.0)
    cls = group * PAIRS_PER_GROUP + pair
    w_lo = jnp.where(i1 < i2, w1, w2)
    w_hi = jnp.where(i1 < i2, w2, w1)
    return jnp.where(lane == 0, cls, jnp.where(lane == 1, w_lo, jnp.where(lane == 2, w_hi, 0.0)))


def _oproj_kernel(a_ref, wo_ref, x_ref, gt_ref, g_ref, sc_ref, sh_ref, wr_ref, br_ref, xo_ref, hx_ref):
    d = x_ref.shape[1]
    xn = x_ref[...] + gt_ref[0] * _dot(a_ref[...], wo_ref[...])
    xo_ref[...] = xn
    h = _rms(xn) * g_ref[...]
    h = h * (1.0 + sc_ref[0]) + sh_ref[0]
    hx_ref[:, :d] = h
    h_hi = h.astype(BF16)
    h_lo = (h - h_hi.astype(F32)).astype(BF16)
    both = _dot(h_hi, wr_ref[...])
    logits = both[:, :LANES] + both[:, LANES:] + _dot(h_lo, wr_ref[:, :LANES]) + br_ref[...]
    hx_ref[:, d:] = _route(logits)


def _oproj(attn, w_o, x, gt, g, sc, sh, w_route, b_route, seq):
    t, d = x.shape
    tm = ROW_TILE
    per_batch = seq // tm
    row = pl.BlockSpec((tm, d), lambda i: (i, 0))
    mod = pl.BlockSpec((1, 1, d), lambda i: (i // per_batch, 0, 0))

    def full(a):
        return pl.BlockSpec(a.shape, lambda i: (0,) * a.ndim)

    g2 = g.reshape(1, d)
    return pl.pallas_call(
        _oproj_kernel,
        grid=(t // tm,),
        in_specs=[pl.BlockSpec((tm, attn.shape[1]), lambda i: (i, 0)), full(w_o), row, mod, full(g2), mod, mod,
                  full(w_route), full(b_route)],
        out_specs=[row, pl.BlockSpec((tm, d + LANES), lambda i: (i, 0))],
        out_shape=[jax.ShapeDtypeStruct((t, d), F32), jax.ShapeDtypeStruct((t, d + LANES), F32)],
        compiler_params=_cparams(1),
        name="oproj_router",
    )(attn, w_o, x, gt, g2, sc, sh, w_route, b_route)


def _sort_kernel(route_ref, ltri_ref, utri_ref, pos_ref, counts_ref, cnt_scr, off_scr, *, tile):
    phase, i = pl.program_id(0), pl.program_id(1)
    lane = lax.broadcasted_iota(jnp.int32, route_ref.shape, 1).astype(F32)
    onehot = jnp.where(lane == route_ref[:, 0:1], 1.0, 0.0)

    @pl.when((phase == 0) & (i == 0))
    def _():
        cnt_scr[...] = jnp.zeros_like(cnt_scr)

    @pl.when((phase == 1) & (i == 0))
    def _():
        cnt = cnt_scr[...]
        counts_ref[...] = cnt
        n_tiles = jnp.floor((cnt + (tile - 1.0)) * (1.0 / tile))
        before = _dot(jnp.broadcast_to(n_tiles, (8, LANES)).astype(BF16), utri_ref[...])
        off_scr[...] = before[0:1] * tile
        cnt_scr[...] = jnp.zeros_like(cnt_scr)

    @pl.when(phase == 1)
    def _():
        earlier = _dot(ltri_ref[...], onehot.astype(BF16))
        row = jnp.sum(onehot * (earlier + off_scr[...] + cnt_scr[...]), axis=1, keepdims=True)
        pos_ref[...] = row.astype(jnp.int32)

    cnt_scr[...] += jnp.sum(onehot, axis=0, keepdims=True)


def _sorted_rows(hx, tile):
    t, w = hx.shape
    tm = 2 * ROW_TILE
    ltri = (jnp.arange(tm)[:, None] > jnp.arange(tm)[None, :]).astype(BF16)
    utri = (jnp.arange(LANES)[:, None] < jnp.arange(LANES)[None, :]).astype(BF16)
    pos, counts = pl.pallas_call(
        functools.partial(_sort_kernel, tile=tile),
        grid=(2, t // tm),
        in_specs=[pl.BlockSpec((tm, LANES), lambda ph, i: (i, w // LANES - 1)),
                  pl.BlockSpec((tm, tm), lambda ph, i: (0, 0)), pl.BlockSpec((LANES, LANES), lambda ph, i: (0, 0))],
        out_specs=[pl.BlockSpec((tm, 1), lambda ph, i: (i * ph, 0)), pl.BlockSpec((1, LANES), lambda ph, i: (0, 0))],
        out_shape=[jax.ShapeDtypeStruct((t, 1), jnp.int32), jax.ShapeDtypeStruct((1, LANES), F32)],
        scratch_shapes=[pltpu.VMEM((1, LANES), F32), pltpu.VMEM((1, LANES), F32)],
        compiler_params=_cparams(2),
        name="moe_sort",
    )(hx, ltri, utri)
    return pos.reshape(t), counts.reshape(LANES)


ROWS_PER_STEP = 1024


def _row_dma_params():
    return pltpu.CompilerParams(dimension_semantics=("arbitrary",), vmem_limit_bytes=VMEM_LIMIT,
                                disable_bounds_checks=True)


def _dispatch_kernel(pos_ref, hx_ref, zeros_hbm, xs_hbm, sem):
    del zeros_hbm
    rows = hx_ref.shape[0]
    base = pl.program_id(0) * rows

    def issue(r, c):
        pltpu.make_async_copy(hx_ref.at[pl.ds(r, 1)], xs_hbm.at[pl.ds(pos_ref[base + r], 1)], sem).start()
        return c

    lax.fori_loop(0, rows, issue, 0, unroll=8)
    pltpu.make_async_copy(hx_ref, xs_hbm.at[pl.ds(0, rows)], sem).wait()


def _dispatch(pos, hx, n_rows):
    t, w = hx.shape
    any_spec = pl.BlockSpec(memory_space=pl.ANY)
    grid_spec = pltpu.PrefetchScalarGridSpec(
        num_scalar_prefetch=1, grid=(t // ROWS_PER_STEP,),
        in_specs=[pl.BlockSpec((ROWS_PER_STEP, w), lambda i, pos: (i, 0)), any_spec], out_specs=any_spec,
        scratch_shapes=[pltpu.SemaphoreType.DMA])
    return pl.pallas_call(
        _dispatch_kernel,
        grid_spec=grid_spec,
        out_shape=jax.ShapeDtypeStruct((n_rows, w), F32),
        input_output_aliases={2: 0},
        compiler_params=_row_dma_params(),
        name="moe_dispatch",
    )(pos, hx, jnp.zeros((n_rows, w), F32))


def _experts_kernel(lo_ref, hi_ref, used_ref, xs_ref, wgu_lo_ref, wgu_hi_ref, wd_lo_ref, wd_hi_ref, ys_ref):
    del lo_ref, hi_ref
    d = ys_ref.shape[1]

    @pl.when(used_ref[pl.program_id(0)] == 0)
    def _():
        ys_ref[...] = jnp.zeros_like(ys_ref)

    @pl.when(used_ref[pl.program_id(0)] == 1)
    def _():
        xb = xs_ref[:, :d].astype(BF16)
        route = xs_ref[:, d:]
        route_lane = lax.broadcasted_iota(jnp.int32, route.shape, 1)
        y = None
        for lane, wgu_ref, wd_ref in ((1, wgu_lo_ref, wd_lo_ref), (2, wgu_hi_ref, wd_hi_ref)):
            gu = _dot(xb, wgu_ref[0, 0].astype(BF16))
            gate, up = gu[:, :EXPERT_FF], gu[:, EXPERT_FF:]
            act = (gate / (1.0 + jnp.exp(-gate)) * up).astype(BF16)
            gate_w = jnp.sum(jnp.where(route_lane == lane, route, 0.0), axis=1, keepdims=True)
            term = gate_w * _dot(act, wd_ref[0, 0].astype(BF16))
            y = term if y is None else y + term
        ys_ref[...] = y


def _experts(xs, tile_lo, tile_hi, tile_used, w_gu, w_d, layer, tile):
    n_rows, w = xs.shape
    d = w - LANES
    gu_block, d_block = (1, 1) + w_gu.shape[2:], (1, 1) + w_d.shape[2:]
    grid_spec = pltpu.PrefetchScalarGridSpec(
        num_scalar_prefetch=3,
        grid=(n_rows // tile,),
        in_specs=[pl.BlockSpec((tile, w), lambda i, lo, hi, used: (i, 0)),
                  pl.BlockSpec(gu_block, lambda i, lo, hi, used: (layer, lo[i], 0, 0)),
                  pl.BlockSpec(gu_block, lambda i, lo, hi, used: (layer, hi[i], 0, 0)),
                  pl.BlockSpec(d_block, lambda i, lo, hi, used: (layer, lo[i], 0, 0)),
                  pl.BlockSpec(d_block, lambda i, lo, hi, used: (layer, hi[i], 0, 0))],
        out_specs=pl.BlockSpec((tile, d), lambda i, lo, hi, used: (i, 0)),
    )
    return pl.pallas_call(
        _experts_kernel,
        grid_spec=grid_spec,
        out_shape=jax.ShapeDtypeStruct((n_rows, d), F32),
        compiler_params=_cparams(1),
        name="moe_experts",
    )(tile_lo, tile_hi, tile_used, xs, w_gu, w_gu, w_d, w_d)


def _combine_kernel(pos_ref, ys_hbm, x_ref, gt_ref, g_ref, o_ref, buf, sem, *, final):
    tm = x_ref.shape[0]
    base = pl.program_id(0) * tm

    def issue(r, c):
        pltpu.make_async_copy(ys_hbm.at[pl.ds(pos_ref[base + r], 1)], buf.at[pl.ds(r, 1)], sem).start()
        return c

    lax.fori_loop(0, tm, issue, 0, unroll=8)
    pltpu.make_async_copy(ys_hbm.at[pl.ds(0, tm)], buf, sem).wait()
    y = x_ref[...] + gt_ref[0] * buf[...]
    o_ref[...] = _rms(y) * g_ref[...] if final else y


def _combine(pos, ys, x, gt, g_final, seq, final):
    t, d = x.shape
    tm = ROW_TILE
    per_batch = seq // tm
    grid_spec = pltpu.PrefetchScalarGridSpec(
        num_scalar_prefetch=1,
        grid=(t // tm,),
        in_specs=[pl.BlockSpec(memory_space=pl.ANY), pl.BlockSpec((tm, d), lambda i, pos: (i, 0)),
                  pl.BlockSpec((1, 1, d), lambda i, pos: (i // per_batch, 0, 0)),
                  pl.BlockSpec((1, d), lambda i, pos: (0, 0))],
        out_specs=pl.BlockSpec((tm, d), lambda i, pos: (i, 0)),
        scratch_shapes=[pltpu.VMEM((tm, d), F32), pltpu.SemaphoreType.DMA],
    )
    return pl.pallas_call(
        functools.partial(_combine_kernel, final=final),
        grid_spec=grid_spec,
        out_shape=jax.ShapeDtypeStruct((t, d), F32),
        compiler_params=_row_dma_params(),
        name="moe_combine",
    )(pos, ys, x, gt, g_final.reshape(1, d))


def _moe(hx, w_gu, w_d, layer, x, gt, g_final, seq, final):
    t = x.shape[0]
    tile = MOE_TILE
    n_tiles = t // tile + N_CLASSES
    pos, counts = _sorted_rows(hx, tile)
    tiles_per_class = jnp.ceil(counts[:N_CLASSES] / tile).astype(jnp.int32)
    ends = jnp.cumsum(tiles_per_class)
    tile_ids = jnp.arange(n_tiles, dtype=jnp.int32)
    tile_class = jnp.minimum(jnp.searchsorted(ends, tile_ids, side="right"), N_CLASSES - 1)
    tile_used = (tile_ids < ends[-1]).astype(jnp.int32)
    lo, hi = _class_experts()
    tile_lo = jnp.asarray(lo, jnp.int32)[tile_class]
    tile_hi = jnp.asarray(hi, jnp.int32)[tile_class]
    xs = _dispatch(pos, hx, n_tiles * tile)
    ys = _experts(xs, tile_lo, tile_hi, tile_used, w_gu, w_d, layer, tile)
    return _combine(pos, ys, x, gt, g_final, seq, final)


def _final_norm_kernel(x_ref, g_ref, o_ref):
    o_ref[...] = _rms(x_ref[...]) * g_ref[...]


def _final_norm(x, g):
    t, d = x.shape
    tm = 1024
    return pl.pallas_call(
        _final_norm_kernel,
        grid=(t // tm,),
        in_specs=[pl.BlockSpec((tm, d), lambda i: (i, 0)), pl.BlockSpec((1, d), lambda i: (0, 0))],
        out_specs=pl.BlockSpec((tm, d), lambda i: (i, 0)),
        out_shape=jax.ShapeDtypeStruct((t, d), F32),
        compiler_params=_cparams(1),
        name="final_norm",
    )(x, g.reshape(1, d))


def _diff_qk_layout(w):
    d = w.shape[0]
    w = w.reshape(d, DIFF_HEADS, 2, 2, HEAD_DIM // 2)
    return w.transpose(0, 1, 3, 2, 4).reshape(d, -1)


def _mla_layouts(w_in, w_q_up, w_kv_up):
    d = w_in.shape[0]
    half = MLA_ROPE // 2
    base = MLA_Q_RANK + MLA_KV_RANK
    r1, r2 = w_in[:, base:base + half], w_in[:, base + half:]
    z = jnp.zeros((d, 64 - 2 * half), w_in.dtype)
    w_in_l = jnp.concatenate([w_in[:, :base], r1, r1, z, r2, r2, z], axis=1)

    r = w_q_up.shape[0]
    wq = w_q_up.reshape(r, MLA_HEADS, MLA_NOPE + MLA_ROPE)
    nope = wq[:, :, :MLA_NOPE].reshape(r, -1)
    q1 = wq[:, :, MLA_NOPE:MLA_NOPE + half].reshape(r, MLA_HEADS // 2, 2 * half)
    q2 = wq[:, :, MLA_NOPE + half:].reshape(r, MLA_HEADS // 2, 2 * half)
    zq = jnp.zeros((r, MLA_HEADS // 2, 64 - 2 * half), w_q_up.dtype)
    rope = jnp.concatenate([q1, zq, q2, zq], axis=2).reshape(r, -1)
    w_q_l = jnp.concatenate([nope, rope], axis=1)

    rk = w_kv_up.shape[0]
    wkv = w_kv_up.reshape(rk, MLA_HEADS, MLA_NOPE + MLA_V)
    w_kn = wkv[:, :, :MLA_NOPE].reshape(rk, -1)
    w_v = wkv[:, :, MLA_NOPE:].reshape(rk, -1)
    return w_in_l.astype(BF16), w_q_l.T.astype(BF16), w_kn.astype(BF16), w_v.T.astype(BF16)


def _router_layout(w_group, b_group, w_router, b_router):
    d = w_group.shape[0]
    pad = LANES - N_EXPERTS - N_GROUPS
    w = jnp.concatenate([w_router, w_group, jnp.zeros((d, pad), F32)], axis=1)
    b = jnp.concatenate([b_router, b_group, jnp.zeros((pad,), F32)]).reshape(1, LANES)
    w_hi = w.astype(BF16)
    w_lo = (w - w_hi.astype(F32)).astype(BF16)
    return jnp.concatenate([w_hi, w_lo], axis=1), b


def kernel(x, c, positions, norm_mix_g, norm_ffn_g, ada_w, ada_b, sb_w_qkv, sb_w_o, mla_w_in, mla_g_q, mla_g_kv, mla_w_q_up, mla_w_kv_up, mla_w_o, diff_w_qkv, diff_lam_q1, diff_lam_k1, diff_lam_q2, diff_lam_k2, diff_g_sub, diff_w_o, moe_w_group, moe_b_group, moe_w_router, moe_b_router, moe_w_gate_up, moe_w_down, final_g):
    bsz, seq, d = x.shape
    depth = ada_w.shape[0]
    xt = x.reshape(bsz * seq, d)
    mod = _adaln(c, ada_w, ada_b)
    rope_mla = _rope_tables(positions, MLA_ROPE) if depth > 1 else None
    rope_diff = _rope_tables(positions, HEAD_DIM) if depth > 2 else None

    for i in range(depth):
        sh_m, sc_m, gt_m, sh_f, sc_f, gt_f = (mod[i, :, k * d:(k + 1) * d].reshape(bsz, 1, d) for k in range(6))
        kind, j = i % N_MIXERS, i // N_MIXERS
        if kind == 0:
            w = sb_w_qkv[j].astype(BF16)
            q_t, k, v_t = _qkv_proj(xt, norm_mix_g[i], sc_m, sh_m, w[:, :d].T, w[:, d:2 * d], w[:, 2 * d:].T, seq)
            attn = _sb_attention(q_t, k, v_t, bsz, seq)
            w_o = sb_w_o[j]
        elif kind == 1:
            w_in_l, wq_t, w_kn, wv_t = _mla_layouts(mla_w_in[j], mla_w_q_up[j], mla_w_kv_up[j])
            q_t, kn, kr, v_t = _mla_proj(xt, norm_mix_g[i], sc_m, sh_m, w_in_l, mla_g_q[j], mla_g_kv[j], wq_t, w_kn,
                                         wv_t, rope_mla, seq)
            attn = _mla_attention(q_t, kn, kr, v_t, bsz, seq)
            w_o = mla_w_o[j]
        else:
            w = diff_w_qkv[j].astype(BF16)
            q_t, k, v_t = _qkv_proj(xt, norm_mix_g[i], sc_m, sh_m, _diff_qk_layout(w[:, :d]).T,
                                    _diff_qk_layout(w[:, d:2 * d]), w[:, 2 * d:].T, seq, rope=rope_diff)
            lam_params = jnp.stack([diff_lam_q1[j], diff_lam_k1[j], diff_lam_q2[j], diff_lam_k2[j]])
            lambda_init = 0.8 - 0.6 * math.exp(-0.3 * i)
            attn = _diff_attention(q_t, k, v_t, lam_params, diff_g_sub[j], lambda_init, bsz, seq)
            w_o = diff_w_o[j]
        w_route, b_route = _router_layout(moe_w_group[i], moe_b_group[i], moe_w_router[i], moe_b_router[i])
        xt, hx = _oproj(attn, w_o.astype(BF16), xt, gt_m, norm_ffn_g[i], sc_f, sh_f, w_route, b_route, seq)
        xt = _moe(hx, moe_w_gate_up, moe_w_down, i, xt, gt_f, final_g, seq, final=i == depth - 1)
    return xt.reshape(bsz, seq, d)
```

```python
import functools
import math

import jax
import jax.numpy as jnp
from jax import lax
from jax.experimental import pallas as pl
from jax.experimental.pallas import tpu as pltpu

F32 = jnp.float32
BF16 = jnp.bfloat16

N_MIXERS = 3
ROPE_THETA = 10000.0
EPS = 1e-6
HEAD_DIM = 64
SB_HEADS = 16
MLA_HEADS = 16
MLA_Q_RANK = 384
MLA_KV_RANK = 256
MLA_NOPE = 64
MLA_ROPE = 32
MLA_V = 64
DIFF_HEADS = 8
N_GROUPS = 4
EXPERTS_PER_GROUP = 4
N_EXPERTS = N_GROUPS * EXPERTS_PER_GROUP
EXPERT_FF = 512

LANES = 128
LOG2E = 1.4426950408889634
NEG_BIG = -1e30
VMEM_LIMIT = 56 * 1024 * 1024

ROW_TILE = 512
ATT_TILE = 256
MOE_TILE = 256
PROJ_CHUNK = 256


def _cparams(n_axes):
    return pltpu.CompilerParams(dimension_semantics=("arbitrary",) * n_axes, vmem_limit_bytes=VMEM_LIMIT)


def _dot(a, b):
    return jnp.dot(a, b, preferred_element_type=F32)


def _dot_nt(a, b):
    return lax.dot_general(a, b, (((1,), (1,)), ((), ())), preferred_element_type=F32)


def _rms(x):
    return x * lax.rsqrt(jnp.mean(x * x, axis=-1, keepdims=True) + EPS)


def _rope_blocks(y, cos, sin_signed):
    out = []
    for j in range(y.shape[1] // LANES):
        yb = y[:, j * LANES:(j + 1) * LANES]
        out.append(yb * cos + pltpu.roll(yb, 64, 1) * sin_signed)
    return out[0] if len(out) == 1 else jnp.concatenate(out, axis=1)


def _rope_rows(y, cos_t, sin_t):
    out = []
    for j in range(y.shape[0] // LANES):
        yb = y[j * LANES:(j + 1) * LANES]
        out.append(yb * cos_t + jnp.concatenate([yb[64:], yb[:64]], axis=0) * sin_t)
    return out[0] if len(out) == 1 else jnp.concatenate(out, axis=0)


def _adaln_kernel(c_ref, w_ref, b_ref, o_ref):
    c = c_ref[...]
    ca = c / (1.0 + jnp.exp(-c))
    o_ref[0] = jnp.dot(ca, w_ref[0], preferred_element_type=F32, precision=lax.Precision.HIGHEST) + b_ref[0]


def _adaln(c, ada_w, ada_b):
    depth, d, n = ada_w.shape
    bsz = c.shape[0]
    rows = 8
    cp = jnp.zeros((rows, d), F32).at[:bsz].set(c)
    tn = 1536
    out = pl.pallas_call(
        _adaln_kernel,
        grid=(depth, n // tn),
        in_specs=[
            pl.BlockSpec((rows, d), lambda i, j: (0, 0)),
            pl.BlockSpec((1, d, tn), lambda i, j: (i, 0, j)),
            pl.BlockSpec((1, 1, tn), lambda i, j: (i, 0, j)),
        ],
        out_specs=pl.BlockSpec((1, rows, tn), lambda i, j: (i, 0, j)),
        out_shape=jax.ShapeDtypeStruct((depth, rows, n), F32),
        compiler_params=_cparams(2),
        name="adaln_mod",
    )(cp, ada_w, ada_b.reshape(depth, 1, n))
    return out[:, :bsz]


def _rope_table_kernel(pos_c_ref, pos_r_ref, invf_r_ref, sign_r_ref, invf_c_ref, sign_c_ref,
                       cos_ref, sin_ref, cost_ref, sint_ref):
    ang = pos_c_ref[...].astype(F32) * invf_r_ref[...]
    cos_ref[...] = jnp.cos(ang)
    sin_ref[...] = jnp.sin(ang) * sign_r_ref[...]
    ang_t = invf_c_ref[...] * pos_r_ref[...].astype(F32)
    cost_ref[...] = jnp.cos(ang_t)
    sint_ref[...] = jnp.sin(ang_t) * sign_c_ref[...]


def _rope_tables(positions, dim):
    t = positions.size
    half = dim // 2
    inv_freq = ROPE_THETA ** (-jnp.arange(0, dim, 2, dtype=F32) / dim)
    invf = jnp.tile(inv_freq, LANES // half)
    sign = jnp.where(jnp.arange(LANES) < 64, -1.0, 1.0).astype(F32)
    tm = 2048
    small_r = pl.BlockSpec((1, LANES), lambda i: (0, 0))
    small_c = pl.BlockSpec((LANES, 1), lambda i: (0, 0))
    return pl.pallas_call(
        _rope_table_kernel,
        grid=(t // tm,),
        in_specs=[pl.BlockSpec((tm, 1), lambda i: (i, 0)), pl.BlockSpec((1, tm), lambda i: (0, i)),
                  small_r, small_r, small_c, small_c],
        out_specs=[pl.BlockSpec((tm, LANES), lambda i: (i, 0))] * 2 + [pl.BlockSpec((LANES, tm), lambda i: (0, i))] * 2,
        out_shape=[jax.ShapeDtypeStruct((t, LANES), F32)] * 2 + [jax.ShapeDtypeStruct((LANES, t), F32)] * 2,
        compiler_params=_cparams(1),
        name="rope_tables",
    )(positions.reshape(t, 1), positions.reshape(1, t), invf.reshape(1, LANES), sign.reshape(1, LANES),
      invf.reshape(LANES, 1), sign.reshape(LANES, 1))


def _modulated_norm(x_ref, g_ref, sc_ref, sh_ref):
    h = _rms(x_ref[...]) * g_ref[...]
    return h * (1.0 + sc_ref[0]) + sh_ref[0]


def _store_token_blocks(o_ref, w_t_ref, hb, post=None):
    tb = o_ref.shape[2]
    for r in range(0, w_t_ref.shape[0], PROJ_CHUNK):
        y = _dot_nt(w_t_ref[r:r + PROJ_CHUNK, :], hb)
        if post is not None:
            y = post(y, r)
        y = y.astype(BF16)
        for c in range(o_ref.shape[0]):
            o_ref[c, r:r + PROJ_CHUNK, :] = y[:, c * tb:(c + 1) * tb]


def _qkv_kernel(x_ref, g_ref, sc_ref, sh_ref, wqt_ref, wk_ref, wvt_ref, cos_ref, sin_ref, cost_ref, sint_ref,
                q_ref, k_ref, v_ref, *, q_scale):
    hb = _modulated_norm(x_ref, g_ref, sc_ref, sh_ref).astype(BF16)
    _store_token_blocks(q_ref, wqt_ref, hb, lambda qt, r: _rope_rows(qt, cost_ref[...], sint_ref[...]) * q_scale)
    for r in range(0, wk_ref.shape[1], 512):
        k = _rope_blocks(_dot(hb, wk_ref[:, r:r + 512]), cos_ref[...], sin_ref[...])
        k_ref[:, r:r + 512] = k.astype(BF16)
    _store_token_blocks(v_ref, wvt_ref, hb)


def _qkv_proj(x, g, sc, sh, wq_t, wk, wv_t, seq, rope):
    t, d = x.shape
    tm, tk = ROW_TILE, ATT_TILE
    per_batch = seq // tm
    row = pl.BlockSpec((tm, d), lambda i: (i, 0))
    mod = pl.BlockSpec((1, 1, d), lambda i: (i // per_batch, 0, 0))

    def full(a):
        return pl.BlockSpec(a.shape, lambda i: (0,) * a.ndim)

    g2 = g.reshape(1, d)
    tab = pl.BlockSpec((tm, LANES), lambda i: (i, 0))
    tab_t = pl.BlockSpec((LANES, tm), lambda i: (0, i))
    nq, nk, nv = wq_t.shape[0], wk.shape[1], wv_t.shape[0]
    return pl.pallas_call(
        functools.partial(_qkv_kernel, q_scale=HEAD_DIM ** -0.5),
        grid=(t // tm,),
        in_specs=[row, full(g2), mod, mod, full(wq_t), full(wk), full(wv_t), tab, tab, tab_t, tab_t],
        out_specs=[pl.BlockSpec((tm // tk, nq, tk), lambda i: (i, 0, 0)), pl.BlockSpec((tm, nk), lambda i: (i, 0)),
                   pl.BlockSpec((tm // tk, nv, tk), lambda i: (i, 0, 0))],
        out_shape=[jax.ShapeDtypeStruct((t // tk, nq, tk), BF16), jax.ShapeDtypeStruct((t, nk), BF16),
                   jax.ShapeDtypeStruct((t // tk, nv, tk), BF16)],
        compiler_params=_cparams(1),
        name="qkv_proj",
    )(x, g2, sc, sh, wq_t, wk, wv_t, *rope)


def _mla_proj_kernel(x_ref, g_ref, sc_ref, sh_ref, win_ref, gq_ref, gkv_ref, wqt_ref, wkn_ref, wvt_ref,
                     cos_ref, sin_ref, cost_ref, sint_ref, q_ref, kn_ref, kr_ref, v_ref):
    hb = _modulated_norm(x_ref, g_ref, sc_ref, sh_ref).astype(BF16)
    lat = _dot(hb, win_ref[...])
    cq = (_rms(lat[:, :MLA_Q_RANK]) * gq_ref[...]).astype(BF16)
    ckv = (_rms(lat[:, MLA_Q_RANK:MLA_Q_RANK + MLA_KV_RANK]) * gkv_ref[...]).astype(BF16)
    kr_ref[...] = _rope_blocks(lat[:, MLA_Q_RANK + MLA_KV_RANK:], cos_ref[...], sin_ref[...]).astype(BF16)
    n_nope = MLA_HEADS * MLA_NOPE
    _store_token_blocks(q_ref, wqt_ref, cq,
                        lambda qt, r: _rope_rows(qt, cost_ref[...], sint_ref[...]) if r >= n_nope else qt)
    kn_ref[...] = _dot(ckv, wkn_ref[...]).astype(BF16)
    _store_token_blocks(v_ref, wvt_ref, ckv)


def _mla_proj(x, g, sc, sh, w_in, g_q, g_kv, wq_t, wkn, wv_t, rope, seq):
    t, d = x.shape
    tm, tk = ROW_TILE, ATT_TILE
    per_batch = seq // tm
    row = pl.BlockSpec((tm, d), lambda i: (i, 0))
    mod = pl.BlockSpec((1, 1, d), lambda i: (i // per_batch, 0, 0))

    def full(a):
        return pl.BlockSpec(a.shape, lambda i: (0,) * a.ndim)

    tab = pl.BlockSpec((tm, LANES), lambda i: (i, 0))
    tab_t = pl.BlockSpec((LANES, tm), lambda i: (0, i))
    g2, gq2, gkv2 = g.reshape(1, d), g_q.reshape(1, -1), g_kv.reshape(1, -1)
    nq, nk, nv = wq_t.shape[0], wkn.shape[1], wv_t.shape[0]
    return pl.pallas_call(
        _mla_proj_kernel,
        grid=(t // tm,),
        in_specs=[row, full(g2), mod, mod, full(w_in), full(gq2), full(gkv2), full(wq_t), full(wkn), full(wv_t),
                  tab, tab, tab_t, tab_t],
        out_specs=[pl.BlockSpec((tm // tk, nq, tk), lambda i: (i, 0, 0)), pl.BlockSpec((tm, nk), lambda i: (i, 0)),
                   pl.BlockSpec((tm, LANES), lambda i: (i, 0)), pl.BlockSpec((tm // tk, nv, tk), lambda i: (i, 0, 0))],
        out_shape=[jax.ShapeDtypeStruct((t // tk, nq, tk), BF16), jax.ShapeDtypeStruct((t, nk), BF16),
                   jax.ShapeDtypeStruct((t, LANES), BF16), jax.ShapeDtypeStruct((t // tk, nv, tk), BF16)],
        compiler_params=_cparams(1),
        name="mla_proj",
    )(x, g2, sc, sh, w_in, gq2, gkv2, wq_t, wkn, wv_t, *rope)


def _qkv_rows_kernel(x_ref, g_ref, sc_ref, sh_ref, w_ref, q_ref, k_ref, v_ref, *, q_scale):
    hb = _modulated_norm(x_ref, g_ref, sc_ref, sh_ref).astype(BF16)
    d = q_ref.shape[1]
    for idx, o_ref in enumerate((q_ref, k_ref, v_ref)):
        for r in range(0, d, 512):
            y = _dot(hb, w_ref[:, idx * d + r:idx * d + r + 512])
            if idx == 0:
                y = y * q_scale
            o_ref[:, r:r + 512] = y.astype(BF16)


def _qkv_rows_proj(x, g, sc, sh, w, seq):
    t, d = x.shape
    tm = ROW_TILE
    per_batch = seq // tm
    row = pl.BlockSpec((tm, d), lambda i: (i, 0))
    mod = pl.BlockSpec((1, 1, d), lambda i: (i // per_batch, 0, 0))
    g2 = g.reshape(1, d)
    return pl.pallas_call(
        functools.partial(_qkv_rows_kernel, q_scale=HEAD_DIM ** -0.5),
        grid=(t // tm,),
        in_specs=[row, pl.BlockSpec((1, d), lambda i: (0, 0)), mod, mod, pl.BlockSpec(w.shape, lambda i: (0, 0))],
        out_specs=[row, row, row],
        out_shape=[jax.ShapeDtypeStruct((t, d), BF16)] * 3,
        compiler_params=_cparams(1),
        name="qkv_rows_proj",
    )(x, g2, sc, sh, w)


SOFTMAX_ITEMS_PER_ITER = 2
SB_ITEMS_PER_ITER = 1
SOFTMAX_ROWS = 32


def _attention_items(nq, depth, per_iter):
    rows = [(qi, qi - s, int(s == 0), int(s == qi), qi & 1, qi & 3) for qi in range(nq) for s in range(qi + 1)]
    rows += [(0, 0, 1, 0, nq & 1, nq & 3)] * (-len(rows) % per_iter + (depth - 1) * per_iter)
    return jnp.asarray(list(zip(*rows)), jnp.int32)


def _diag_bias(t, strict):
    key = jnp.arange(t)[:, None]
    query = jnp.arange(t)[None, :]
    masked = key >= query if strict else key > query
    return jnp.stack([jnp.zeros((t, t), F32), jnp.where(masked, NEG_BIG, 0.0).astype(F32)])


def _run_pipeline(stages, n_items, finalize, per_iter):
    depth = len(stages)

    def iteration(it, static):
        for k in reversed(range(depth)):
            if static and it < k:
                continue
            for u in range(per_iter):
                stages[k]((it - k) * per_iter + u)

    for it in range(depth - 1):
        iteration(it, True)

    def body(it, c):
        iteration(it, False)
        for u in range(per_iter):
            finalize((it - (depth - 1)) * per_iter + u)
        return c

    lax.fori_loop(depth - 1, n_items // per_iter, body, 0)


def _pipe_call(kernel_fn, name, depth, bsz, seq, pairs, in_specs, args, scratch, per_iter):
    t = ATT_TILE
    items = _attention_items(seq // t, depth, per_iter)
    grid_spec = pltpu.PrefetchScalarGridSpec(
        num_scalar_prefetch=1,
        grid=(bsz, pairs),
        in_specs=in_specs,
        out_specs=pl.BlockSpec((seq, LANES), lambda b, p, it: (b, p)),
        scratch_shapes=scratch,
    )
    return pl.pallas_call(
        kernel_fn,
        grid_spec=grid_spec,
        out_shape=jax.ShapeDtypeStruct((bsz * seq, pairs * LANES), BF16),
        compiler_params=_cparams(2),
        name=name,
    )(items, *args)


def _blocked_spec(seq, t, offset=0):
    return pl.BlockSpec((seq // t, LANES, t), lambda b, p, it: (b, offset + p, 0))


def _const_spec(a):
    return pl.BlockSpec(a.shape, lambda b, p, it: (0,) * a.ndim)


def _keep_rows(x, keep):
    n_rows, n = x.shape
    parts, pos = [], 0
    for a, b in keep:
        if a > pos:
            parts.append(jnp.zeros((a - pos, n), x.dtype))
        parts.append(x[a:b])
        pos = b
    if pos < n_rows:
        parts.append(jnp.zeros((n_rows - pos, n), x.dtype))
    return jnp.concatenate(parts, axis=0)


def _store_tile(o_ref, qi, o_t):
    t = o_t.shape[1]
    o_ref[pl.ds(pl.multiple_of(qi * t, t), t), :] = o_t.T.astype(o_ref.dtype)


def _softmax_pipeline(items_ref, q_heads_of, k_ref, vt_ref, bias_ref, scr, c_log2, finalize_tile):
    s_scr, p_scr, alpha_scr, lfin_scr, m_ref, l_ref, acc_ref = scr
    t = s_scr.shape[-1]
    n_slots = s_scr.shape[0]
    m_ref[...] = jnp.zeros_like(m_ref)
    l_ref[...] = jnp.zeros_like(l_ref)
    acc_ref[...] = jnp.zeros_like(acc_ref)

    def scores(w):
        qi, j, first = items_ref[0, w], items_ref[1, w], items_ref[2, w]
        kb = k_ref[pl.ds(pl.multiple_of(j * t, t), t), :]
        bias = bias_ref[first]
        for hd, qt in enumerate(q_heads_of(qi)):
            s_scr[w & (n_slots - 1), hd] = _dot(kb, qt) * c_log2 + bias

    def softmax(w):
        slot = w & (n_slots - 1)
        first = items_ref[2, w] == 1
        for hd in range(2):
            mx = s_scr[slot, hd, 0:SOFTMAX_ROWS, :]
            for r in range(SOFTMAX_ROWS, t, SOFTMAX_ROWS):
                mx = jnp.maximum(mx, s_scr[slot, hd, r:r + SOFTMAX_ROWS, :])
            m_old = jnp.where(first, NEG_BIG, m_ref[hd])
            m_new = jnp.maximum(m_old, jnp.max(mx, axis=0, keepdims=True))
            alpha = jnp.where(first, 0.0, jnp.exp2(m_old - m_new))
            psum = None
            for r in range(0, t, SOFTMAX_ROWS):
                p = jnp.exp2(s_scr[slot, hd, r:r + SOFTMAX_ROWS, :] - m_new)
                p_scr[slot, hd, r:r + SOFTMAX_ROWS, :] = p.astype(BF16)
                psum = p if psum is None else psum + p
            l_new = alpha * l_ref[hd] + jnp.sum(psum, axis=0, keepdims=True)
            l_ref[hd] = l_new
            m_ref[hd] = m_new
            alpha_scr[slot, hd] = alpha
            lfin_scr[items_ref[5, w], hd] = l_new

    def values(w):
        slot = w & (n_slots - 1)
        vb = vt_ref[items_ref[1, w]]
        par = items_ref[4, w]
        for hd in range(2):
            acc_ref[par, hd] = alpha_scr[slot, hd] * acc_ref[par, hd] + _dot(vb, p_scr[slot, hd])

    def finalize(w):
        @pl.when(items_ref[3, w] == 1)
        def _():
            finalize_tile(items_ref[0, w], acc_ref.at[items_ref[4, w]], lfin_scr[items_ref[5, w]])

    _run_pipeline([scores, softmax, values], items_ref.shape[1], finalize, n_slots // 2)


def _mla_pipe_kernel(items_ref, qn_ref, qr_ref, kn_ref, kr_ref, vt_ref, bias_ref, o_ref, kcat_ref, *scr, scale):
    kcat_ref[:, :LANES] = kn_ref[...]
    kcat_ref[:, LANES:] = kr_ref[...]
    half = MLA_ROPE // 2

    def q_heads_of(qi):
        qn = qn_ref[qi]
        qr = qr_ref[qi]
        return (
            jnp.concatenate([_keep_rows(qn, [(0, MLA_NOPE)]), _keep_rows(qr, [(0, half), (64, 64 + half)])], axis=0),
            jnp.concatenate([_keep_rows(qn, [(MLA_NOPE, 2 * MLA_NOPE)]),
                             _keep_rows(qr, [(half, 2 * half), (64 + half, 64 + 2 * half)])], axis=0),
        )

    def finalize_tile(qi, acc_ref, l_fin):
        o_t = jnp.concatenate([acc_ref[0, :MLA_V] / l_fin[0], acc_ref[1, MLA_V:] / l_fin[1]], axis=0)
        _store_tile(o_ref, qi, o_t)

    _softmax_pipeline(items_ref, q_heads_of, kcat_ref, vt_ref, bias_ref, scr, scale * LOG2E, finalize_tile)


def _diff_pipe_kernel(items_ref, q_ref, k_ref, vt_ref, bias_ref, lam_ref, gsub_ref, o_ref, *scr, lambda_init):
    def q_heads_of(qi):
        qt = q_ref[qi]
        return _keep_rows(qt, [(0, 32), (64, 96)]), _keep_rows(qt, [(32, 64), (96, 128)])

    def finalize_tile(qi, acc_ref, l_fin):
        lp = lam_ref[...]
        lam = (jnp.exp(jnp.sum(lp[0:1] * lp[1:2], axis=1, keepdims=True))
               - jnp.exp(jnp.sum(lp[2:3] * lp[3:4], axis=1, keepdims=True)) + lambda_init)
        o_t = acc_ref[0] / l_fin[0] - lam * (acc_ref[1] / l_fin[1])
        o_t = o_t * lax.rsqrt(jnp.mean(o_t * o_t, axis=0, keepdims=True) + EPS)
        t = o_t.shape[1]
        o = o_t.T * gsub_ref[...] * (1.0 - lambda_init)
        o_ref[pl.ds(pl.multiple_of(qi * t, t), t), :] = o.astype(o_ref.dtype)

    _softmax_pipeline(items_ref, q_heads_of, k_ref, vt_ref, bias_ref, scr, LOG2E, finalize_tile)


def _softmax_scratch(t):
    n = 2 * SOFTMAX_ITEMS_PER_ITER
    return [pltpu.VMEM((n, 2, t, t), F32), pltpu.VMEM((n, 2, t, t), BF16), pltpu.VMEM((n, 2, 1, t), F32),
            pltpu.VMEM((4, 2, 1, t), F32), pltpu.VMEM((2, 1, t), F32), pltpu.VMEM((2, 1, t), F32),
            pltpu.VMEM((2, 2, LANES, t), F32)]


def _mla_attention(q_t, kn, kr, v_t, bsz, seq):
    t = ATT_TILE
    pairs = MLA_HEADS // 2
    bias = _diag_bias(t, False)
    in_specs = [_blocked_spec(seq, t), _blocked_spec(seq, t, pairs),
                pl.BlockSpec((seq, LANES), lambda b, p, it: (b, p)), pl.BlockSpec((seq, LANES), lambda b, p, it: (b, 0)),
                _blocked_spec(seq, t), _const_spec(bias)]
    scratch = [pltpu.VMEM((seq, 2 * LANES), BF16)] + _softmax_scratch(t)
    kernel_fn = functools.partial(_mla_pipe_kernel, scale=(MLA_NOPE + MLA_ROPE) ** -0.5)
    return _pipe_call(kernel_fn, "mla_attention", 3, bsz, seq, pairs, in_specs, (q_t, q_t, kn, kr, v_t, bias),
                      scratch, SOFTMAX_ITEMS_PER_ITER)


def _diff_attention(q_t, k, v_t, lam_params, g_sub, lambda_init, bsz, seq):
    t = ATT_TILE
    bias = _diag_bias(t, False)
    g2 = g_sub.reshape(1, LANES)
    in_specs = [_blocked_spec(seq, t), pl.BlockSpec((seq, LANES), lambda b, p, it: (b, p)), _blocked_spec(seq, t),
                _const_spec(bias), _const_spec(lam_params), _const_spec(g2)]
    kernel_fn = functools.partial(_diff_pipe_kernel, lambda_init=lambda_init)
    return _pipe_call(kernel_fn, "diff_attention", 3, bsz, seq, DIFF_HEADS, in_specs,
                      (q_t, k, v_t, bias, lam_params, g2), _softmax_scratch(t), SOFTMAX_ITEMS_PER_ITER)


def _sb_rows_kernel(items_ref, q_ref, k_ref, v_ref, umat_ref, bias_ref, o_ref,
                    z_scr, sp_scr, cs_scr, w_scr, acc_ref, carry_ref):
    t = z_scr.shape[-1]
    n_slots = sp_scr.shape[0]
    lane = lax.broadcasted_iota(jnp.int32, (t, LANES), 1)
    acc_ref[...] = jnp.zeros_like(acc_ref)
    carry_ref[...] = jnp.zeros_like(carry_ref)

    def rows_of(block):
        return pl.ds(pl.multiple_of(block * t, t), t)

    def scores(w):
        qi, j, first = items_ref[0, w], items_ref[1, w], items_ref[2, w]
        q2 = q_ref[rows_of(qi), :]
        zero = jnp.zeros_like(q2)
        q_heads = jnp.concatenate([jnp.where(lane < HEAD_DIM, q2, zero), jnp.where(lane >= HEAD_DIM, q2, zero)], axis=0)
        bias = bias_ref[first]
        z_scr[w & (2 * n_slots - 1)] = _dot_nt(q_heads, k_ref[rows_of(j), :]) + jnp.concatenate([bias, bias], axis=0)

    def softplus(w):
        z = z_scr[w & (2 * n_slots - 1)]
        sp = jnp.maximum(z, 0.0) + jnp.log(1.0 + jnp.exp2(jnp.abs(z) * -LOG2E))
        sp_scr[w & (n_slots - 1)] = sp.astype(BF16)

    def sums(w):
        cs_scr[w & (n_slots - 1)] = _dot(sp_scr[w & (n_slots - 1)], umat_ref[...])

    def weights(w):
        cs = cs_scr[w & (n_slots - 1)]
        carry = jnp.where(items_ref[2, w] == 1, 0.0, carry_ref[...])
        carry_keys = jnp.concatenate([carry] * (t // LANES), axis=1)
        w_scr[w & (n_slots - 1)] = jnp.exp(z_scr[w & (2 * n_slots - 1)] - cs - carry_keys).astype(BF16)
        carry_ref[...] = carry + jnp.broadcast_to(cs[:, 0:1], carry.shape)

    def values(w):
        keep = jnp.where(items_ref[2, w] == 1, 0.0, 1.0)
        par = items_ref[4, w]
        acc_ref[par] = keep * acc_ref[par] + _dot(w_scr[w & (n_slots - 1)], v_ref[rows_of(items_ref[1, w]), :])

    def finalize(w):
        @pl.when(items_ref[3, w] == 1)
        def _():
            acc = acc_ref[items_ref[4, w]]
            o_ref[rows_of(items_ref[0, w]), :] = jnp.where(lane < HEAD_DIM, acc[:t], acc[t:]).astype(o_ref.dtype)

    _run_pipeline([scores, softplus, sums, weights, values], items_ref.shape[1], finalize, n_slots // 2)


def _sb_rows_attention(q, k, v, bsz, seq):
    t = ATT_TILE
    umat = (jnp.arange(t)[:, None] >= jnp.arange(t)[None, :]).astype(BF16)
    bias = _diag_bias(t, True).transpose(0, 2, 1)
    seq_spec = pl.BlockSpec((seq, LANES), lambda b, p, it: (b, p))
    n = 2 * SB_ITEMS_PER_ITER
    scratch = [pltpu.VMEM((2 * n, 2 * t, t), F32), pltpu.VMEM((n, 2 * t, t), BF16), pltpu.VMEM((n, 2 * t, t), F32),
               pltpu.VMEM((n, 2 * t, t), BF16), pltpu.VMEM((2, 2 * t, LANES), F32), pltpu.VMEM((2 * t, LANES), F32)]
    return _pipe_call(_sb_rows_kernel, "sb_attention", 5, bsz, seq, SB_HEADS // 2,
                      [seq_spec, seq_spec, seq_spec, _const_spec(umat), _const_spec(bias)],
                      (q, k, v, umat, bias), scratch, SB_ITEMS_PER_ITER)


PAIRS_PER_GROUP = EXPERTS_PER_GROUP * (EXPERTS_PER_GROUP - 1) // 2
N_CLASSES = N_GROUPS * PAIRS_PER_GROUP


def _class_experts():
    lo, hi = [], []
    for g in range(N_GROUPS):
        for a in range(EXPERTS_PER_GROUP):
            for b in range(a + 1, EXPERTS_PER_GROUP):
                lo.append(g * EXPERTS_PER_GROUP + a)
                hi.append(g * EXPERTS_PER_GROUP + b)
    return lo, hi


def _route(logits):
    lane = lax.broadcasted_iota(jnp.int32, logits.shape, 1).astype(F32)
    big = jnp.float32(1 << 20)

    def top1(vals):
        v = jnp.max(vals, axis=1, keepdims=True)
        i = jnp.min(jnp.where(vals == v, lane, big), axis=1, keepdims=True)
        return v, i

    is_group = (lane >= N_EXPERTS) & (lane < N_EXPERTS + N_GROUPS)
    gl = jnp.where(is_group, logits, -jnp.inf)
    gmax, gidx = top1(gl)
    g_w = 1.0 / jnp.sum(jnp.exp(gl - gmax), axis=1, keepdims=True)
    first = (gidx - N_EXPERTS) * EXPERTS_PER_GROUP
    el = jnp.where((lane >= first) & (lane < first + EXPERTS_PER_GROUP), logits, -jnp.inf)
    v1, i1 = top1(el)
    v2, i2 = top1(jnp.where(lane == i1, -jnp.inf, el))
    e2 = jnp.exp(v2 - v1)
    w1 = g_w / (1.0 + e2)
    w2 = g_w * e2 / (1.0 + e2)
    group = gidx - N_EXPERTS
    a = jnp.minimum(i1, i2) - first
    b = jnp.maximum(i1, i2) - first
    pair = a * (2 * EXPERTS_PER_GROUP - 1 - a) * 0.5 + (b - a - 1.0)
    cls = group * PAIRS_PER_GROUP + pair
    w_lo = jnp.where(i1 < i2, w1, w2)
    w_hi = jnp.where(i1 < i2, w2, w1)
    return jnp.where(lane == 0, cls, jnp.where(lane == 1, w_lo, jnp.where(lane == 2, w_hi, 0.0)))


def _oproj_kernel(a_ref, wo_ref, x_ref, gt_ref, g_ref, sc_ref, sh_ref, wr_ref, br_ref, xo_ref, hx_ref):
    d = x_ref.shape[1]
    xn = x_ref[...] + gt_ref[0] * _dot(a_ref[...], wo_ref[...])
    xo_ref[...] = xn
    h = _rms(xn) * g_ref[...]
    h = h * (1.0 + sc_ref[0]) + sh_ref[0]
    hx_ref[:, :d] = h
    h_hi = h.astype(BF16)
    h_lo = (h - h_hi.astype(F32)).astype(BF16)
    both = _dot(h_hi, wr_ref[...])
    logits = both[:, :LANES] + both[:, LANES:] + _dot(h_lo, wr_ref[:, :LANES]) + br_ref[...]
    hx_ref[:, d:] = _route(logits)


def _oproj(attn, w_o, x, gt, g, sc, sh, w_route, b_route, seq):
    t, d = x.shape
    tm = ROW_TILE
    per_batch = seq // tm
    row = pl.BlockSpec((tm, d), lambda i: (i, 0))
    mod = pl.BlockSpec((1, 1, d), lambda i: (i // per_batch, 0, 0))

    def full(a):
        return pl.BlockSpec(a.shape, lambda i: (0,) * a.ndim)

    g2 = g.reshape(1, d)
    return pl.pallas_call(
        _oproj_kernel,
        grid=(t // tm,),
        in_specs=[pl.BlockSpec((tm, attn.shape[1]), lambda i: (i, 0)), full(w_o), row, mod, full(g2), mod, mod,
                  full(w_route), full(b_route)],
        out_specs=[row, pl.BlockSpec((tm, d + LANES), lambda i: (i, 0))],
        out_shape=[jax.ShapeDtypeStruct((t, d), F32), jax.ShapeDtypeStruct((t, d + LANES), F32)],
        compiler_params=_cparams(1),
        name="oproj_router",
    )(attn, w_o, x, gt, g2, sc, sh, w_route, b_route)


ROWS_PER_STEP = 1024


def _sort_kernel(route_ref, ltri_ref, utri_ref, pos_ref, counts_ref, cnt_scr, off_scr, *, tile):
    phase, i = pl.program_id(0), pl.program_id(1)
    lane = lax.broadcasted_iota(jnp.int32, route_ref.shape, 1).astype(F32)
    onehot = jnp.where(lane == route_ref[:, 0:1], 1.0, 0.0)

    @pl.when((phase == 0) & (i == 0))
    def _():
        cnt_scr[...] = jnp.zeros_like(cnt_scr)

    @pl.when((phase == 1) & (i == 0))
    def _():
        cnt = cnt_scr[...]
        counts_ref[...] = cnt
        n_tiles = jnp.floor((cnt + (tile - 1.0)) * (1.0 / tile))
        before = _dot(jnp.broadcast_to(n_tiles, (8, LANES)).astype(BF16), utri_ref[...])
        off_scr[...] = before[0:1] * tile
        cnt_scr[...] = jnp.zeros_like(cnt_scr)

    @pl.when(phase == 1)
    def _():
        earlier = _dot(ltri_ref[...], onehot.astype(BF16))
        row = jnp.sum(onehot * (earlier + off_scr[...] + cnt_scr[...]), axis=1, keepdims=True)
        pos_ref[...] = row.astype(jnp.int32)

    cnt_scr[...] += jnp.sum(onehot, axis=0, keepdims=True)


def _sorted_rows(hx, tile):
    t, w = hx.shape
    tm = 2 * ROW_TILE
    ltri = (jnp.arange(tm)[:, None] > jnp.arange(tm)[None, :]).astype(BF16)
    utri = (jnp.arange(LANES)[:, None] < jnp.arange(LANES)[None, :]).astype(BF16)
    pos, counts = pl.pallas_call(
        functools.partial(_sort_kernel, tile=tile),
        grid=(2, t // tm),
        in_specs=[pl.BlockSpec((tm, LANES), lambda ph, i: (i, w // LANES - 1)),
                  pl.BlockSpec((tm, tm), lambda ph, i: (0, 0)), pl.BlockSpec((LANES, LANES), lambda ph, i: (0, 0))],
        out_specs=[pl.BlockSpec((tm, 1), lambda ph, i: (i * ph, 0)), pl.BlockSpec((1, LANES), lambda ph, i: (0, 0))],
        out_shape=[jax.ShapeDtypeStruct((t, 1), jnp.int32), jax.ShapeDtypeStruct((1, LANES), F32)],
        scratch_shapes=[pltpu.VMEM((1, LANES), F32), pltpu.VMEM((1, LANES), F32)],
        compiler_params=_cparams(2),
        name="moe_sort",
    )(hx, ltri, utri)
    return pos.reshape(t), counts.reshape(LANES)


def _row_dma_params():
    return pltpu.CompilerParams(dimension_semantics=("arbitrary",), vmem_limit_bytes=VMEM_LIMIT,
                                disable_bounds_checks=True)


def _dispatch_kernel(pos_ref, hx_ref, zeros_hbm, xs_hbm, sem):
    del zeros_hbm
    rows = hx_ref.shape[0]
    base = pl.program_id(0) * rows

    def issue(r, c):
        pltpu.make_async_copy(hx_ref.at[pl.ds(r, 1)], xs_hbm.at[pl.ds(pos_ref[base + r], 1)], sem).start()
        return c

    lax.fori_loop(0, rows, issue, 0, unroll=8)
    pltpu.make_async_copy(hx_ref, xs_hbm.at[pl.ds(0, rows)], sem).wait()


def _dispatch(pos, hx, n_rows):
    t, w = hx.shape
    any_spec = pl.BlockSpec(memory_space=pl.ANY)
    grid_spec = pltpu.PrefetchScalarGridSpec(
        num_scalar_prefetch=1, grid=(t // ROWS_PER_STEP,),
        in_specs=[pl.BlockSpec((ROWS_PER_STEP, w), lambda i, pos: (i, 0)), any_spec], out_specs=any_spec,
        scratch_shapes=[pltpu.SemaphoreType.DMA])
    return pl.pallas_call(
        _dispatch_kernel,
        grid_spec=grid_spec,
        out_shape=jax.ShapeDtypeStruct((n_rows, w), F32),
        input_output_aliases={2: 0},
        compiler_params=_row_dma_params(),
        name="moe_dispatch",
    )(pos, hx, jnp.zeros((n_rows, w), F32))


def _experts_kernel(lo_ref, hi_ref, used_ref, xs_ref, wgu_lo_ref, wgu_hi_ref, wd_lo_ref, wd_hi_ref, ys_ref):
    del lo_ref, hi_ref
    d = ys_ref.shape[1]

    @pl.when(used_ref[pl.program_id(0)] == 0)
    def _():
        ys_ref[...] = jnp.zeros_like(ys_ref)

    @pl.when(used_ref[pl.program_id(0)] == 1)
    def _():
        xb = xs_ref[:, :d].astype(BF16)
        route = xs_ref[:, d:]
        route_lane = lax.broadcasted_iota(jnp.int32, route.shape, 1)
        y = None
        for lane, wgu_ref, wd_ref in ((1, wgu_lo_ref, wd_lo_ref), (2, wgu_hi_ref, wd_hi_ref)):
            gu = _dot(xb, wgu_ref[0, 0].astype(BF16))
            gate, up = gu[:, :EXPERT_FF], gu[:, EXPERT_FF:]
            act = (gate / (1.0 + jnp.exp(-gate)) * up).astype(BF16)
            gate_w = jnp.sum(jnp.where(route_lane == lane, route, 0.0), axis=1, keepdims=True)
            term = gate_w * _dot(act, wd_ref[0, 0].astype(BF16))
            y = term if y is None else y + term
        ys_ref[...] = y


def _experts(xs, tile_lo, tile_hi, tile_used, w_gu, w_d, layer, tile):
    n_rows, w = xs.shape
    d = w - LANES
    gu_block, d_block = (1, 1) + w_gu.shape[2:], (1, 1) + w_d.shape[2:]
    grid_spec = pltpu.PrefetchScalarGridSpec(
        num_scalar_prefetch=3,
        grid=(n_rows // tile,),
        in_specs=[pl.BlockSpec((tile, w), lambda i, lo, hi, used: (i, 0)),
                  pl.BlockSpec(gu_block, lambda i, lo, hi, used: (layer, lo[i], 0, 0)),
                  pl.BlockSpec(gu_block, lambda i, lo, hi, used: (layer, hi[i], 0, 0)),
                  pl.BlockSpec(d_block, lambda i, lo, hi, used: (layer, lo[i], 0, 0)),
                  pl.BlockSpec(d_block, lambda i, lo, hi, used: (layer, hi[i], 0, 0))],
        out_specs=pl.BlockSpec((tile, d), lambda i, lo, hi, used: (i, 0)),
    )
    return pl.pallas_call(
        _experts_kernel,
        grid_spec=grid_spec,
        out_shape=jax.ShapeDtypeStruct((n_rows, d), F32),
        compiler_params=_cparams(1),
        name="moe_experts",
    )(tile_lo, tile_hi, tile_used, xs, w_gu, w_gu, w_d, w_d)


def _combine_kernel(pos_ref, ys_hbm, x_ref, gt_ref, g_ref, o_ref, buf, sem, *, final):
    tm = x_ref.shape[0]
    base = pl.program_id(0) * tm

    def issue(r, c):
        pltpu.make_async_copy(ys_hbm.at[pl.ds(pos_ref[base + r], 1)], buf.at[pl.ds(r, 1)], sem).start()
        return c

    lax.fori_loop(0, tm, issue, 0, unroll=8)
    pltpu.make_async_copy(ys_hbm.at[pl.ds(0, tm)], buf, sem).wait()
    y = x_ref[...] + gt_ref[0] * buf[...]
    o_ref[...] = _rms(y) * g_ref[...] if final else y


def _combine(pos, ys, x, gt, g_final, seq, final):
    t, d = x.shape
    tm = ROW_TILE
    per_batch = seq // tm
    grid_spec = pltpu.PrefetchScalarGridSpec(
        num_scalar_prefetch=1,
        grid=(t // tm,),
        in_specs=[pl.BlockSpec(memory_space=pl.ANY), pl.BlockSpec((tm, d), lambda i, pos: (i, 0)),
                  pl.BlockSpec((1, 1, d), lambda i, pos: (i // per_batch, 0, 0)),
                  pl.BlockSpec((1, d), lambda i, pos: (0, 0))],
        out_specs=pl.BlockSpec((tm, d), lambda i, pos: (i, 0)),
        scratch_shapes=[pltpu.VMEM((tm, d), F32), pltpu.SemaphoreType.DMA],
    )
    return pl.pallas_call(
        functools.partial(_combine_kernel, final=final),
        grid_spec=grid_spec,
        out_shape=jax.ShapeDtypeStruct((t, d), F32),
        compiler_params=_row_dma_params(),
        name="moe_combine",
    )(pos, ys, x, gt, g_final.reshape(1, d))


def _moe(hx, w_gu, w_d, layer, x, gt, g_final, seq, final):
    t = x.shape[0]
    tile = MOE_TILE
    n_tiles = t // tile + N_CLASSES
    pos, counts = _sorted_rows(hx, tile)
    tiles_per_class = jnp.ceil(counts[:N_CLASSES] / tile).astype(jnp.int32)
    ends = jnp.cumsum(tiles_per_class)
    tile_ids = jnp.arange(n_tiles, dtype=jnp.int32)
    tile_class = jnp.minimum(jnp.searchsorted(ends, tile_ids, side="right"), N_CLASSES - 1)
    tile_used = (tile_ids < ends[-1]).astype(jnp.int32)
    lo, hi = _class_experts()
    tile_lo = jnp.asarray(lo, jnp.int32)[tile_class]
    tile_hi = jnp.asarray(hi, jnp.int32)[tile_class]
    xs = _dispatch(pos, hx, n_tiles * tile)
    ys = _experts(xs, tile_lo, tile_hi, tile_used, w_gu, w_d, layer, tile)
    return _combine(pos, ys, x, gt, g_final, seq, final)


def _diff_qk_layout(w):
    d = w.shape[0]
    w = w.reshape(d, DIFF_HEADS, 2, 2, HEAD_DIM // 2)
    return w.transpose(0, 1, 3, 2, 4).reshape(d, -1)


def _mla_layouts(w_in, w_q_up, w_kv_up):
    d = w_in.shape[0]
    half = MLA_ROPE // 2
    base = MLA_Q_RANK + MLA_KV_RANK
    r1, r2 = w_in[:, base:base + half], w_in[:, base + half:]
    z = jnp.zeros((d, 64 - 2 * half), w_in.dtype)
    w_in_l = jnp.concatenate([w_in[:, :base], r1, r1, z, r2, r2, z], axis=1)

    r = w_q_up.shape[0]
    wq = w_q_up.reshape(r, MLA_HEADS, MLA_NOPE + MLA_ROPE)
    nope = wq[:, :, :MLA_NOPE].reshape(r, -1)
    q1 = wq[:, :, MLA_NOPE:MLA_NOPE + half].reshape(r, MLA_HEADS // 2, 2 * half)
    q2 = wq[:, :, MLA_NOPE + half:].reshape(r, MLA_HEADS // 2, 2 * half)
    zq = jnp.zeros((r, MLA_HEADS // 2, 64 - 2 * half), w_q_up.dtype)
    rope = jnp.concatenate([q1, zq, q2, zq], axis=2).reshape(r, -1)
    w_q_l = jnp.concatenate([nope, rope], axis=1)

    rk = w_kv_up.shape[0]
    wkv = w_kv_up.reshape(rk, MLA_HEADS, MLA_NOPE + MLA_V)
    w_kn = wkv[:, :, :MLA_NOPE].reshape(rk, -1)
    w_v = wkv[:, :, MLA_NOPE:].reshape(rk, -1)
    return w_in_l.astype(BF16), w_q_l.T.astype(BF16), w_kn.astype(BF16), w_v.T.astype(BF16)


def _router_layout(w_group, b_group, w_router, b_router):
    d = w_group.shape[0]
    pad = LANES - N_EXPERTS - N_GROUPS
    w = jnp.concatenate([w_router, w_group, jnp.zeros((d, pad), F32)], axis=1)
    b = jnp.concatenate([b_router, b_group, jnp.zeros((pad,), F32)]).reshape(1, LANES)
    w_hi = w.astype(BF16)
    w_lo = (w - w_hi.astype(F32)).astype(BF16)
    return jnp.concatenate([w_hi, w_lo], axis=1), b


def kernel(x, c, positions, norm_mix_g, norm_ffn_g, ada_w, ada_b, sb_w_qkv, sb_w_o, mla_w_in, mla_g_q, mla_g_kv, mla_w_q_up, mla_w_kv_up, mla_w_o, diff_w_qkv, diff_lam_q1, diff_lam_k1, diff_lam_q2, diff_lam_k2, diff_g_sub, diff_w_o, moe_w_group, moe_b_group, moe_w_router, moe_b_router, moe_w_gate_up, moe_w_down, final_g):
    bsz, seq, d = x.shape
    depth = ada_w.shape[0]
    xt = x.reshape(bsz * seq, d)
    mod = _adaln(c, ada_w, ada_b)
    rope_mla = _rope_tables(positions, MLA_ROPE) if depth > 1 else None
    rope_diff = _rope_tables(positions, HEAD_DIM) if depth > 2 else None

    for i in range(depth):
        sh_m, sc_m, gt_m, sh_f, sc_f, gt_f = (mod[i, :, k * d:(k + 1) * d].reshape(bsz, 1, d) for k in range(6))
        kind, j = i % N_MIXERS, i // N_MIXERS
        if kind == 0:
            q, k, v = _qkv_rows_proj(xt, norm_mix_g[i], sc_m, sh_m, sb_w_qkv[j].astype(BF16), seq)
            attn = _sb_rows_attention(q, k, v, bsz, seq)
            w_o = sb_w_o[j]
        elif kind == 1:
            w_in_l, wq_t, w_kn, wv_t = _mla_layouts(mla_w_in[j], mla_w_q_up[j], mla_w_kv_up[j])
            q_t, kn, kr, v_t = _mla_proj(xt, norm_mix_g[i], sc_m, sh_m, w_in_l, mla_g_q[j], mla_g_kv[j], wq_t, w_kn,
                                         wv_t, rope_mla, seq)
            attn = _mla_attention(q_t, kn, kr, v_t, bsz, seq)
            w_o = mla_w_o[j]
        else:
            w = diff_w_qkv[j].astype(BF16)
            q_t, k, v_t = _qkv_proj(xt, norm_mix_g[i], sc_m, sh_m, _diff_qk_layout(w[:, :d]).T,
                                    _diff_qk_layout(w[:, d:2 * d]), w[:, 2 * d:].T, seq, rope_diff)
            lam_params = jnp.stack([diff_lam_q1[j], diff_lam_k1[j], diff_lam_q2[j], diff_lam_k2[j]])
            lambda_init = 0.8 - 0.6 * math.exp(-0.3 * i)
            attn = _diff_attention(q_t, k, v_t, lam_params, diff_g_sub[j], lambda_init, bsz, seq)
            w_o = diff_w_o[j]
        w_route, b_route = _router_layout(moe_w_group[i], moe_b_group[i], moe_w_router[i], moe_b_router[i])
        xt, hx = _oproj(attn, w_o.astype(BF16), xt, gt_m, norm_ffn_g[i], sc_f, sh_f, w_route, b_route, seq)
        xt = _moe(hx, moe_w_gate_up, moe_w_down, i, xt, gt_f, final_g, seq, final=i == depth - 1)
    return xt.reshape(bsz, seq, d)
```

```python
import functools
import math

import jax
import jax.numpy as jnp
from jax import lax
from jax.experimental import pallas as pl
from jax.experimental.pallas import tpu as pltpu

F32 = jnp.float32
BF16 = jnp.bfloat16

N_MIXERS = 3
ROPE_THETA = 10000.0
EPS = 1e-6
HEAD_DIM = 64
SB_HEADS = 16
MLA_HEADS = 16
MLA_Q_RANK = 384
MLA_KV_RANK = 256
MLA_NOPE = 64
MLA_ROPE = 32
MLA_V = 64
DIFF_HEADS = 8
N_GROUPS = 4
EXPERTS_PER_GROUP = 4
N_EXPERTS = N_GROUPS * EXPERTS_PER_GROUP
EXPERT_FF = 512

LANES = 128
LOG2E = 1.4426950408889634
NEG_BIG = -1e30
VMEM_LIMIT = 56 * 1024 * 1024

ROW_TILE = 512
ATT_TILE = 256
MOE_TILE = 256
PROJ_CHUNK = 256


def _cparams(n_axes):
    return pltpu.CompilerParams(dimension_semantics=("arbitrary",) * n_axes, vmem_limit_bytes=VMEM_LIMIT)


def _dot(a, b):
    return jnp.dot(a, b, preferred_element_type=F32)


def _dot_nt(a, b):
    return lax.dot_general(a, b, (((1,), (1,)), ((), ())), preferred_element_type=F32)


def _rms(x):
    return x * lax.rsqrt(jnp.mean(x * x, axis=-1, keepdims=True) + EPS)


def _rope_blocks(y, cos, sin_signed):
    out = []
    for j in range(y.shape[1] // LANES):
        yb = y[:, j * LANES:(j + 1) * LANES]
        out.append(yb * cos + pltpu.roll(yb, 64, 1) * sin_signed)
    return out[0] if len(out) == 1 else jnp.concatenate(out, axis=1)


def _rope_rows(y, cos_t, sin_t):
    out = []
    for j in range(y.shape[0] // LANES):
        yb = y[j * LANES:(j + 1) * LANES]
        out.append(yb * cos_t + jnp.concatenate([yb[64:], yb[:64]], axis=0) * sin_t)
    return out[0] if len(out) == 1 else jnp.concatenate(out, axis=0)


def _adaln_kernel(c_ref, w_ref, b_ref, o_ref):
    c = c_ref[...]
    ca = c / (1.0 + jnp.exp(-c))
    o_ref[0] = jnp.dot(ca, w_ref[0], preferred_element_type=F32, precision=lax.Precision.HIGHEST) + b_ref[0]


def _adaln(c, ada_w, ada_b):
    depth, d, n = ada_w.shape
    bsz = c.shape[0]
    rows = 8
    cp = jnp.zeros((rows, d), F32).at[:bsz].set(c)
    tn = 1536
    out = pl.pallas_call(
        _adaln_kernel,
        grid=(depth, n // tn),
        in_specs=[
            pl.BlockSpec((rows, d), lambda i, j: (0, 0)),
            pl.BlockSpec((1, d, tn), lambda i, j: (i, 0, j)),
            pl.BlockSpec((1, 1, tn), lambda i, j: (i, 0, j)),
        ],
        out_specs=pl.BlockSpec((1, rows, tn), lambda i, j: (i, 0, j)),
        out_shape=jax.ShapeDtypeStruct((depth, rows, n), F32),
        compiler_params=_cparams(2),
        name="adaln_mod",
    )(cp, ada_w, ada_b.reshape(depth, 1, n))
    return out[:, :bsz]


def _rope_table_kernel(pos_ref, invf_ref, sign_ref, cos_ref, sin_ref, cost_ref, sint_ref):
    ang = pos_ref[...].astype(F32) * invf_ref[...]
    cos = jnp.cos(ang)
    sin = jnp.sin(ang) * sign_ref[...]
    cos_ref[...] = cos
    sin_ref[...] = sin
    cost_ref[...] = cos.T
    sint_ref[...] = sin.T


def _rope_tables(positions, dim):
    t = positions.size
    half = dim // 2
    inv_freq = ROPE_THETA ** (-jnp.arange(0, dim, 2, dtype=F32) / dim)
    invf = jnp.tile(inv_freq, LANES // half)
    sign = jnp.where(jnp.arange(LANES) < 64, -1.0, 1.0).astype(F32)
    tm = 2048
    small_r = pl.BlockSpec((1, LANES), lambda i: (0, 0))
    return pl.pallas_call(
        _rope_table_kernel,
        grid=(t // tm,),
        in_specs=[pl.BlockSpec((tm, 1), lambda i: (i, 0)), small_r, small_r],
        out_specs=[pl.BlockSpec((tm, LANES), lambda i: (i, 0))] * 2 + [pl.BlockSpec((LANES, tm), lambda i: (0, i))] * 2,
        out_shape=[jax.ShapeDtypeStruct((t, LANES), F32)] * 2 + [jax.ShapeDtypeStruct((LANES, t), F32)] * 2,
        compiler_params=_cparams(1),
        name="rope_tables",
    )(positions.reshape(t, 1), invf.reshape(1, LANES), sign.reshape(1, LANES))


def _modulated_norm(x_ref, g_ref, sc_ref, sh_ref):
    h = _rms(x_ref[...]) * g_ref[...]
    return h * (1.0 + sc_ref[0]) + sh_ref[0]


def _store_token_blocks(o_ref, w_t_ref, hb, post=None):
    tb = o_ref.shape[2]
    for r in range(0, w_t_ref.shape[0], PROJ_CHUNK):
        y = _dot_nt(w_t_ref[r:r + PROJ_CHUNK, :], hb)
        if post is not None:
            y = post(y, r)
        y = y.astype(BF16)
        for c in range(o_ref.shape[0]):
            o_ref[c, r:r + PROJ_CHUNK, :] = y[:, c * tb:(c + 1) * tb]


def _qkv_kernel(*refs, rope, q_scale):
    x_ref, g_ref, sc_ref, sh_ref, wqt_ref, wk_ref, wvt_ref = refs[:7]
    q_ref, k_ref, v_ref = refs[-3:]
    hb = _modulated_norm(x_ref, g_ref, sc_ref, sh_ref).astype(BF16)
    if rope:
        cos_ref, sin_ref, cost_ref, sint_ref = refs[7:11]
        _store_token_blocks(q_ref, wqt_ref, hb, lambda qt, r: _rope_rows(qt, cost_ref[...], sint_ref[...]) * q_scale)
    else:
        _store_token_blocks(q_ref, wqt_ref, hb, lambda qt, r: qt * q_scale)
    for r in range(0, wk_ref.shape[1], 512):
        k = _dot(hb, wk_ref[:, r:r + 512])
        if rope:
            k = _rope_blocks(k, cos_ref[...], sin_ref[...])
        k_ref[:, r:r + 512] = k.astype(BF16)
    _store_token_blocks(v_ref, wvt_ref, hb)


def _qkv_proj(x, g, sc, sh, wq_t, wk, wv_t, seq, rope=None):
    t, d = x.shape
    tm, tk = ROW_TILE, ATT_TILE
    per_batch = seq // tm
    row = pl.BlockSpec((tm, d), lambda i: (i, 0))
    mod = pl.BlockSpec((1, 1, d), lambda i: (i // per_batch, 0, 0))

    def full(a):
        return pl.BlockSpec(a.shape, lambda i: (0,) * a.ndim)

    g2 = g.reshape(1, d)
    in_specs = [row, full(g2), mod, mod, full(wq_t), full(wk), full(wv_t)]
    if rope is not None:
        tab = pl.BlockSpec((tm, LANES), lambda i: (i, 0))
        tab_t = pl.BlockSpec((LANES, tm), lambda i: (0, i))
        in_specs += [tab, tab, tab_t, tab_t]
    nq, nk, nv = wq_t.shape[0], wk.shape[1], wv_t.shape[0]
    return pl.pallas_call(
        functools.partial(_qkv_kernel, rope=rope is not None, q_scale=HEAD_DIM ** -0.5),
        grid=(t // tm,),
        in_specs=in_specs,
        out_specs=[pl.BlockSpec((tm // tk, nq, tk), lambda i: (i, 0, 0)), pl.BlockSpec((tm, nk), lambda i: (i, 0)),
                   pl.BlockSpec((tm // tk, nv, tk), lambda i: (i, 0, 0))],
        out_shape=[jax.ShapeDtypeStruct((t // tk, nq, tk), BF16), jax.ShapeDtypeStruct((t, nk), BF16),
                   jax.ShapeDtypeStruct((t // tk, nv, tk), BF16)],
        compiler_params=_cparams(1),
        name="qkv_proj",
    )(x, g2, sc, sh, wq_t, wk, wv_t, *(rope or ()))


def _mla_proj_kernel(x_ref, g_ref, sc_ref, sh_ref, win_ref, gq_ref, gkv_ref, wqt_ref, wkn_ref, wvt_ref,
                     cos_ref, sin_ref, cost_ref, sint_ref, q_ref, kn_ref, kr_ref, v_ref):
    hb = _modulated_norm(x_ref, g_ref, sc_ref, sh_ref).astype(BF16)
    lat = _dot(hb, win_ref[...])
    cq = (_rms(lat[:, :MLA_Q_RANK]) * gq_ref[...]).astype(BF16)
    ckv = (_rms(lat[:, MLA_Q_RANK:MLA_Q_RANK + MLA_KV_RANK]) * gkv_ref[...]).astype(BF16)
    kr_ref[...] = _rope_blocks(lat[:, MLA_Q_RANK + MLA_KV_RANK:], cos_ref[...], sin_ref[...]).astype(BF16)
    n_nope = MLA_HEADS * MLA_NOPE
    _store_token_blocks(q_ref, wqt_ref, cq,
                        lambda qt, r: _rope_rows(qt, cost_ref[...], sint_ref[...]) if r >= n_nope else qt)
    kn_ref[...] = _dot(ckv, wkn_ref[...]).astype(BF16)
    _store_token_blocks(v_ref, wvt_ref, ckv)


def _mla_proj(x, g, sc, sh, w_in, g_q, g_kv, wq_t, wkn, wv_t, rope, seq):
    t, d = x.shape
    tm, tk = ROW_TILE, ATT_TILE
    per_batch = seq // tm
    row = pl.BlockSpec((tm, d), lambda i: (i, 0))
    mod = pl.BlockSpec((1, 1, d), lambda i: (i // per_batch, 0, 0))

    def full(a):
        return pl.BlockSpec(a.shape, lambda i: (0,) * a.ndim)

    tab = pl.BlockSpec((tm, LANES), lambda i: (i, 0))
    tab_t = pl.BlockSpec((LANES, tm), lambda i: (0, i))
    g2, gq2, gkv2 = g.reshape(1, d), g_q.reshape(1, -1), g_kv.reshape(1, -1)
    nq, nk, nv = wq_t.shape[0], wkn.shape[1], wv_t.shape[0]
    return pl.pallas_call(
        _mla_proj_kernel,
        grid=(t // tm,),
        in_specs=[row, full(g2), mod, mod, full(w_in), full(gq2), full(gkv2), full(wq_t), full(wkn), full(wv_t),
                  tab, tab, tab_t, tab_t],
        out_specs=[pl.BlockSpec((tm // tk, nq, tk), lambda i: (i, 0, 0)), pl.BlockSpec((tm, nk), lambda i: (i, 0)),
                   pl.BlockSpec((tm, LANES), lambda i: (i, 0)), pl.BlockSpec((tm // tk, nv, tk), lambda i: (i, 0, 0))],
        out_shape=[jax.ShapeDtypeStruct((t // tk, nq, tk), BF16), jax.ShapeDtypeStruct((t, nk), BF16),
                   jax.ShapeDtypeStruct((t, LANES), BF16), jax.ShapeDtypeStruct((t // tk, nv, tk), BF16)],
        compiler_params=_cparams(1),
        name="mla_proj",
    )(x, g2, sc, sh, w_in, gq2, gkv2, wq_t, wkn, wv_t, *rope)


SOFTMAX_ITEMS_PER_ITER = 2
SB_ITEMS_PER_ITER = 1
SOFTMAX_ROWS = 32


def _attention_items(nq, depth, per_iter):
    rows = [(qi, qi - s, int(s == 0), int(s == qi), qi & 1, qi & 3) for qi in range(nq) for s in range(qi + 1)]
    rows += [(0, 0, 1, 0, nq & 1, nq & 3)] * (-len(rows) % per_iter + (depth - 1) * per_iter)
    return jnp.asarray(list(zip(*rows)), jnp.int32)


def _diag_bias(t, strict):
    key = jnp.arange(t)[:, None]
    query = jnp.arange(t)[None, :]
    masked = key >= query if strict else key > query
    return jnp.stack([jnp.zeros((t, t), F32), jnp.where(masked, NEG_BIG, 0.0).astype(F32)])


def _run_pipeline(stages, n_items, finalize, per_iter):
    depth = len(stages)

    def iteration(it, static):
        for k in reversed(range(depth)):
            if static and it < k:
                continue
            for u in range(per_iter):
                stages[k]((it - k) * per_iter + u)

    for it in range(depth - 1):
        iteration(it, True)

    def body(it, c):
        iteration(it, False)
        for u in range(per_iter):
            finalize((it - (depth - 1)) * per_iter + u)
        return c

    lax.fori_loop(depth - 1, n_items // per_iter, body, 0)


def _pipe_call(kernel_fn, name, depth, bsz, seq, pairs, in_specs, args, scratch, per_iter):
    t = ATT_TILE
    items = _attention_items(seq // t, depth, per_iter)
    grid_spec = pltpu.PrefetchScalarGridSpec(
        num_scalar_prefetch=1,
        grid=(bsz, pairs),
        in_specs=in_specs,
        out_specs=pl.BlockSpec((seq, LANES), lambda b, p, it: (b, p)),
        scratch_shapes=scratch,
    )
    return pl.pallas_call(
        kernel_fn,
        grid_spec=grid_spec,
        out_shape=jax.ShapeDtypeStruct((bsz * seq, pairs * LANES), BF16),
        compiler_params=_cparams(2),
        name=name,
    )(items, *args)


def _blocked_spec(seq, t, offset=0):
    return pl.BlockSpec((seq // t, LANES, t), lambda b, p, it: (b, offset + p, 0))


def _const_spec(a):
    return pl.BlockSpec(a.shape, lambda b, p, it: (0,) * a.ndim)


def _keep_rows(x, keep):
    n_rows, n = x.shape
    parts, pos = [], 0
    for a, b in keep:
        if a > pos:
            parts.append(jnp.zeros((a - pos, n), x.dtype))
        parts.append(x[a:b])
        pos = b
    if pos < n_rows:
        parts.append(jnp.zeros((n_rows - pos, n), x.dtype))
    return jnp.concatenate(parts, axis=0)


def _store_tile(o_ref, qi, o_t):
    t = o_t.shape[1]
    o_ref[pl.ds(pl.multiple_of(qi * t, t), t), :] = o_t.T.astype(o_ref.dtype)


def _softmax_pipeline(items_ref, q_heads_of, k_ref, vt_ref, bias_ref, scr, c_log2, finalize_tile):
    s_scr, p_scr, alpha_scr, lfin_scr, m_ref, l_ref, acc_ref = scr
    t = s_scr.shape[-1]
    n_slots = s_scr.shape[0]
    m_ref[...] = jnp.zeros_like(m_ref)
    l_ref[...] = jnp.zeros_like(l_ref)
    acc_ref[...] = jnp.zeros_like(acc_ref)

    def scores(w):
        qi, j, first = items_ref[0, w], items_ref[1, w], items_ref[2, w]
        kb = k_ref[pl.ds(pl.multiple_of(j * t, t), t), :]
        bias = bias_ref[first]
        for hd, qt in enumerate(q_heads_of(qi)):
            s_scr[w & (n_slots - 1), hd] = _dot(kb, qt) * c_log2 + bias

    def softmax(w):
        slot = w & (n_slots - 1)
        first = items_ref[2, w] == 1
        for hd in range(2):
            mx = s_scr[slot, hd, 0:SOFTMAX_ROWS, :]
            for r in range(SOFTMAX_ROWS, t, SOFTMAX_ROWS):
                mx = jnp.maximum(mx, s_scr[slot, hd, r:r + SOFTMAX_ROWS, :])
            m_old = jnp.where(first, NEG_BIG, m_ref[hd])
            m_new = jnp.maximum(m_old, jnp.max(mx, axis=0, keepdims=True))
            alpha = jnp.where(first, 0.0, jnp.exp2(m_old - m_new))
            psum = None
            for r in range(0, t, SOFTMAX_ROWS):
                p = jnp.exp2(s_scr[slot, hd, r:r + SOFTMAX_ROWS, :] - m_new)
                p_scr[slot, hd, r:r + SOFTMAX_ROWS, :] = p.astype(BF16)
                psum = p if psum is None else psum + p
            l_new = alpha * l_ref[hd] + jnp.sum(psum, axis=0, keepdims=True)
            l_ref[hd] = l_new
            m_ref[hd] = m_new
            alpha_scr[slot, hd] = alpha
            lfin_scr[items_ref[5, w], hd] = l_new

    def values(w):
        slot = w & (n_slots - 1)
        vb = vt_ref[items_ref[1, w]]
        par = items_ref[4, w]
        for hd in range(2):
            acc_ref[par, hd] = alpha_scr[slot, hd] * acc_ref[par, hd] + _dot(vb, p_scr[slot, hd])

    def finalize(w):
        @pl.when(items_ref[3, w] == 1)
        def _():
            finalize_tile(items_ref[0, w], acc_ref.at[items_ref[4, w]], lfin_scr[items_ref[5, w]])

    _run_pipeline([scores, softmax, values], items_ref.shape[1], finalize, n_slots // 2)


def _mla_pipe_kernel(items_ref, qn_ref, qr_ref, kn_ref, kr_ref, vt_ref, bias_ref, o_ref, kcat_ref, *scr, scale):
    kcat_ref[:, :LANES] = kn_ref[...]
    kcat_ref[:, LANES:] = kr_ref[...]
    half = MLA_ROPE // 2

    def q_heads_of(qi):
        qn = qn_ref[qi]
        qr = qr_ref[qi]
        return (
            jnp.concatenate([_keep_rows(qn, [(0, MLA_NOPE)]), _keep_rows(qr, [(0, half), (64, 64 + half)])], axis=0),
            jnp.concatenate([_keep_rows(qn, [(MLA_NOPE, 2 * MLA_NOPE)]),
                             _keep_rows(qr, [(half, 2 * half), (64 + half, 64 + 2 * half)])], axis=0),
        )

    def finalize_tile(qi, acc_ref, l_fin):
        o_t = jnp.concatenate([acc_ref[0, :MLA_V] / l_fin[0], acc_ref[1, MLA_V:] / l_fin[1]], axis=0)
        _store_tile(o_ref, qi, o_t)

    _softmax_pipeline(items_ref, q_heads_of, kcat_ref, vt_ref, bias_ref, scr, scale * LOG2E, finalize_tile)


def _diff_pipe_kernel(items_ref, q_ref, k_ref, vt_ref, bias_ref, lam_ref, gsub_ref, o_ref, *scr, lambda_init):
    def q_heads_of(qi):
        qt = q_ref[qi]
        return _keep_rows(qt, [(0, 32), (64, 96)]), _keep_rows(qt, [(32, 64), (96, 128)])

    def finalize_tile(qi, acc_ref, l_fin):
        lp = lam_ref[...]
        lam = (jnp.exp(jnp.sum(lp[0:1] * lp[1:2], axis=1, keepdims=True))
               - jnp.exp(jnp.sum(lp[2:3] * lp[3:4], axis=1, keepdims=True)) + lambda_init)
        o_t = acc_ref[0] / l_fin[0] - lam * (acc_ref[1] / l_fin[1])
        o_t = o_t * lax.rsqrt(jnp.mean(o_t * o_t, axis=0, keepdims=True) + EPS)
        t = o_t.shape[1]
        o = o_t.T * gsub_ref[...] * (1.0 - lambda_init)
        o_ref[pl.ds(pl.multiple_of(qi * t, t), t), :] = o.astype(o_ref.dtype)

    _softmax_pipeline(items_ref, q_heads_of, k_ref, vt_ref, bias_ref, scr, LOG2E, finalize_tile)


def _softmax_scratch(t):
    n = 2 * SOFTMAX_ITEMS_PER_ITER
    return [pltpu.VMEM((n, 2, t, t), F32), pltpu.VMEM((n, 2, t, t), BF16), pltpu.VMEM((n, 2, 1, t), F32),
            pltpu.VMEM((4, 2, 1, t), F32), pltpu.VMEM((2, 1, t), F32), pltpu.VMEM((2, 1, t), F32),
            pltpu.VMEM((2, 2, LANES, t), F32)]


def _mla_attention(q_t, kn, kr, v_t, bsz, seq):
    t = ATT_TILE
    pairs = MLA_HEADS // 2
    bias = _diag_bias(t, False)
    in_specs = [_blocked_spec(seq, t), _blocked_spec(seq, t, pairs),
                pl.BlockSpec((seq, LANES), lambda b, p, it: (b, p)), pl.BlockSpec((seq, LANES), lambda b, p, it: (b, 0)),
                _blocked_spec(seq, t), _const_spec(bias)]
    scratch = [pltpu.VMEM((seq, 2 * LANES), BF16)] + _softmax_scratch(t)
    kernel_fn = functools.partial(_mla_pipe_kernel, scale=(MLA_NOPE + MLA_ROPE) ** -0.5)
    return _pipe_call(kernel_fn, "mla_attention", 3, bsz, seq, pairs, in_specs, (q_t, q_t, kn, kr, v_t, bias),
                      scratch, SOFTMAX_ITEMS_PER_ITER)


def _diff_attention(q_t, k, v_t, lam_params, g_sub, lambda_init, bsz, seq):
    t = ATT_TILE
    bias = _diag_bias(t, False)
    g2 = g_sub.reshape(1, LANES)
    in_specs = [_blocked_spec(seq, t), pl.BlockSpec((seq, LANES), lambda b, p, it: (b, p)), _blocked_spec(seq, t),
                _const_spec(bias), _const_spec(lam_params), _const_spec(g2)]
    kernel_fn = functools.partial(_diff_pipe_kernel, lambda_init=lambda_init)
    return _pipe_call(kernel_fn, "diff_attention", 3, bsz, seq, DIFF_HEADS, in_specs,
                      (q_t, k, v_t, bias, lam_params, g2), _softmax_scratch(t), SOFTMAX_ITEMS_PER_ITER)


def _sb_pipe_kernel(items_ref, q_ref, k_ref, vt_ref, lmat_ref, bias_ref, o_ref,
                    z_scr, sp_scr, cs_scr, w_scr, acc_ref, carry_ref):
    t = z_scr.shape[-1]
    n_slots = sp_scr.shape[0]
    lmat = lmat_ref[...]
    acc_ref[...] = jnp.zeros_like(acc_ref)
    carry_ref[...] = jnp.zeros_like(carry_ref)

    def scores(w):
        qi, j, first = items_ref[0, w], items_ref[1, w], items_ref[2, w]
        kb = k_ref[pl.ds(pl.multiple_of(j * t, t), t), :]
        qt = q_ref[qi]
        bias = bias_ref[first]
        for hd, rows in enumerate(([(0, HEAD_DIM)], [(HEAD_DIM, 2 * HEAD_DIM)])):
            z_scr[w & (2 * n_slots - 1), hd] = _dot(kb, _keep_rows(qt, rows)) + bias

    def softplus(w):
        for hd in range(2):
            z = z_scr[w & (2 * n_slots - 1), hd]
            sp = jnp.maximum(z, 0.0) + jnp.log(1.0 + jnp.exp2(jnp.abs(z) * -LOG2E))
            sp_scr[w & (n_slots - 1), hd] = sp.astype(BF16)

    def sums(w):
        for hd in range(2):
            cs_scr[w & (n_slots - 1), hd] = _dot(lmat, sp_scr[w & (n_slots - 1), hd])

    def weights(w):
        first = items_ref[2, w] == 1
        for hd in range(2):
            cs = cs_scr[w & (n_slots - 1), hd]
            carry = jnp.where(first, 0.0, carry_ref[hd])
            w_scr[w & (n_slots - 1), hd] = jnp.exp(z_scr[w & (2 * n_slots - 1), hd] - cs - carry).astype(BF16)
            carry_ref[hd] = carry + cs[0:1, :]

    def values(w):
        vb = vt_ref[items_ref[1, w]]
        keep = jnp.where(items_ref[2, w] == 1, 0.0, 1.0)
        par = items_ref[4, w]
        for hd in range(2):
            acc_ref[par, hd] = keep * acc_ref[par, hd] + _dot(vb, w_scr[w & (n_slots - 1), hd])

    def finalize(w):
        @pl.when(items_ref[3, w] == 1)
        def _():
            par = items_ref[4, w]
            o_t = jnp.concatenate([acc_ref[par, 0, :HEAD_DIM], acc_ref[par, 1, HEAD_DIM:]], axis=0)
            _store_tile(o_ref, items_ref[0, w], o_t)

    _run_pipeline([scores, softplus, sums, weights, values], items_ref.shape[1], finalize, n_slots // 2)


def _sb_attention(q_t, k, v_t, bsz, seq):
    t = ATT_TILE
    lmat = (jnp.arange(t)[None, :] >= jnp.arange(t)[:, None]).astype(BF16)
    bias = _diag_bias(t, True)
    in_specs = [_blocked_spec(seq, t), pl.BlockSpec((seq, LANES), lambda b, p, it: (b, p)), _blocked_spec(seq, t),
                _const_spec(lmat), _const_spec(bias)]
    n = 2 * SB_ITEMS_PER_ITER
    scratch = [pltpu.VMEM((2 * n, 2, t, t), F32), pltpu.VMEM((n, 2, t, t), BF16), pltpu.VMEM((n, 2, t, t), F32),
               pltpu.VMEM((n, 2, t, t), BF16), pltpu.VMEM((2, 2, LANES, t), F32), pltpu.VMEM((2, 1, t), F32)]
    return _pipe_call(_sb_pipe_kernel, "sb_attention", 5, bsz, seq, SB_HEADS // 2, in_specs,
                      (q_t, k, v_t, lmat, bias), scratch, SB_ITEMS_PER_ITER)


PAIRS_PER_GROUP = EXPERTS_PER_GROUP * (EXPERTS_PER_GROUP - 1) // 2
N_CLASSES = N_GROUPS * PAIRS_PER_GROUP


def _class_experts():
    lo, hi = [], []
    for g in range(N_GROUPS):
        for a in range(EXPERTS_PER_GROUP):
            for b in range(a + 1, EXPERTS_PER_GROUP):
                lo.append(g * EXPERTS_PER_GROUP + a)
                hi.append(g * EXPERTS_PER_GROUP + b)
    return lo, hi


def _route(logits):
    lane = lax.broadcasted_iota(jnp.int32, logits.shape, 1).astype(F32)
    big = jnp.float32(1 << 20)

    def top1(vals):
        v = jnp.max(vals, axis=1, keepdims=True)
        i = jnp.min(jnp.where(vals == v, lane, big), axis=1, keepdims=True)
        return v, i

    is_group = (lane >= N_EXPERTS) & (lane < N_EXPERTS + N_GROUPS)
    gl = jnp.where(is_group, logits, -jnp.inf)
    gmax, gidx = top1(gl)
    g_w = 1.0 / jnp.sum(jnp.exp(gl - gmax), axis=1, keepdims=True)
    first = (gidx - N_EXPERTS) * EXPERTS_PER_GROUP
    el = jnp.where((lane >= first) & (lane < first + EXPERTS_PER_GROUP), logits, -jnp.inf)
    v1, i1 = top1(el)
    v2, i2 = top1(jnp.where(lane == i1, -jnp.inf, el))
    e2 = jnp.exp(v2 - v1)
    w1 = g_w / (1.0 + e2)
    w2 = g_w * e2 / (1.0 + e2)
    group = gidx - N_EXPERTS
    a = jnp.minimum(i1, i2) - first
    b = jnp.maximum(i1, i2) - first
    pair = a * (2 * EXPERTS_PER_GROUP - 1 - a) * 0.5 + (b - a - 1.0)
    cls = group * PAIRS_PER_GROUP + pair
    w_lo = jnp.where(i1 < i2, w1, w2)
    w_hi = jnp.where(i1 < i2, w2, w1)
    return jnp.where(lane == 0, cls, jnp.where(lane == 1, w_lo, jnp.where(lane == 2, w_hi, 0.0)))


def _oproj_kernel(a_ref, wo_ref, x_ref, gt_ref, g_ref, sc_ref, sh_ref, wr_ref, br_ref, xo_ref, hx_ref):
    d = x_ref.shape[1]
    xn = x_ref[...] + gt_ref[0] * _dot(a_ref[...], wo_ref[...])
    xo_ref[...] = xn
    h = _rms(xn) * g_ref[...]
    h = h * (1.0 + sc_ref[0]) + sh_ref[0]
    hx_ref[:, :d] = h
    h_hi = h.astype(BF16)
    h_lo = (h - h_hi.astype(F32)).astype(BF16)
    both = _dot(h_hi, wr_ref[...])
    logits = both[:, :LANES] + both[:, LANES:] + _dot(h_lo, wr_ref[:, :LANES]) + br_ref[...]
    hx_ref[:, d:] = _route(logits)


def _oproj(attn, w_o, x, gt, g, sc, sh, w_route, b_route, seq):
    t, d = x.shape
    tm = ROW_TILE
    per_batch = seq // tm
    row = pl.BlockSpec((tm, d), lambda i: (i, 0))
    mod = pl.BlockSpec((1, 1, d), lambda i: (i // per_batch, 0, 0))

    def full(a):
        return pl.BlockSpec(a.shape, lambda i: (0,) * a.ndim)

    g2 = g.reshape(1, d)
    return pl.pallas_call(
        _oproj_kernel,
        grid=(t // tm,),
        in_specs=[pl.BlockSpec((tm, attn.shape[1]), lambda i: (i, 0)), full(w_o), row, mod, full(g2), mod, mod,
                  full(w_route), full(b_route)],
        out_specs=[row, pl.BlockSpec((tm, d + LANES), lambda i: (i, 0))],
        out_shape=[jax.ShapeDtypeStruct((t, d), F32), jax.ShapeDtypeStruct((t, d + LANES), F32)],
        compiler_params=_cparams(1),
        name="oproj_router",
    )(attn, w_o, x, gt, g2, sc, sh, w_route, b_route)


ROWS_PER_STEP = 1024


def _sort_kernel(route_ref, ltri_ref, utri_ref, pos_ref, counts_ref, cnt_scr, off_scr, *, tile):
    phase, i = pl.program_id(0), pl.program_id(1)
    lane = lax.broadcasted_iota(jnp.int32, route_ref.shape, 1).astype(F32)
    onehot = jnp.where(lane == route_ref[:, 0:1], 1.0, 0.0)

    @pl.when((phase == 0) & (i == 0))
    def _():
        cnt_scr[...] = jnp.zeros_like(cnt_scr)

    @pl.when((phase == 1) & (i == 0))
    def _():
        cnt = cnt_scr[...]
        counts_ref[...] = cnt
        n_tiles = jnp.floor((cnt + (tile - 1.0)) * (1.0 / tile))
        before = _dot(jnp.broadcast_to(n_tiles, (8, LANES)).astype(BF16), utri_ref[...])
        off_scr[...] = before[0:1] * tile
        cnt_scr[...] = jnp.zeros_like(cnt_scr)

    @pl.when(phase == 1)
    def _():
        earlier = _dot(ltri_ref[...], onehot.astype(BF16))
        row = jnp.sum(onehot * (earlier + off_scr[...] + cnt_scr[...]), axis=1, keepdims=True)
        pos_ref[...] = row.astype(jnp.int32)

    cnt_scr[...] += jnp.sum(onehot, axis=0, keepdims=True)


def _sorted_rows(hx, tile):
    t, w = hx.shape
    tm = 2 * ROW_TILE
    ltri = (jnp.arange(tm)[:, None] > jnp.arange(tm)[None, :]).astype(BF16)
    utri = (jnp.arange(LANES)[:, None] < jnp.arange(LANES)[None, :]).astype(BF16)
    pos, counts = pl.pallas_call(
        functools.partial(_sort_kernel, tile=tile),
        grid=(2, t // tm),
        in_specs=[pl.BlockSpec((tm, LANES), lambda ph, i: (i, w // LANES - 1)),
                  pl.BlockSpec((tm, tm), lambda ph, i: (0, 0)), pl.BlockSpec((LANES, LANES), lambda ph, i: (0, 0))],
        out_specs=[pl.BlockSpec((tm, 1), lambda ph, i: (i * ph, 0)), pl.BlockSpec((1, LANES), lambda ph, i: (0, 0))],
        out_shape=[jax.ShapeDtypeStruct((t, 1), jnp.int32), jax.ShapeDtypeStruct((1, LANES), F32)],
        scratch_shapes=[pltpu.VMEM((1, LANES), F32), pltpu.VMEM((1, LANES), F32)],
        compiler_params=_cparams(2),
        name="moe_sort",
    )(hx, ltri, utri)
    return pos.reshape(t), counts.reshape(LANES)


def _row_dma_params():
    return pltpu.CompilerParams(dimension_semantics=("arbitrary",), vmem_limit_bytes=VMEM_LIMIT,
                                disable_bounds_checks=True)


def _dispatch_kernel(pos_ref, hx_ref, zeros_hbm, xs_hbm, sem):
    del zeros_hbm
    rows = hx_ref.shape[0]
    base = pl.program_id(0) * rows

    def issue(r, c):
        pltpu.make_async_copy(hx_ref.at[pl.ds(r, 1)], xs_hbm.at[pl.ds(pos_ref[base + r], 1)], sem).start()
        return c

    lax.fori_loop(0, rows, issue, 0, unroll=8)
    pltpu.make_async_copy(hx_ref, xs_hbm.at[pl.ds(0, rows)], sem).wait()


def _dispatch(pos, hx, n_rows):
    t, w = hx.shape
    any_spec = pl.BlockSpec(memory_space=pl.ANY)
    grid_spec = pltpu.PrefetchScalarGridSpec(
        num_scalar_prefetch=1, grid=(t // ROWS_PER_STEP,),
        in_specs=[pl.BlockSpec((ROWS_PER_STEP, w), lambda i, pos: (i, 0)), any_spec], out_specs=any_spec,
        scratch_shapes=[pltpu.SemaphoreType.DMA])
    return pl.pallas_call(
        _dispatch_kernel,
        grid_spec=grid_spec,
        out_shape=jax.ShapeDtypeStruct((n_rows, w), F32),
        input_output_aliases={2: 0},
        compiler_params=_row_dma_params(),
        name="moe_dispatch",
    )(pos, hx, jnp.zeros((n_rows, w), F32))


def _experts_kernel(lo_ref, hi_ref, used_ref, xs_ref, wgu_lo_ref, wgu_hi_ref, wd_lo_ref, wd_hi_ref, ys_ref):
    del lo_ref, hi_ref
    d = ys_ref.shape[1]

    @pl.when(used_ref[pl.program_id(0)] == 0)
    def _():
        ys_ref[...] = jnp.zeros_like(ys_ref)

    @pl.when(used_ref[pl.program_id(0)] == 1)
    def _():
        xb = xs_ref[:, :d].astype(BF16)
        route = xs_ref[:, d:]
        route_lane = lax.broadcasted_iota(jnp.int32, route.shape, 1)
        y = None
        for lane, wgu_ref, wd_ref in ((1, wgu_lo_ref, wd_lo_ref), (2, wgu_hi_ref, wd_hi_ref)):
            gu = _dot(xb, wgu_ref[0, 0].astype(BF16))
            gate, up = gu[:, :EXPERT_FF], gu[:, EXPERT_FF:]
            act = (gate / (1.0 + jnp.exp(-gate)) * up).astype(BF16)
            gate_w = jnp.sum(jnp.where(route_lane == lane, route, 0.0), axis=1, keepdims=True)
            term = gate_w * _dot(act, wd_ref[0, 0].astype(BF16))
            y = term if y is None else y + term
        ys_ref[...] = y


def _experts(xs, tile_lo, tile_hi, tile_used, w_gu, w_d, layer, tile):
    n_rows, w = xs.shape
    d = w - LANES
    gu_block, d_block = (1, 1) + w_gu.shape[2:], (1, 1) + w_d.shape[2:]
    grid_spec = pltpu.PrefetchScalarGridSpec(
        num_scalar_prefetch=3,
        grid=(n_rows // tile,),
        in_specs=[pl.BlockSpec((tile, w), lambda i, lo, hi, used: (i, 0)),
                  pl.BlockSpec(gu_block, lambda i, lo, hi, used: (layer, lo[i], 0, 0)),
                  pl.BlockSpec(gu_block, lambda i, lo, hi, used: (layer, hi[i], 0, 0)),
                  pl.BlockSpec(d_block, lambda i, lo, hi, used: (layer, lo[i], 0, 0)),
                  pl.BlockSpec(d_block, lambda i, lo, hi, used: (layer, hi[i], 0, 0))],
        out_specs=pl.BlockSpec((tile, d), lambda i, lo, hi, used: (i, 0)),
    )
    return pl.pallas_call(
        _experts_kernel,
        grid_spec=grid_spec,
        out_shape=jax.ShapeDtypeStruct((n_rows, d), F32),
        compiler_params=_cparams(1),
        name="moe_experts",
    )(tile_lo, tile_hi, tile_used, xs, w_gu, w_gu, w_d, w_d)


def _combine_kernel(pos_ref, ys_hbm, x_ref, gt_ref, g_ref, o_ref, buf, sem, *, final):
    tm = x_ref.shape[0]
    base = pl.program_id(0) * tm

    def issue(r, c):
        pltpu.make_async_copy(ys_hbm.at[pl.ds(pos_ref[base + r], 1)], buf.at[pl.ds(r, 1)], sem).start()
        return c

    lax.fori_loop(0, tm, issue, 0, unroll=8)
    pltpu.make_async_copy(ys_hbm.at[pl.ds(0, tm)], buf, sem).wait()
    y = x_ref[...] + gt_ref[0] * buf[...]
    o_ref[...] = _rms(y) * g_ref[...] if final else y


def _combine(pos, ys, x, gt, g_final, seq, final):
    t, d = x.shape
    tm = ROW_TILE
    per_batch = seq // tm
    grid_spec = pltpu.PrefetchScalarGridSpec(
        num_scalar_prefetch=1,
        grid=(t // tm,),
        in_specs=[pl.BlockSpec(memory_space=pl.ANY), pl.BlockSpec((tm, d), lambda i, pos: (i, 0)),
                  pl.BlockSpec((1, 1, d), lambda i, pos: (i // per_batch, 0, 0)),
                  pl.BlockSpec((1, d), lambda i, pos: (0, 0))],
        out_specs=pl.BlockSpec((tm, d), lambda i, pos: (i, 0)),
        scratch_shapes=[pltpu.VMEM((tm, d), F32), pltpu.SemaphoreType.DMA],
    )
    return pl.pallas_call(
        functools.partial(_combine_kernel, final=final),
        grid_spec=grid_spec,
        out_shape=jax.ShapeDtypeStruct((t, d), F32),
        compiler_params=_row_dma_params(),
        name="moe_combine",
    )(pos, ys, x, gt, g_final.reshape(1, d))


def _moe(hx, w_gu, w_d, layer, x, gt, g_final, seq, final):
    t = x.shape[0]
    tile = MOE_TILE
    n_tiles = t // tile + N_CLASSES
    pos, counts = _sorted_rows(hx, tile)
    tiles_per_class = jnp.ceil(counts[:N_CLASSES] / tile).astype(jnp.int32)
    ends = jnp.cumsum(tiles_per_class)
    tile_ids = jnp.arange(n_tiles, dtype=jnp.int32)
    tile_class = jnp.minimum(jnp.searchsorted(ends, tile_ids, side="right"), N_CLASSES - 1)
    tile_used = (tile_ids < ends[-1]).astype(jnp.int32)
    lo, hi = _class_experts()
    tile_lo = jnp.asarray(lo, jnp.int32)[tile_class]
    tile_hi = jnp.asarray(hi, jnp.int32)[tile_class]
    xs = _dispatch(pos, hx, n_tiles * tile)
    ys = _experts(xs, tile_lo, tile_hi, tile_used, w_gu, w_d, layer, tile)
    return _combine(pos, ys, x, gt, g_final, seq, final)


def _diff_qk_layout(w):
    d = w.shape[0]
    w = w.reshape(d, DIFF_HEADS, 2, 2, HEAD_DIM // 2)
    return w.transpose(0, 1, 3, 2, 4).reshape(d, -1)


def _mla_layouts(w_in, w_q_up, w_kv_up):
    d = w_in.shape[0]
    half = MLA_ROPE // 2
    base = MLA_Q_RANK + MLA_KV_RANK
    r1, r2 = w_in[:, base:base + half], w_in[:, base + half:]
    z = jnp.zeros((d, 64 - 2 * half), w_in.dtype)
    w_in_l = jnp.concatenate([w_in[:, :base], r1, r1, z, r2, r2, z], axis=1)

    r = w_q_up.shape[0]
    wq = w_q_up.reshape(r, MLA_HEADS, MLA_NOPE + MLA_ROPE)
    nope = wq[:, :, :MLA_NOPE].reshape(r, -1)
    q1 = wq[:, :, MLA_NOPE:MLA_NOPE + half].reshape(r, MLA_HEADS // 2, 2 * half)
    q2 = wq[:, :, MLA_NOPE + half:].reshape(r, MLA_HEADS // 2, 2 * half)
    zq = jnp.zeros((r, MLA_HEADS // 2, 64 - 2 * half), w_q_up.dtype)
    rope = jnp.concatenate([q1, zq, q2, zq], axis=2).reshape(r, -1)
    w_q_l = jnp.concatenate([nope, rope], axis=1)

    rk = w_kv_up.shape[0]
    wkv = w_kv_up.reshape(rk, MLA_HEADS, MLA_NOPE + MLA_V)
    w_kn = wkv[:, :, :MLA_NOPE].reshape(rk, -1)
    w_v = wkv[:, :, MLA_NOPE:].reshape(rk, -1)
    return w_in_l.astype(BF16), w_q_l.T.astype(BF16), w_kn.astype(BF16), w_v.T.astype(BF16)


def _router_layout(w_group, b_group, w_router, b_router):
    d = w_group.shape[0]
    pad = LANES - N_EXPERTS - N_GROUPS
    w = jnp.concatenate([w_router, w_group, jnp.zeros((d, pad), F32)], axis=1)
    b = jnp.concatenate([b_router, b_group, jnp.zeros((pad,), F32)]).reshape(1, LANES)
    w_hi = w.astype(BF16)
    w_lo = (w - w_hi.astype(F32)).astype(BF16)
    return jnp.concatenate([w_hi, w_lo], axis=1), b


def kernel(x, c, positions, norm_mix_g, norm_ffn_g, ada_w, ada_b, sb_w_qkv, sb_w_o, mla_w_in, mla_g_q, mla_g_kv, mla_w_q_up, mla_w_kv_up, mla_w_o, diff_w_qkv, diff_lam_q1, diff_lam_k1, diff_lam_q2, diff_lam_k2, diff_g_sub, diff_w_o, moe_w_group, moe_b_group, moe_w_router, moe_b_router, moe_w_gate_up, moe_w_down, final_g):
    bsz, seq, d = x.shape
    depth = ada_w.shape[0]
    xt = x.reshape(bsz * seq, d)
    mod = _adaln(c, ada_w, ada_b)
    rope_mla = _rope_tables(positions, MLA_ROPE) if depth > 1 else None
    rope_diff = _rope_tables(positions, HEAD_DIM) if depth > 2 else None

    for i in range(depth):
        sh_m, sc_m, gt_m, sh_f, sc_f, gt_f = (mod[i, :, k * d:(k + 1) * d].reshape(bsz, 1, d) for k in range(6))
        kind, j = i % N_MIXERS, i // N_MIXERS
        if kind == 0:
            w = sb_w_qkv[j].astype(BF16)
            q_t, k, v_t = _qkv_proj(xt, norm_mix_g[i], sc_m, sh_m, w[:, :d].T, w[:, d:2 * d], w[:, 2 * d:].T, seq)
            attn = _sb_attention(q_t, k, v_t, bsz, seq)
            w_o = sb_w_o[j]
        elif kind == 1:
            w_in_l, wq_t, w_kn, wv_t = _mla_layouts(mla_w_in[j], mla_w_q_up[j], mla_w_kv_up[j])
            q_t, kn, kr, v_t = _mla_proj(xt, norm_mix_g[i], sc_m, sh_m, w_in_l, mla_g_q[j], mla_g_kv[j], wq_t, w_kn,
                                         wv_t, rope_mla, seq)
            attn = _mla_attention(q_t, kn, kr, v_t, bsz, seq)
            w_o = mla_w_o[j]
        else:
            w = diff_w_qkv[j].astype(BF16)
            q_t, k, v_t = _qkv_proj(xt, norm_mix_g[i], sc_m, sh_m, _diff_qk_layout(w[:, :d]).T,
                                    _diff_qk_layout(w[:, d:2 * d]), w[:, 2 * d:].T, seq, rope_diff)
            lam_params = jnp.stack([diff_lam_q1[j], diff_lam_k1[j], diff_lam_q2[j], diff_lam_k2[j]])
            lambda_init = 0.8 - 0.6 * math.exp(-0.3 * i)
            attn = _diff_attention(q_t, k, v_t, lam_params, diff_g_sub[j], lambda_init, bsz, seq)
            w_o = diff_w_o[j]
        w_route, b_route = _router_layout(moe_w_group[i], moe_b_group[i], moe_w_router[i], moe_b_router[i])
        xt, hx = _oproj(attn, w_o.astype(BF16), xt, gt_m, norm_ffn_g[i], sc_f, sh_f, w_route, b_route, seq)
        xt = _moe(hx, moe_w_gate_up, moe_w_down, i, xt, gt_f, final_g, seq, final=i == depth - 1)
    return xt.reshape(bsz, seq, d)
```

```python
import functools
import math

import jax
import jax.numpy as jnp
from jax import lax
from jax.experimental import pallas as pl
from jax.experimental.pallas import tpu as pltpu

F32 = jnp.float32
BF16 = jnp.bfloat16

N_MIXERS = 3
ROPE_THETA = 10000.0
EPS = 1e-6
HEAD_DIM = 64
SB_HEADS = 16
MLA_HEADS = 16
MLA_Q_RANK = 384
MLA_KV_RANK = 256
MLA_NOPE = 64
MLA_ROPE = 32
MLA_V = 64
DIFF_HEADS = 8
N_GROUPS = 4
EXPERTS_PER_GROUP = 4
N_EXPERTS = N_GROUPS * EXPERTS_PER_GROUP
EXPERT_FF = 512

LANES = 128
LOG2E = 1.4426950408889634
NEG_BIG = -1e30
VMEM_LIMIT = 56 * 1024 * 1024

ROW_TILE = 512
ATT_TILE = 256
MOE_TILE = 256
PROJ_CHUNK = 256


def _cparams(n_axes):
    return pltpu.CompilerParams(dimension_semantics=("arbitrary",) * n_axes, vmem_limit_bytes=VMEM_LIMIT)


def _dot(a, b):
    return jnp.dot(a, b, preferred_element_type=F32)


def _dot_nt(a, b):
    return lax.dot_general(a, b, (((1,), (1,)), ((), ())), preferred_element_type=F32)


def _rms(x):
    return x * lax.rsqrt(jnp.mean(x * x, axis=-1, keepdims=True) + EPS)


def _rope_blocks(y, cos, sin_signed):
    out = []
    for j in range(y.shape[1] // LANES):
        yb = y[:, j * LANES:(j + 1) * LANES]
        out.append(yb * cos + pltpu.roll(yb, 64, 1) * sin_signed)
    return out[0] if len(out) == 1 else jnp.concatenate(out, axis=1)


def _rope_rows(y, cos_t, sin_t):
    out = []
    for j in range(y.shape[0] // LANES):
        yb = y[j * LANES:(j + 1) * LANES]
        out.append(yb * cos_t + jnp.concatenate([yb[64:], yb[:64]], axis=0) * sin_t)
    return out[0] if len(out) == 1 else jnp.concatenate(out, axis=0)


def _adaln_kernel(c_ref, w_ref, b_ref, o_ref):
    c = c_ref[...]
    ca = c / (1.0 + jnp.exp(-c))
    o_ref[0] = jnp.dot(ca, w_ref[0], preferred_element_type=F32, precision=lax.Precision.HIGHEST) + b_ref[0]


def _adaln(c, ada_w, ada_b):
    depth, d, n = ada_w.shape
    bsz = c.shape[0]
    rows = 8
    cp = jnp.zeros((rows, d), F32).at[:bsz].set(c)
    tn = 1536
    out = pl.pallas_call(
        _adaln_kernel,
        grid=(depth, n // tn),
        in_specs=[
            pl.BlockSpec((rows, d), lambda i, j: (0, 0)),
            pl.BlockSpec((1, d, tn), lambda i, j: (i, 0, j)),
            pl.BlockSpec((1, 1, tn), lambda i, j: (i, 0, j)),
        ],
        out_specs=pl.BlockSpec((1, rows, tn), lambda i, j: (i, 0, j)),
        out_shape=jax.ShapeDtypeStruct((depth, rows, n), F32),
        compiler_params=_cparams(2),
        name="adaln_mod",
    )(cp, ada_w, ada_b.reshape(depth, 1, n))
    return out[:, :bsz]


def _rope_table_kernel(pos_ref, invf_ref, sign_ref, cos_ref, sin_ref, cost_ref, sint_ref):
    ang = pos_ref[...].astype(F32) * invf_ref[...]
    cos = jnp.cos(ang)
    sin = jnp.sin(ang) * sign_ref[...]
    cos_ref[...] = cos
    sin_ref[...] = sin
    cost_ref[...] = cos.T
    sint_ref[...] = sin.T


def _rope_tables(positions, dim):
    t = positions.size
    half = dim // 2
    inv_freq = ROPE_THETA ** (-jnp.arange(0, dim, 2, dtype=F32) / dim)
    invf = jnp.tile(inv_freq, LANES // half)
    sign = jnp.where(jnp.arange(LANES) < 64, -1.0, 1.0).astype(F32)
    tm = 2048
    small_r = pl.BlockSpec((1, LANES), lambda i: (0, 0))
    return pl.pallas_call(
        _rope_table_kernel,
        grid=(t // tm,),
        in_specs=[pl.BlockSpec((tm, 1), lambda i: (i, 0)), small_r, small_r],
        out_specs=[pl.BlockSpec((tm, LANES), lambda i: (i, 0))] * 2 + [pl.BlockSpec((LANES, tm), lambda i: (0, i))] * 2,
        out_shape=[jax.ShapeDtypeStruct((t, LANES), F32)] * 2 + [jax.ShapeDtypeStruct((LANES, t), F32)] * 2,
        compiler_params=_cparams(1),
        name="rope_tables",
    )(positions.reshape(t, 1), invf.reshape(1, LANES), sign.reshape(1, LANES))


def _modulated_norm(x_ref, g_ref, sc_ref, sh_ref):
    h = _rms(x_ref[...]) * g_ref[...]
    return h * (1.0 + sc_ref[0]) + sh_ref[0]


def _store_token_blocks(o_ref, w_t_ref, hb, post=None):
    tb = o_ref.shape[2]
    for r in range(0, w_t_ref.shape[0], PROJ_CHUNK):
        y = _dot_nt(w_t_ref[r:r + PROJ_CHUNK, :], hb)
        if post is not None:
            y = post(y, r)
        y = y.astype(BF16)
        for c in range(o_ref.shape[0]):
            o_ref[c, r:r + PROJ_CHUNK, :] = y[:, c * tb:(c + 1) * tb]


def _qkv_kernel(*refs, rope, q_scale):
    x_ref, g_ref, sc_ref, sh_ref, wqt_ref, wk_ref, wvt_ref = refs[:7]
    q_ref, k_ref, v_ref = refs[-3:]
    hb = _modulated_norm(x_ref, g_ref, sc_ref, sh_ref).astype(BF16)
    if rope:
        cos_ref, sin_ref, cost_ref, sint_ref = refs[7:11]
        _store_token_blocks(q_ref, wqt_ref, hb, lambda qt, r: _rope_rows(qt, cost_ref[...], sint_ref[...]) * q_scale)
    else:
        _store_token_blocks(q_ref, wqt_ref, hb, lambda qt, r: qt * q_scale)
    for r in range(0, wk_ref.shape[1], 512):
        k = _dot(hb, wk_ref[:, r:r + 512])
        if rope:
            k = _rope_blocks(k, cos_ref[...], sin_ref[...])
        k_ref[:, r:r + 512] = k.astype(BF16)
    _store_token_blocks(v_ref, wvt_ref, hb)


def _qkv_proj(x, g, sc, sh, wq_t, wk, wv_t, seq, rope=None):
    t, d = x.shape
    tm, tk = ROW_TILE, ATT_TILE
    per_batch = seq // tm
    row = pl.BlockSpec((tm, d), lambda i: (i, 0))
    mod = pl.BlockSpec((1, 1, d), lambda i: (i // per_batch, 0, 0))

    def full(a):
        return pl.BlockSpec(a.shape, lambda i: (0,) * a.ndim)

    g2 = g.reshape(1, d)
    in_specs = [row, full(g2), mod, mod, full(wq_t), full(wk), full(wv_t)]
    if rope is not None:
        tab = pl.BlockSpec((tm, LANES), lambda i: (i, 0))
        tab_t = pl.BlockSpec((LANES, tm), lambda i: (0, i))
        in_specs += [tab, tab, tab_t, tab_t]
    nq, nk, nv = wq_t.shape[0], wk.shape[1], wv_t.shape[0]
    return pl.pallas_call(
        functools.partial(_qkv_kernel, rope=rope is not None, q_scale=HEAD_DIM ** -0.5),
        grid=(t // tm,),
        in_specs=in_specs,
        out_specs=[pl.BlockSpec((tm // tk, nq, tk), lambda i: (i, 0, 0)), pl.BlockSpec((tm, nk), lambda i: (i, 0)),
                   pl.BlockSpec((tm // tk, nv, tk), lambda i: (i, 0, 0))],
        out_shape=[jax.ShapeDtypeStruct((t // tk, nq, tk), BF16), jax.ShapeDtypeStruct((t, nk), BF16),
                   jax.ShapeDtypeStruct((t // tk, nv, tk), BF16)],
        compiler_params=_cparams(1),
        name="qkv_proj",
    )(x, g2, sc, sh, wq_t, wk, wv_t, *(rope or ()))


def _mla_proj_kernel(x_ref, g_ref, sc_ref, sh_ref, win_ref, gq_ref, gkv_ref, wqt_ref, wkn_ref, wvt_ref,
                     cos_ref, sin_ref, cost_ref, sint_ref, q_ref, kn_ref, kr_ref, v_ref):
    hb = _modulated_norm(x_ref, g_ref, sc_ref, sh_ref).astype(BF16)
    lat = _dot(hb, win_ref[...])
    cq = (_rms(lat[:, :MLA_Q_RANK]) * gq_ref[...]).astype(BF16)
    ckv = (_rms(lat[:, MLA_Q_RANK:MLA_Q_RANK + MLA_KV_RANK]) * gkv_ref[...]).astype(BF16)
    kr_ref[...] = _rope_blocks(lat[:, MLA_Q_RANK + MLA_KV_RANK:], cos_ref[...], sin_ref[...]).astype(BF16)
    n_nope = MLA_HEADS * MLA_NOPE
    _store_token_blocks(q_ref, wqt_ref, cq,
                        lambda qt, r: _rope_rows(qt, cost_ref[...], sint_ref[...]) if r >= n_nope else qt)
    kn_ref[...] = _dot(ckv, wkn_ref[...]).astype(BF16)
    _store_token_blocks(v_ref, wvt_ref, ckv)


def _mla_proj(x, g, sc, sh, w_in, g_q, g_kv, wq_t, wkn, wv_t, rope, seq):
    t, d = x.shape
    tm, tk = ROW_TILE, ATT_TILE
    per_batch = seq // tm
    row = pl.BlockSpec((tm, d), lambda i: (i, 0))
    mod = pl.BlockSpec((1, 1, d), lambda i: (i // per_batch, 0, 0))

    def full(a):
        return pl.BlockSpec(a.shape, lambda i: (0,) * a.ndim)

    tab = pl.BlockSpec((tm, LANES), lambda i: (i, 0))
    tab_t = pl.BlockSpec((LANES, tm), lambda i: (0, i))
    g2, gq2, gkv2 = g.reshape(1, d), g_q.reshape(1, -1), g_kv.reshape(1, -1)
    nq, nk, nv = wq_t.shape[0], wkn.shape[1], wv_t.shape[0]
    return pl.pallas_call(
        _mla_proj_kernel,
        grid=(t // tm,),
        in_specs=[row, full(g2), mod, mod, full(w_in), full(gq2), full(gkv2), full(wq_t), full(wkn), full(wv_t),
                  tab, tab, tab_t, tab_t],
        out_specs=[pl.BlockSpec((tm // tk, nq, tk), lambda i: (i, 0, 0)), pl.BlockSpec((tm, nk), lambda i: (i, 0)),
                   pl.BlockSpec((tm, LANES), lambda i: (i, 0)), pl.BlockSpec((tm // tk, nv, tk), lambda i: (i, 0, 0))],
        out_shape=[jax.ShapeDtypeStruct((t // tk, nq, tk), BF16), jax.ShapeDtypeStruct((t, nk), BF16),
                   jax.ShapeDtypeStruct((t, LANES), BF16), jax.ShapeDtypeStruct((t // tk, nv, tk), BF16)],
        compiler_params=_cparams(1),
        name="mla_proj",
    )(x, g2, sc, sh, w_in, gq2, gkv2, wq_t, wkn, wv_t, *rope)


SOFTMAX_ITEMS_PER_ITER = 4
SB_ITEMS_PER_ITER = 1
SOFTMAX_ROWS = 32


def _attention_items(nq, depth, per_iter):
    rows = [(qi, qi - s, int(s == 0), int(s == qi), qi & 3, qi & 3) for qi in range(nq) for s in range(qi + 1)]
    rows += [(0, 0, 1, 0, nq & 3, nq & 3)] * (-len(rows) % per_iter + (depth - 1) * per_iter)
    return jnp.asarray(list(zip(*rows)), jnp.int32)


def _diag_bias(t, strict):
    key = jnp.arange(t)[:, None]
    query = jnp.arange(t)[None, :]
    masked = key >= query if strict else key > query
    return jnp.stack([jnp.zeros((t, t), F32), jnp.where(masked, NEG_BIG, 0.0).astype(F32)])


def _run_pipeline(stages, n_items, finalize, per_iter):
    depth = len(stages)

    def iteration(it, static):
        for k in reversed(range(depth)):
            if static and it < k:
                continue
            for u in range(per_iter):
                stages[k]((it - k) * per_iter + u)

    for it in range(depth - 1):
        iteration(it, True)

    def body(it, c):
        iteration(it, False)
        for u in range(per_iter):
            finalize((it - (depth - 1)) * per_iter + u)
        return c

    lax.fori_loop(depth - 1, n_items // per_iter, body, 0)


def _pipe_call(kernel_fn, name, depth, bsz, seq, pairs, in_specs, args, scratch, per_iter):
    t = ATT_TILE
    items = _attention_items(seq // t, depth, per_iter)
    grid_spec = pltpu.PrefetchScalarGridSpec(
        num_scalar_prefetch=1,
        grid=(bsz, pairs),
        in_specs=in_specs,
        out_specs=pl.BlockSpec((seq, LANES), lambda b, p, it: (b, p)),
        scratch_shapes=scratch,
    )
    return pl.pallas_call(
        kernel_fn,
        grid_spec=grid_spec,
        out_shape=jax.ShapeDtypeStruct((bsz * seq, pairs * LANES), BF16),
        compiler_params=_cparams(2),
        name=name,
    )(items, *args)


def _blocked_spec(seq, t, offset=0):
    return pl.BlockSpec((seq // t, LANES, t), lambda b, p, it: (b, offset + p, 0))


def _const_spec(a):
    return pl.BlockSpec(a.shape, lambda b, p, it: (0,) * a.ndim)


def _keep_rows(x, keep):
    n_rows, n = x.shape
    parts, pos = [], 0
    for a, b in keep:
        if a > pos:
            parts.append(jnp.zeros((a - pos, n), x.dtype))
        parts.append(x[a:b])
        pos = b
    if pos < n_rows:
        parts.append(jnp.zeros((n_rows - pos, n), x.dtype))
    return jnp.concatenate(parts, axis=0)


def _store_tile(o_ref, qi, o_t):
    t = o_t.shape[1]
    o_ref[pl.ds(pl.multiple_of(qi * t, t), t), :] = o_t.T.astype(o_ref.dtype)


def _softmax_pipeline(items_ref, q_heads_of, k_ref, vt_ref, bias_ref, scr, c_log2, finalize_tile):
    s_scr, p_scr, alpha_scr, lfin_scr, m_ref, l_ref, acc_ref = scr
    t = s_scr.shape[-1]
    n_slots = s_scr.shape[0]
    m_ref[...] = jnp.zeros_like(m_ref)
    l_ref[...] = jnp.zeros_like(l_ref)
    acc_ref[...] = jnp.zeros_like(acc_ref)

    def scores(w):
        qi, j, first = items_ref[0, w], items_ref[1, w], items_ref[2, w]
        kb = k_ref[pl.ds(pl.multiple_of(j * t, t), t), :]
        bias = bias_ref[first]
        for hd, qt in enumerate(q_heads_of(qi)):
            s_scr[w & (n_slots - 1), hd] = _dot(kb, qt) * c_log2 + bias

    def softmax(w):
        slot = w & (n_slots - 1)
        first = items_ref[2, w] == 1
        for hd in range(2):
            mx = s_scr[slot, hd, 0:SOFTMAX_ROWS, :]
            for r in range(SOFTMAX_ROWS, t, SOFTMAX_ROWS):
                mx = jnp.maximum(mx, s_scr[slot, hd, r:r + SOFTMAX_ROWS, :])
            m_old = jnp.where(first, NEG_BIG, m_ref[hd])
            m_new = jnp.maximum(m_old, jnp.max(mx, axis=0, keepdims=True))
            alpha = jnp.where(first, 0.0, jnp.exp2(m_old - m_new))
            psum = None
            for r in range(0, t, SOFTMAX_ROWS):
                p = jnp.exp2(s_scr[slot, hd, r:r + SOFTMAX_ROWS, :] - m_new)
                p_scr[slot, hd, r:r + SOFTMAX_ROWS, :] = p.astype(BF16)
                psum = p if psum is None else psum + p
            l_new = alpha * l_ref[hd] + jnp.sum(psum, axis=0, keepdims=True)
            l_ref[hd] = l_new
            m_ref[hd] = m_new
            alpha_scr[slot, hd] = alpha
            lfin_scr[items_ref[5, w], hd] = l_new

    def values(w):
        slot = w & (n_slots - 1)
        vb = vt_ref[items_ref[1, w]]
        par = items_ref[4, w]
        for hd in range(2):
            acc_ref[par, hd] = alpha_scr[slot, hd] * acc_ref[par, hd] + _dot(vb, p_scr[slot, hd])

    def finalize(w):
        @pl.when(items_ref[3, w] == 1)
        def _():
            finalize_tile(items_ref[0, w], acc_ref.at[items_ref[4, w]], lfin_scr[items_ref[5, w]])

    _run_pipeline([scores, softmax, values], items_ref.shape[1], finalize, n_slots // 2)


def _mla_pipe_kernel(items_ref, qn_ref, qr_ref, kn_ref, kr_ref, vt_ref, bias_ref, o_ref, kcat_ref, *scr, scale):
    kcat_ref[:, :LANES] = kn_ref[...]
    kcat_ref[:, LANES:] = kr_ref[...]
    half = MLA_ROPE // 2

    def q_heads_of(qi):
        qn = qn_ref[qi]
        qr = qr_ref[qi]
        return (
            jnp.concatenate([_keep_rows(qn, [(0, MLA_NOPE)]), _keep_rows(qr, [(0, half), (64, 64 + half)])], axis=0),
            jnp.concatenate([_keep_rows(qn, [(MLA_NOPE, 2 * MLA_NOPE)]),
                             _keep_rows(qr, [(half, 2 * half), (64 + half, 64 + 2 * half)])], axis=0),
        )

    def finalize_tile(qi, acc_ref, l_fin):
        o_t = jnp.concatenate([acc_ref[0, :MLA_V] / l_fin[0], acc_ref[1, MLA_V:] / l_fin[1]], axis=0)
        _store_tile(o_ref, qi, o_t)

    _softmax_pipeline(items_ref, q_heads_of, kcat_ref, vt_ref, bias_ref, scr, scale * LOG2E, finalize_tile)


def _diff_pipe_kernel(items_ref, q_ref, k_ref, vt_ref, bias_ref, lam_ref, gsub_ref, o_ref, *scr, lambda_init):
    def q_heads_of(qi):
        qt = q_ref[qi]
        return _keep_rows(qt, [(0, 32), (64, 96)]), _keep_rows(qt, [(32, 64), (96, 128)])

    def finalize_tile(qi, acc_ref, l_fin):
        lp = lam_ref[...]
        lam = (jnp.exp(jnp.sum(lp[0:1] * lp[1:2], axis=1, keepdims=True))
               - jnp.exp(jnp.sum(lp[2:3] * lp[3:4], axis=1, keepdims=True)) + lambda_init)
        o_t = acc_ref[0] / l_fin[0] - lam * (acc_ref[1] / l_fin[1])
        o_t = o_t * lax.rsqrt(jnp.mean(o_t * o_t, axis=0, keepdims=True) + EPS)
        t = o_t.shape[1]
        o = o_t.T * gsub_ref[...] * (1.0 - lambda_init)
        o_ref[pl.ds(pl.multiple_of(qi * t, t), t), :] = o.astype(o_ref.dtype)

    _softmax_pipeline(items_ref, q_heads_of, k_ref, vt_ref, bias_ref, scr, LOG2E, finalize_tile)


def _softmax_scratch(t):
    n = 2 * SOFTMAX_ITEMS_PER_ITER
    return [pltpu.VMEM((n, 2, t, t), F32), pltpu.VMEM((n, 2, t, t), BF16), pltpu.VMEM((n, 2, 1, t), F32),
            pltpu.VMEM((4, 2, 1, t), F32), pltpu.VMEM((2, 1, t), F32), pltpu.VMEM((2, 1, t), F32),
            pltpu.VMEM((4, 2, LANES, t), F32)]


def _mla_attention(q_t, kn, kr, v_t, bsz, seq):
    t = ATT_TILE
    pairs = MLA_HEADS // 2
    bias = _diag_bias(t, False)
    in_specs = [_blocked_spec(seq, t), _blocked_spec(seq, t, pairs),
                pl.BlockSpec((seq, LANES), lambda b, p, it: (b, p)), pl.BlockSpec((seq, LANES), lambda b, p, it: (b, 0)),
                _blocked_spec(seq, t), _const_spec(bias)]
    scratch = [pltpu.VMEM((seq, 2 * LANES), BF16)] + _softmax_scratch(t)
    kernel_fn = functools.partial(_mla_pipe_kernel, scale=(MLA_NOPE + MLA_ROPE) ** -0.5)
    return _pipe_call(kernel_fn, "mla_attention", 3, bsz, seq, pairs, in_specs, (q_t, q_t, kn, kr, v_t, bias),
                      scratch, SOFTMAX_ITEMS_PER_ITER)


def _diff_attention(q_t, k, v_t, lam_params, g_sub, lambda_init, bsz, seq):
    t = ATT_TILE
    bias = _diag_bias(t, False)
    g2 = g_sub.reshape(1, LANES)
    in_specs = [_blocked_spec(seq, t), pl.BlockSpec((seq, LANES), lambda b, p, it: (b, p)), _blocked_spec(seq, t),
                _const_spec(bias), _const_spec(lam_params), _const_spec(g2)]
    kernel_fn = functools.partial(_diff_pipe_kernel, lambda_init=lambda_init)
    return _pipe_call(kernel_fn, "diff_attention", 3, bsz, seq, DIFF_HEADS, in_specs,
                      (q_t, k, v_t, bias, lam_params, g2), _softmax_scratch(t), SOFTMAX_ITEMS_PER_ITER)


def _sb_pipe_kernel(items_ref, q_ref, k_ref, vt_ref, lmat_ref, bias_ref, o_ref,
                    z_scr, sp_scr, cs_scr, w_scr, acc_ref, carry_ref):
    t = z_scr.shape[-1]
    n_slots = sp_scr.shape[0]
    lmat = lmat_ref[...]
    acc_ref[...] = jnp.zeros_like(acc_ref)
    carry_ref[...] = jnp.zeros_like(carry_ref)

    def scores(w):
        qi, j, first = items_ref[0, w], items_ref[1, w], items_ref[2, w]
        kb = k_ref[pl.ds(pl.multiple_of(j * t, t), t), :]
        qt = q_ref[qi]
        bias = bias_ref[first]
        for hd, rows in enumerate(([(0, HEAD_DIM)], [(HEAD_DIM, 2 * HEAD_DIM)])):
            z_scr[w & (2 * n_slots - 1), hd] = _dot(kb, _keep_rows(qt, rows)) + bias

    def softplus(w):
        for hd in range(2):
            z = z_scr[w & (2 * n_slots - 1), hd]
            sp = jnp.maximum(z, 0.0) + jnp.log(1.0 + jnp.exp2(jnp.abs(z) * -LOG2E))
            sp_scr[w & (n_slots - 1), hd] = sp.astype(BF16)

    def sums(w):
        for hd in range(2):
            cs_scr[w & (n_slots - 1), hd] = _dot(lmat, sp_scr[w & (n_slots - 1), hd])

    def weights(w):
        first = items_ref[2, w] == 1
        for hd in range(2):
            cs = cs_scr[w & (n_slots - 1), hd]
            carry = jnp.where(first, 0.0, carry_ref[hd])
            w_scr[w & (n_slots - 1), hd] = jnp.exp(z_scr[w & (2 * n_slots - 1), hd] - cs - carry).astype(BF16)
            carry_ref[hd] = carry + cs[0:1, :]

    def values(w):
        vb = vt_ref[items_ref[1, w]]
        keep = jnp.where(items_ref[2, w] == 1, 0.0, 1.0)
        par = items_ref[4, w]
        for hd in range(2):
            acc_ref[par, hd] = keep * acc_ref[par, hd] + _dot(vb, w_scr[w & (n_slots - 1), hd])

    def finalize(w):
        @pl.when(items_ref[3, w] == 1)
        def _():
            par = items_ref[4, w]
            o_t = jnp.concatenate([acc_ref[par, 0, :HEAD_DIM], acc_ref[par, 1, HEAD_DIM:]], axis=0)
            _store_tile(o_ref, items_ref[0, w], o_t)

    _run_pipeline([scores, softplus, sums, weights, values], items_ref.shape[1], finalize, n_slots // 2)


def _sb_attention(q_t, k, v_t, bsz, seq):
    t = ATT_TILE
    lmat = (jnp.arange(t)[None, :] >= jnp.arange(t)[:, None]).astype(BF16)
    bias = _diag_bias(t, True)
    in_specs = [_blocked_spec(seq, t), pl.BlockSpec((seq, LANES), lambda b, p, it: (b, p)), _blocked_spec(seq, t),
                _const_spec(lmat), _const_spec(bias)]
    n = 2 * SB_ITEMS_PER_ITER
    scratch = [pltpu.VMEM((2 * n, 2, t, t), F32), pltpu.VMEM((n, 2, t, t), BF16), pltpu.VMEM((n, 2, t, t), F32),
               pltpu.VMEM((n, 2, t, t), BF16), pltpu.VMEM((4, 2, LANES, t), F32), pltpu.VMEM((2, 1, t), F32)]
    return _pipe_call(_sb_pipe_kernel, "sb_attention", 5, bsz, seq, SB_HEADS // 2, in_specs,
                      (q_t, k, v_t, lmat, bias), scratch, SB_ITEMS_PER_ITER)


PAIRS_PER_GROUP = EXPERTS_PER_GROUP * (EXPERTS_PER_GROUP - 1) // 2
N_CLASSES = N_GROUPS * PAIRS_PER_GROUP


def _class_experts():
    lo, hi = [], []
    for g in range(N_GROUPS):
        for a in range(EXPERTS_PER_GROUP):
            for b in range(a + 1, EXPERTS_PER_GROUP):
                lo.append(g * EXPERTS_PER_GROUP + a)
                hi.append(g * EXPERTS_PER_GROUP + b)
    return lo, hi


def _route(logits):
    lane = lax.broadcasted_iota(jnp.int32, logits.shape, 1).astype(F32)
    big = jnp.float32(1 << 20)

    def top1(vals):
        v = jnp.max(vals, axis=1, keepdims=True)
        i = jnp.min(jnp.where(vals == v, lane, big), axis=1, keepdims=True)
        return v, i

    is_group = (lane >= N_EXPERTS) & (lane < N_EXPERTS + N_GROUPS)
    gl = jnp.where(is_group, logits, -jnp.inf)
    gmax, gidx = top1(gl)
    g_w = 1.0 / jnp.sum(jnp.exp(gl - gmax), axis=1, keepdims=True)
    first = (gidx - N_EXPERTS) * EXPERTS_PER_GROUP
    el = jnp.where((lane >= first) & (lane < first + EXPERTS_PER_GROUP), logits, -jnp.inf)
    v1, i1 = top1(el)
    v2, i2 = top1(jnp.where(lane == i1, -jnp.inf, el))
    e2 = jnp.exp(v2 - v1)
    w1 = g_w / (1.0 + e2)
    w2 = g_w * e2 / (1.0 + e2)
    group = gidx - N_EXPERTS
    a = jnp.minimum(i1, i2) - first
    b = jnp.maximum(i1, i2) - first
    pair = a * (2 * EXPERTS_PER_GROUP - 1 - a) * 0.5 + (b - a - 1.0)
    cls = group * PAIRS_PER_GROUP + pair
    w_lo = jnp.where(i1 < i2, w1, w2)
    w_hi = jnp.where(i1 < i2, w2, w1)
    return jnp.where(lane == 0, cls, jnp.where(lane == 1, w_lo, jnp.where(lane == 2, w_hi, 0.0)))


def _oproj_kernel(a_ref, wo_ref, x_ref, gt_ref, g_ref, sc_ref, sh_ref, wr_ref, br_ref, xo_ref, hx_ref):
    d = x_ref.shape[1]
    xn = x_ref[...] + gt_ref[0] * _dot(a_ref[...], wo_ref[...])
    xo_ref[...] = xn
    h = _rms(xn) * g_ref[...]
    h = h * (1.0 + sc_ref[0]) + sh_ref[0]
    hx_ref[:, :d] = h
    h_hi = h.astype(BF16)
    h_lo = (h - h_hi.astype(F32)).astype(BF16)
    both = _dot(h_hi, wr_ref[...])
    logits = both[:, :LANES] + both[:, LANES:] + _dot(h_lo, wr_ref[:, :LANES]) + br_ref[...]
    hx_ref[:, d:] = _route(logits)


def _oproj(attn, w_o, x, gt, g, sc, sh, w_route, b_route, seq):
    t, d = x.shape
    tm = ROW_TILE
    per_batch = seq // tm
    row = pl.BlockSpec((tm, d), lambda i: (i, 0))
    mod = pl.BlockSpec((1, 1, d), lambda i: (i // per_batch, 0, 0))

    def full(a):
        return pl.BlockSpec(a.shape, lambda i: (0,) * a.ndim)

    g2 = g.reshape(1, d)
    return pl.pallas_call(
        _oproj_kernel,
        grid=(t // tm,),
        in_specs=[pl.BlockSpec((tm, attn.shape[1]), lambda i: (i, 0)), full(w_o), row, mod, full(g2), mod, mod,
                  full(w_route), full(b_route)],
        out_specs=[row, pl.BlockSpec((tm, d + LANES), lambda i: (i, 0))],
        out_shape=[jax.ShapeDtypeStruct((t, d), F32), jax.ShapeDtypeStruct((t, d + LANES), F32)],
        compiler_params=_cparams(1),
        name="oproj_router",
    )(attn, w_o, x, gt, g2, sc, sh, w_route, b_route)


ROWS_PER_STEP = 1024


def _sort_kernel(route_ref, ltri_ref, utri_ref, pos_ref, counts_ref, cnt_scr, off_scr, *, tile):
    phase, i = pl.program_id(0), pl.program_id(1)
    lane = lax.broadcasted_iota(jnp.int32, route_ref.shape, 1).astype(F32)
    onehot = jnp.where(lane == route_ref[:, 0:1], 1.0, 0.0)

    @pl.when((phase == 0) & (i == 0))
    def _():
        cnt_scr[...] = jnp.zeros_like(cnt_scr)

    @pl.when((phase == 1) & (i == 0))
    def _():
        cnt = cnt_scr[...]
        counts_ref[...] = cnt
        n_tiles = jnp.floor((cnt + (tile - 1.0)) * (1.0 / tile))
        before = _dot(jnp.broadcast_to(n_tiles, (8, LANES)).astype(BF16), utri_ref[...])
        off_scr[...] = before[0:1] * tile
        cnt_scr[...] = jnp.zeros_like(cnt_scr)

    @pl.when(phase == 1)
    def _():
        earlier = _dot(ltri_ref[...], onehot.astype(BF16))
        row = jnp.sum(onehot * (earlier + off_scr[...] + cnt_scr[...]), axis=1, keepdims=True)
        pos_ref[...] = row.astype(jnp.int32)

    cnt_scr[...] += jnp.sum(onehot, axis=0, keepdims=True)


def _sorted_rows(hx, tile):
    t, w = hx.shape
    tm = 2 * ROW_TILE
    ltri = (jnp.arange(tm)[:, None] > jnp.arange(tm)[None, :]).astype(BF16)
    utri = (jnp.arange(LANES)[:, None] < jnp.arange(LANES)[None, :]).astype(BF16)
    pos, counts = pl.pallas_call(
        functools.partial(_sort_kernel, tile=tile),
        grid=(2, t // tm),
        in_specs=[pl.BlockSpec((tm, LANES), lambda ph, i: (i, w // LANES - 1)),
                  pl.BlockSpec((tm, tm), lambda ph, i: (0, 0)), pl.BlockSpec((LANES, LANES), lambda ph, i: (0, 0))],
        out_specs=[pl.BlockSpec((tm, 1), lambda ph, i: (i * ph, 0)), pl.BlockSpec((1, LANES), lambda ph, i: (0, 0))],
        out_shape=[jax.ShapeDtypeStruct((t, 1), jnp.int32), jax.ShapeDtypeStruct((1, LANES), F32)],
        scratch_shapes=[pltpu.VMEM((1, LANES), F32), pltpu.VMEM((1, LANES), F32)],
        compiler_params=_cparams(2),
        name="moe_sort",
    )(hx, ltri, utri)
    return pos.reshape(t), counts.reshape(LANES)


def _row_dma_params():
    return pltpu.CompilerParams(dimension_semantics=("arbitrary",), vmem_limit_bytes=VMEM_LIMIT,
                                disable_bounds_checks=True)


def _dispatch_kernel(pos_ref, hx_ref, zeros_hbm, xs_hbm, sem):
    del zeros_hbm
    rows = hx_ref.shape[0]
    base = pl.program_id(0) * rows

    def issue(r, c):
        pltpu.make_async_copy(hx_ref.at[pl.ds(r, 1)], xs_hbm.at[pl.ds(pos_ref[base + r], 1)], sem).start()
        return c

    lax.fori_loop(0, rows, issue, 0, unroll=8)
    pltpu.make_async_copy(hx_ref, xs_hbm.at[pl.ds(0, rows)], sem).wait()


def _dispatch(pos, hx, n_rows):
    t, w = hx.shape
    any_spec = pl.BlockSpec(memory_space=pl.ANY)
    grid_spec = pltpu.PrefetchScalarGridSpec(
        num_scalar_prefetch=1, grid=(t // ROWS_PER_STEP,),
        in_specs=[pl.BlockSpec((ROWS_PER_STEP, w), lambda i, pos: (i, 0)), any_spec], out_specs=any_spec,
        scratch_shapes=[pltpu.SemaphoreType.DMA])
    return pl.pallas_call(
        _dispatch_kernel,
        grid_spec=grid_spec,
        out_shape=jax.ShapeDtypeStruct((n_rows, w), F32),
        input_output_aliases={2: 0},
        compiler_params=_row_dma_params(),
        name="moe_dispatch",
    )(pos, hx, jnp.zeros((n_rows, w), F32))


def _experts_kernel(lo_ref, hi_ref, used_ref, xs_ref, wgu_lo_ref, wgu_hi_ref, wd_lo_ref, wd_hi_ref, ys_ref):
    del lo_ref, hi_ref
    d = ys_ref.shape[1]

    @pl.when(used_ref[pl.program_id(0)] == 0)
    def _():
        ys_ref[...] = jnp.zeros_like(ys_ref)

    @pl.when(used_ref[pl.program_id(0)] == 1)
    def _():
        xb = xs_ref[:, :d].astype(BF16)
        route = xs_ref[:, d:]
        route_lane = lax.broadcasted_iota(jnp.int32, route.shape, 1)
        y = None
        for lane, wgu_ref, wd_ref in ((1, wgu_lo_ref, wd_lo_ref), (2, wgu_hi_ref, wd_hi_ref)):
            gu = _dot(xb, wgu_ref[0, 0].astype(BF16))
            gate, up = gu[:, :EXPERT_FF], gu[:, EXPERT_FF:]
            act = (gate / (1.0 + jnp.exp(-gate)) * up).astype(BF16)
            gate_w = jnp.sum(jnp.where(route_lane == lane, route, 0.0), axis=1, keepdims=True)
            term = gate_w * _dot(act, wd_ref[0, 0].astype(BF16))
            y = term if y is None else y + term
        ys_ref[...] = y


def _experts(xs, tile_lo, tile_hi, tile_used, w_gu, w_d, layer, tile):
    n_rows, w = xs.shape
    d = w - LANES
    gu_block, d_block = (1, 1) + w_gu.shape[2:], (1, 1) + w_d.shape[2:]
    grid_spec = pltpu.PrefetchScalarGridSpec(
        num_scalar_prefetch=3,
        grid=(n_rows // tile,),
        in_specs=[pl.BlockSpec((tile, w), lambda i, lo, hi, used: (i, 0)),
                  pl.BlockSpec(gu_block, lambda i, lo, hi, used: (layer, lo[i], 0, 0)),
                  pl.BlockSpec(gu_block, lambda i, lo, hi, used: (layer, hi[i], 0, 0)),
                  pl.BlockSpec(d_block, lambda i, lo, hi, used: (layer, lo[i], 0, 0)),
                  pl.BlockSpec(d_block, lambda i, lo, hi, used: (layer, hi[i], 0, 0))],
        out_specs=pl.BlockSpec((tile, d), lambda i, lo, hi, used: (i, 0)),
    )
    return pl.pallas_call(
        _experts_kernel,
        grid_spec=grid_spec,
        out_shape=jax.ShapeDtypeStruct((n_rows, d), F32),
        compiler_params=_cparams(1),
        name="moe_experts",
    )(tile_lo, tile_hi, tile_used, xs, w_gu, w_gu, w_d, w_d)


def _combine_kernel(pos_ref, ys_hbm, x_ref, gt_ref, g_ref, o_ref, buf, sem, *, final):
    tm = x_ref.shape[0]
    base = pl.program_id(0) * tm

    def issue(r, c):
        pltpu.make_async_copy(ys_hbm.at[pl.ds(pos_ref[base + r], 1)], buf.at[pl.ds(r, 1)], sem).start()
        return c

    lax.fori_loop(0, tm, issue, 0, unroll=8)
    pltpu.make_async_copy(ys_hbm.at[pl.ds(0, tm)], buf, sem).wait()
    y = x_ref[...] + gt_ref[0] * buf[...]
    o_ref[...] = _rms(y) * g_ref[...] if final else y


def _combine(pos, ys, x, gt, g_final, seq, final):
    t, d = x.shape
    tm = ROW_TILE
    per_batch = seq // tm
    grid_spec = pltpu.PrefetchScalarGridSpec(
        num_scalar_prefetch=1,
        grid=(t // tm,),
        in_specs=[pl.BlockSpec(memory_space=pl.ANY), pl.BlockSpec((tm, d), lambda i, pos: (i, 0)),
                  pl.BlockSpec((1, 1, d), lambda i, pos: (i // per_batch, 0, 0)),
                  pl.BlockSpec((1, d), lambda i, pos: (0, 0))],
        out_specs=pl.BlockSpec((tm, d), lambda i, pos: (i, 0)),
        scratch_shapes=[pltpu.VMEM((tm, d), F32), pltpu.SemaphoreType.DMA],
    )
    return pl.pallas_call(
        functools.partial(_combine_kernel, final=final),
        grid_spec=grid_spec,
        out_shape=jax.ShapeDtypeStruct((t, d), F32),
        compiler_params=_row_dma_params(),
        name="moe_combine",
    )(pos, ys, x, gt, g_final.reshape(1, d))


def _moe(hx, w_gu, w_d, layer, x, gt, g_final, seq, final):
    t = x.shape[0]
    tile = MOE_TILE
    n_tiles = t // tile + N_CLASSES
    pos, counts = _sorted_rows(hx, tile)
    tiles_per_class = jnp.ceil(counts[:N_CLASSES] / tile).astype(jnp.int32)
    ends = jnp.cumsum(tiles_per_class)
    tile_ids = jnp.arange(n_tiles, dtype=jnp.int32)
    tile_class = jnp.minimum(jnp.searchsorted(ends, tile_ids, side="right"), N_CLASSES - 1)
    tile_used = (tile_ids < ends[-1]).astype(jnp.int32)
    lo, hi = _class_experts()
    tile_lo = jnp.asarray(lo, jnp.int32)[tile_class]
    tile_hi = jnp.asarray(hi, jnp.int32)[tile_class]
    xs = _dispatch(pos, hx, n_tiles * tile)
    ys = _experts(xs, tile_lo, tile_hi, tile_used, w_gu, w_d, layer, tile)
    return _combine(pos, ys, x, gt, g_final, seq, final)


def _diff_qk_layout(w):
    d = w.shape[0]
    w = w.reshape(d, DIFF_HEADS, 2, 2, HEAD_DIM // 2)
    return w.transpose(0, 1, 3, 2, 4).reshape(d, -1)


def _mla_layouts(w_in, w_q_up, w_kv_up):
    d = w_in.shape[0]
    half = MLA_ROPE // 2
    base = MLA_Q_RANK + MLA_KV_RANK
    r1, r2 = w_in[:, base:base + half], w_in[:, base + half:]
    z = jnp.zeros((d, 64 - 2 * half), w_in.dtype)
    w_in_l = jnp.concatenate([w_in[:, :base], r1, r1, z, r2, r2, z], axis=1)

    r = w_q_up.shape[0]
    wq = w_q_up.reshape(r, MLA_HEADS, MLA_NOPE + MLA_ROPE)
    nope = wq[:, :, :MLA_NOPE].reshape(r, -1)
    q1 = wq[:, :, MLA_NOPE:MLA_NOPE + half].reshape(r, MLA_HEADS // 2, 2 * half)
    q2 = wq[:, :, MLA_NOPE + half:].reshape(r, MLA_HEADS // 2, 2 * half)
    zq = jnp.zeros((r, MLA_HEADS // 2, 64 - 2 * half), w_q_up.dtype)
    rope = jnp.concatenate([q1, zq, q2, zq], axis=2).reshape(r, -1)
    w_q_l = jnp.concatenate([nope, rope], axis=1)

    rk = w_kv_up.shape[0]
    wkv = w_kv_up.reshape(rk, MLA_HEADS, MLA_NOPE + MLA_V)
    w_kn = wkv[:, :, :MLA_NOPE].reshape(rk, -1)
    w_v = wkv[:, :, MLA_NOPE:].reshape(rk, -1)
    return w_in_l.astype(BF16), w_q_l.T.astype(BF16), w_kn.astype(BF16), w_v.T.astype(BF16)


def _router_layout(w_group, b_group, w_router, b_router):
    d = w_group.shape[0]
    pad = LANES - N_EXPERTS - N_GROUPS
    w = jnp.concatenate([w_router, w_group, jnp.zeros((d, pad), F32)], axis=1)
    b = jnp.concatenate([b_router, b_group, jnp.zeros((pad,), F32)]).reshape(1, LANES)
    w_hi = w.astype(BF16)
    w_lo = (w - w_hi.astype(F32)).astype(BF16)
    return jnp.concatenate([w_hi, w_lo], axis=1), b


def kernel(x, c, positions, norm_mix_g, norm_ffn_g, ada_w, ada_b, sb_w_qkv, sb_w_o, mla_w_in, mla_g_q, mla_g_kv, mla_w_q_up, mla_w_kv_up, mla_w_o, diff_w_qkv, diff_lam_q1, diff_lam_k1, diff_lam_q2, diff_lam_k2, diff_g_sub, diff_w_o, moe_w_group, moe_b_group, moe_w_router, moe_b_router, moe_w_gate_up, moe_w_down, final_g):
    bsz, seq, d = x.shape
    depth = ada_w.shape[0]
    xt = x.reshape(bsz * seq, d)
    mod = _adaln(c, ada_w, ada_b)
    rope_mla = _rope_tables(positions, MLA_ROPE) if depth > 1 else None
    rope_diff = _rope_tables(positions, HEAD_DIM) if depth > 2 else None

    for i in range(depth):
        sh_m, sc_m, gt_m, sh_f, sc_f, gt_f = (mod[i, :, k * d:(k + 1) * d].reshape(bsz, 1, d) for k in range(6))
        kind, j = i % N_MIXERS, i // N_MIXERS
        if kind == 0:
            w = sb_w_qkv[j].astype(BF16)
            q_t, k, v_t = _qkv_proj(xt, norm_mix_g[i], sc_m, sh_m, w[:, :d].T, w[:, d:2 * d], w[:, 2 * d:].T, seq)
            attn = _sb_attention(q_t, k, v_t, bsz, seq)
            w_o = sb_w_o[j]
        elif kind == 1:
            w_in_l, wq_t, w_kn, wv_t = _mla_layouts(mla_w_in[j], mla_w_q_up[j], mla_w_kv_up[j])
            q_t, kn, kr, v_t = _mla_proj(xt, norm_mix_g[i], sc_m, sh_m, w_in_l, mla_g_q[j], mla_g_kv[j], wq_t, w_kn,
                                         wv_t, rope_mla, seq)
            attn = _mla_attention(q_t, kn, kr, v_t, bsz, seq)
            w_o = mla_w_o[j]
        else:
            w = diff_w_qkv[j].astype(BF16)
            q_t, k, v_t = _qkv_proj(xt, norm_mix_g[i], sc_m, sh_m, _diff_qk_layout(w[:, :d]).T,
                                    _diff_qk_layout(w[:, d:2 * d]), w[:, 2 * d:].T, seq, rope_diff)
            lam_params = jnp.stack([diff_lam_q1[j], diff_lam_k1[j], diff_lam_q2[j], diff_lam_k2[j]])
            lambda_init = 0.8 - 0.6 * math.exp(-0.3 * i)
            attn = _diff_attention(q_t, k, v_t, lam_params, diff_g_sub[j], lambda_init, bsz, seq)
            w_o = diff_w_o[j]
        w_route, b_route = _router_layout(moe_w_group[i], moe_b_group[i], moe_w_router[i], moe_b_router[i])
        xt, hx = _oproj(attn, w_o.astype(BF16), xt, gt_m, norm_ffn_g[i], sc_f, sh_f, w_route, b_route, seq)
        xt = _moe(hx, moe_w_gate_up, moe_w_down, i, xt, gt_f, final_g, seq, final=i == depth - 1)
    return xt.reshape(bsz, seq, d)
```

```python
import functools
import math

import jax
import jax.numpy as jnp
from jax import lax
from jax.experimental import pallas as pl
from jax.experimental.pallas import tpu as pltpu

F32 = jnp.float32
BF16 = jnp.bfloat16

N_MIXERS = 3
ROPE_THETA = 10000.0
EPS = 1e-6
HEAD_DIM = 64
SB_HEADS = 16
MLA_HEADS = 16
MLA_Q_RANK = 384
MLA_KV_RANK = 256
MLA_NOPE = 64
MLA_ROPE = 32
MLA_V = 64
DIFF_HEADS = 8
N_GROUPS = 4
EXPERTS_PER_GROUP = 4
N_EXPERTS = N_GROUPS * EXPERTS_PER_GROUP
EXPERT_FF = 512

LANES = 128
LOG2E = 1.4426950408889634
NEG_BIG = -1e30
VMEM_LIMIT = 56 * 1024 * 1024

ROW_TILE = 512
ATT_TILE = 256
MOE_TILE = 256
PROJ_CHUNK = 256


def _cparams(n_axes):
    return pltpu.CompilerParams(dimension_semantics=("arbitrary",) * n_axes, vmem_limit_bytes=VMEM_LIMIT)


def _dot(a, b):
    return jnp.dot(a, b, preferred_element_type=F32)


def _dot_nt(a, b):
    return lax.dot_general(a, b, (((1,), (1,)), ((), ())), preferred_element_type=F32)


def _rms(x):
    return x * lax.rsqrt(jnp.mean(x * x, axis=-1, keepdims=True) + EPS)


def _rope_blocks(y, cos, sin_signed):
    out = []
    for j in range(y.shape[1] // LANES):
        yb = y[:, j * LANES:(j + 1) * LANES]
        out.append(yb * cos + pltpu.roll(yb, 64, 1) * sin_signed)
    return out[0] if len(out) == 1 else jnp.concatenate(out, axis=1)


def _rope_rows(y, cos_t, sin_t):
    out = []
    for j in range(y.shape[0] // LANES):
        yb = y[j * LANES:(j + 1) * LANES]
        out.append(yb * cos_t + jnp.concatenate([yb[64:], yb[:64]], axis=0) * sin_t)
    return out[0] if len(out) == 1 else jnp.concatenate(out, axis=0)


def _adaln_kernel(c_ref, w_ref, b_ref, o_ref):
    c = c_ref[...]
    ca = c / (1.0 + jnp.exp(-c))
    o_ref[0] = jnp.dot(ca, w_ref[0], preferred_element_type=F32, precision=lax.Precision.HIGHEST) + b_ref[0]


def _adaln(c, ada_w, ada_b):
    depth, d, n = ada_w.shape
    bsz = c.shape[0]
    rows = 8
    cp = jnp.zeros((rows, d), F32).at[:bsz].set(c)
    tn = 1536
    out = pl.pallas_call(
        _adaln_kernel,
        grid=(depth, n // tn),
        in_specs=[
            pl.BlockSpec((rows, d), lambda i, j: (0, 0)),
            pl.BlockSpec((1, d, tn), lambda i, j: (i, 0, j)),
            pl.BlockSpec((1, 1, tn), lambda i, j: (i, 0, j)),
        ],
        out_specs=pl.BlockSpec((1, rows, tn), lambda i, j: (i, 0, j)),
        out_shape=jax.ShapeDtypeStruct((depth, rows, n), F32),
        compiler_params=_cparams(2),
        name="adaln_mod",
    )(cp, ada_w, ada_b.reshape(depth, 1, n))
    return out[:, :bsz]


def _rope_table_kernel(pos_ref, invf_ref, sign_ref, cos_ref, sin_ref, cost_ref, sint_ref):
    ang = pos_ref[...].astype(F32) * invf_ref[...]
    cos = jnp.cos(ang)
    sin = jnp.sin(ang) * sign_ref[...]
    cos_ref[...] = cos
    sin_ref[...] = sin
    cost_ref[...] = cos.T
    sint_ref[...] = sin.T


def _rope_tables(positions, dim):
    t = positions.size
    half = dim // 2
    inv_freq = ROPE_THETA ** (-jnp.arange(0, dim, 2, dtype=F32) / dim)
    invf = jnp.tile(inv_freq, LANES // half)
    sign = jnp.where(jnp.arange(LANES) < 64, -1.0, 1.0).astype(F32)
    tm = 2048
    small_r = pl.BlockSpec((1, LANES), lambda i: (0, 0))
    return pl.pallas_call(
        _rope_table_kernel,
        grid=(t // tm,),
        in_specs=[pl.BlockSpec((tm, 1), lambda i: (i, 0)), small_r, small_r],
        out_specs=[pl.BlockSpec((tm, LANES), lambda i: (i, 0))] * 2 + [pl.BlockSpec((LANES, tm), lambda i: (0, i))] * 2,
        out_shape=[jax.ShapeDtypeStruct((t, LANES), F32)] * 2 + [jax.ShapeDtypeStruct((LANES, t), F32)] * 2,
        compiler_params=_cparams(1),
        name="rope_tables",
    )(positions.reshape(t, 1), invf.reshape(1, LANES), sign.reshape(1, LANES))


def _modulated_norm(x_ref, g_ref, sc_ref, sh_ref):
    h = _rms(x_ref[...]) * g_ref[...]
    return h * (1.0 + sc_ref[0]) + sh_ref[0]


def _store_token_blocks(o_ref, w_t_ref, hb, post=None):
    tb = o_ref.shape[2]
    for r in range(0, w_t_ref.shape[0], PROJ_CHUNK):
        y = _dot_nt(w_t_ref[r:r + PROJ_CHUNK, :], hb)
        if post is not None:
            y = post(y, r)
        y = y.astype(BF16)
        for c in range(o_ref.shape[0]):
            o_ref[c, r:r + PROJ_CHUNK, :] = y[:, c * tb:(c + 1) * tb]


def _qkv_kernel(*refs, rope, q_scale):
    x_ref, g_ref, sc_ref, sh_ref, wqt_ref, wk_ref, wvt_ref = refs[:7]
    q_ref, k_ref, v_ref = refs[-3:]
    hb = _modulated_norm(x_ref, g_ref, sc_ref, sh_ref).astype(BF16)
    if rope:
        cos_ref, sin_ref, cost_ref, sint_ref = refs[7:11]
        _store_token_blocks(q_ref, wqt_ref, hb, lambda qt, r: _rope_rows(qt, cost_ref[...], sint_ref[...]) * q_scale)
    else:
        _store_token_blocks(q_ref, wqt_ref, hb, lambda qt, r: qt * q_scale)
    for r in range(0, wk_ref.shape[1], 512):
        k = _dot(hb, wk_ref[:, r:r + 512])
        if rope:
            k = _rope_blocks(k, cos_ref[...], sin_ref[...])
        k_ref[:, r:r + 512] = k.astype(BF16)
    _store_token_blocks(v_ref, wvt_ref, hb)


def _qkv_proj(x, g, sc, sh, wq_t, wk, wv_t, seq, rope=None):
    t, d = x.shape
    tm, tk = ROW_TILE, ATT_TILE
    per_batch = seq // tm
    row = pl.BlockSpec((tm, d), lambda i: (i, 0))
    mod = pl.BlockSpec((1, 1, d), lambda i: (i // per_batch, 0, 0))

    def full(a):
        return pl.BlockSpec(a.shape, lambda i: (0,) * a.ndim)

    g2 = g.reshape(1, d)
    in_specs = [row, full(g2), mod, mod, full(wq_t), full(wk), full(wv_t)]
    if rope is not None:
        tab = pl.BlockSpec((tm, LANES), lambda i: (i, 0))
        tab_t = pl.BlockSpec((LANES, tm), lambda i: (0, i))
        in_specs += [tab, tab, tab_t, tab_t]
    nq, nk, nv = wq_t.shape[0], wk.shape[1], wv_t.shape[0]
    return pl.pallas_call(
        functools.partial(_qkv_kernel, rope=rope is not None, q_scale=HEAD_DIM ** -0.5),
        grid=(t // tm,),
        in_specs=in_specs,
        out_specs=[pl.BlockSpec((tm // tk, nq, tk), lambda i: (i, 0, 0)), pl.BlockSpec((tm, nk), lambda i: (i, 0)),
                   pl.BlockSpec((tm // tk, nv, tk), lambda i: (i, 0, 0))],
        out_shape=[jax.ShapeDtypeStruct((t // tk, nq, tk), BF16), jax.ShapeDtypeStruct((t, nk), BF16),
                   jax.ShapeDtypeStruct((t // tk, nv, tk), BF16)],
        compiler_params=_cparams(1),
        name="qkv_proj",
    )(x, g2, sc, sh, wq_t, wk, wv_t, *(rope or ()))


def _mla_proj_kernel(x_ref, g_ref, sc_ref, sh_ref, win_ref, gq_ref, gkv_ref, wqt_ref, wkn_ref, wvt_ref,
                     cos_ref, sin_ref, cost_ref, sint_ref, q_ref, kn_ref, kr_ref, v_ref):
    hb = _modulated_norm(x_ref, g_ref, sc_ref, sh_ref).astype(BF16)
    lat = _dot(hb, win_ref[...])
    cq = (_rms(lat[:, :MLA_Q_RANK]) * gq_ref[...]).astype(BF16)
    ckv = (_rms(lat[:, MLA_Q_RANK:MLA_Q_RANK + MLA_KV_RANK]) * gkv_ref[...]).astype(BF16)
    kr_ref[...] = _rope_blocks(lat[:, MLA_Q_RANK + MLA_KV_RANK:], cos_ref[...], sin_ref[...]).astype(BF16)
    n_nope = MLA_HEADS * MLA_NOPE
    _store_token_blocks(q_ref, wqt_ref, cq,
                        lambda qt, r: _rope_rows(qt, cost_ref[...], sint_ref[...]) if r >= n_nope else qt)
    kn_ref[...] = _dot(ckv, wkn_ref[...]).astype(BF16)
    _store_token_blocks(v_ref, wvt_ref, ckv)


def _mla_proj(x, g, sc, sh, w_in, g_q, g_kv, wq_t, wkn, wv_t, rope, seq):
    t, d = x.shape
    tm, tk = ROW_TILE, ATT_TILE
    per_batch = seq // tm
    row = pl.BlockSpec((tm, d), lambda i: (i, 0))
    mod = pl.BlockSpec((1, 1, d), lambda i: (i // per_batch, 0, 0))

    def full(a):
        return pl.BlockSpec(a.shape, lambda i: (0,) * a.ndim)

    tab = pl.BlockSpec((tm, LANES), lambda i: (i, 0))
    tab_t = pl.BlockSpec((LANES, tm), lambda i: (0, i))
    g2, gq2, gkv2 = g.reshape(1, d), g_q.reshape(1, -1), g_kv.reshape(1, -1)
    nq, nk, nv = wq_t.shape[0], wkn.shape[1], wv_t.shape[0]
    return pl.pallas_call(
        _mla_proj_kernel,
        grid=(t // tm,),
        in_specs=[row, full(g2), mod, mod, full(w_in), full(gq2), full(gkv2), full(wq_t), full(wkn), full(wv_t),
                  tab, tab, tab_t, tab_t],
        out_specs=[pl.BlockSpec((tm // tk, nq, tk), lambda i: (i, 0, 0)), pl.BlockSpec((tm, nk), lambda i: (i, 0)),
                   pl.BlockSpec((tm, LANES), lambda i: (i, 0)), pl.BlockSpec((tm // tk, nv, tk), lambda i: (i, 0, 0))],
        out_shape=[jax.ShapeDtypeStruct((t // tk, nq, tk), BF16), jax.ShapeDtypeStruct((t, nk), BF16),
                   jax.ShapeDtypeStruct((t, LANES), BF16), jax.ShapeDtypeStruct((t // tk, nv, tk), BF16)],
        compiler_params=_cparams(1),
        name="mla_proj",
    )(x, g2, sc, sh, w_in, gq2, gkv2, wq_t, wkn, wv_t, *rope)


SOFTMAX_ITEMS_PER_ITER = 2
SB_ITEMS_PER_ITER = 1
SOFTMAX_ROWS = 32


def _attention_items(nq, depth, per_iter):
    rows = [(qi, qi - s, int(s == 0), int(s == qi), qi & 1, qi & 3) for qi in range(nq) for s in range(qi + 1)]
    rows += [(0, 0, 1, 0, nq & 1, nq & 3)] * (-len(rows) % per_iter + (depth - 1) * per_iter)
    return jnp.asarray(list(zip(*rows)), jnp.int32)


def _diag_bias(t, strict):
    key = jnp.arange(t)[:, None]
    query = jnp.arange(t)[None, :]
    masked = key >= query if strict else key > query
    return jnp.stack([jnp.zeros((t, t), F32), jnp.where(masked, NEG_BIG, 0.0).astype(F32)])


def _run_pipeline(stages, n_items, finalize, per_iter):
    depth = len(stages)

    def iteration(it, static):
        for k in reversed(range(depth)):
            if static and it < k:
                continue
            for u in range(per_iter):
                stages[k]((it - k) * per_iter + u)

    for it in range(depth - 1):
        iteration(it, True)

    def body(it, c):
        iteration(it, False)
        for u in range(per_iter):
            finalize((it - (depth - 1)) * per_iter + u)
        return c

    lax.fori_loop(depth - 1, n_items // per_iter, body, 0)


def _pipe_call(kernel_fn, name, depth, bsz, seq, pairs, in_specs, args, scratch, per_iter):
    t = ATT_TILE
    items = _attention_items(seq // t, depth, per_iter)
    grid_spec = pltpu.PrefetchScalarGridSpec(
        num_scalar_prefetch=1,
        grid=(bsz, pairs),
        in_specs=in_specs,
        out_specs=pl.BlockSpec((seq, LANES), lambda b, p, it: (b, p)),
        scratch_shapes=scratch,
    )
    return pl.pallas_call(
        kernel_fn,
        grid_spec=grid_spec,
        out_shape=jax.ShapeDtypeStruct((bsz * seq, pairs * LANES), BF16),
        compiler_params=_cparams(2),
        name=name,
    )(items, *args)


def _blocked_spec(seq, t, offset=0):
    return pl.BlockSpec((seq // t, LANES, t), lambda b, p, it: (b, offset + p, 0))


def _const_spec(a):
    return pl.BlockSpec(a.shape, lambda b, p, it: (0,) * a.ndim)


def _keep_rows(x, keep):
    n_rows, n = x.shape
    parts, pos = [], 0
    for a, b in keep:
        if a > pos:
            parts.append(jnp.zeros((a - pos, n), x.dtype))
        parts.append(x[a:b])
        pos = b
    if pos < n_rows:
        parts.append(jnp.zeros((n_rows - pos, n), x.dtype))
    return jnp.concatenate(parts, axis=0)


def _store_tile(o_ref, qi, o_t):
    t = o_t.shape[1]
    o_ref[pl.ds(pl.multiple_of(qi * t, t), t), :] = o_t.T.astype(o_ref.dtype)


def _softmax_pipeline(items_ref, q_heads_of, k_ref, vt_ref, bias_ref, scr, c_log2, finalize_tile):
    s_scr, p_scr, alpha_scr, lfin_scr, m_ref, l_ref, acc_ref = scr
    t = s_scr.shape[-1]
    n_slots = s_scr.shape[0]
    m_ref[...] = jnp.zeros_like(m_ref)
    l_ref[...] = jnp.zeros_like(l_ref)
    acc_ref[...] = jnp.zeros_like(acc_ref)

    def scores(w):
        qi, j, first = items_ref[0, w], items_ref[1, w], items_ref[2, w]
        kb = k_ref[pl.ds(pl.multiple_of(j * t, t), t), :]
        bias = bias_ref[first]
        for hd, qt in enumerate(q_heads_of(qi)):
            s_scr[w & (n_slots - 1), hd] = _dot(kb, qt) * c_log2 + bias

    def softmax(w):
        slot = w & (n_slots - 1)
        first = items_ref[2, w] == 1
        for hd in range(2):
            mx = s_scr[slot, hd, 0:SOFTMAX_ROWS, :]
            for r in range(SOFTMAX_ROWS, t, SOFTMAX_ROWS):
                mx = jnp.maximum(mx, s_scr[slot, hd, r:r + SOFTMAX_ROWS, :])
            m_old = jnp.where(first, NEG_BIG, m_ref[hd])
            m_new = jnp.maximum(m_old, jnp.max(mx, axis=0, keepdims=True))
            alpha = jnp.where(first, 0.0, jnp.exp2(m_old - m_new))
            psum = None
            for r in range(0, t, SOFTMAX_ROWS):
                p = jnp.exp2(s_scr[slot, hd, r:r + SOFTMAX_ROWS, :] - m_new)
                p_scr[slot, hd, r:r + SOFTMAX_ROWS, :] = p.astype(BF16)
                psum = p if psum is None else psum + p
            l_new = alpha * l_ref[hd] + jnp.sum(psum, axis=0, keepdims=True)
            l_ref[hd] = l_new
            m_ref[hd] = m_new
            alpha_scr[slot, hd] = alpha
            lfin_scr[items_ref[5, w], hd] = l_new

    def values(w):
        slot = w & (n_slots - 1)
        vb = vt_ref[items_ref[1, w]]
        par = items_ref[4, w]
        for hd in range(2):
            acc_ref[par, hd] = alpha_scr[slot, hd] * acc_ref[par, hd] + _dot(vb, p_scr[slot, hd])

    def finalize(w):
        @pl.when(items_ref[3, w] == 1)
        def _():
            finalize_tile(items_ref[0, w], acc_ref.at[items_ref[4, w]], lfin_scr[items_ref[5, w]])

    _run_pipeline([scores, softmax, values], items_ref.shape[1], finalize, n_slots // 2)


def _mla_pipe_kernel(items_ref, qn_ref, qr_ref, kn_ref, kr_ref, vt_ref, bias_ref, o_ref, kcat_ref, *scr, scale):
    kcat_ref[:, :LANES] = kn_ref[...]
    kcat_ref[:, LANES:] = kr_ref[...]
    half = MLA_ROPE // 2

    def q_heads_of(qi):
        qn = qn_ref[qi]
        qr = qr_ref[qi]
        return (
            jnp.concatenate([_keep_rows(qn, [(0, MLA_NOPE)]), _keep_rows(qr, [(0, half), (64, 64 + half)])], axis=0),
            jnp.concatenate([_keep_rows(qn, [(MLA_NOPE, 2 * MLA_NOPE)]),
                             _keep_rows(qr, [(half, 2 * half), (64 + half, 64 + 2 * half)])], axis=0),
        )

    def finalize_tile(qi, acc_ref, l_fin):
        o_t = jnp.concatenate([acc_ref[0, :MLA_V] / l_fin[0], acc_ref[1, MLA_V:] / l_fin[1]], axis=0)
        _store_tile(o_ref, qi, o_t)

    _softmax_pipeline(items_ref, q_heads_of, kcat_ref, vt_ref, bias_ref, scr, scale * LOG2E, finalize_tile)


def _diff_pipe_kernel(items_ref, q_ref, k_ref, vt_ref, bias_ref, lam_ref, gsub_ref, o_ref, *scr, lambda_init):
    def q_heads_of(qi):
        qt = q_ref[qi]
        return _keep_rows(qt, [(0, 32), (64, 96)]), _keep_rows(qt, [(32, 64), (96, 128)])

    def finalize_tile(qi, acc_ref, l_fin):
        lp = lam_ref[...]
        lam = (jnp.exp(jnp.sum(lp[0:1] * lp[1:2], axis=1, keepdims=True))
               - jnp.exp(jnp.sum(lp[2:3] * lp[3:4], axis=1, keepdims=True)) + lambda_init)
        o_t = acc_ref[0] / l_fin[0] - lam * (acc_ref[1] / l_fin[1])
        o_t = o_t * lax.rsqrt(jnp.mean(o_t * o_t, axis=0, keepdims=True) + EPS)
        t = o_t.shape[1]
        o = o_t.T * gsub_ref[...] * (1.0 - lambda_init)
        o_ref[pl.ds(pl.multiple_of(qi * t, t), t), :] = o.astype(o_ref.dtype)

    _softmax_pipeline(items_ref, q_heads_of, k_ref, vt_ref, bias_ref, scr, LOG2E, finalize_tile)


def _softmax_scratch(t):
    n = 2 * SOFTMAX_ITEMS_PER_ITER
    return [pltpu.VMEM((n, 2, t, t), F32), pltpu.VMEM((n, 2, t, t), BF16), pltpu.VMEM((n, 2, 1, t), F32),
            pltpu.VMEM((4, 2, 1, t), F32), pltpu.VMEM((2, 1, t), F32), pltpu.VMEM((2, 1, t), F32),
            pltpu.VMEM((2, 2, LANES, t), F32)]


def _mla_attention(q_t, kn, kr, v_t, bsz, seq):
    t = ATT_TILE
    pairs = MLA_HEADS // 2
    bias = _diag_bias(t, False)
    in_specs = [_blocked_spec(seq, t), _blocked_spec(seq, t, pairs),
                pl.BlockSpec((seq, LANES), lambda b, p, it: (b, p)), pl.BlockSpec((seq, LANES), lambda b, p, it: (b, 0)),
                _blocked_spec(seq, t), _const_spec(bias)]
    scratch = [pltpu.VMEM((seq, 2 * LANES), BF16)] + _softmax_scratch(t)
    kernel_fn = functools.partial(_mla_pipe_kernel, scale=(MLA_NOPE + MLA_ROPE) ** -0.5)
    return _pipe_call(kernel_fn, "mla_attention", 3, bsz, seq, pairs, in_specs, (q_t, q_t, kn, kr, v_t, bias),
                      scratch, SOFTMAX_ITEMS_PER_ITER)


def _diff_attention(q_t, k, v_t, lam_params, g_sub, lambda_init, bsz, seq):
    t = ATT_TILE
    bias = _diag_bias(t, False)
    g2 = g_sub.reshape(1, LANES)
    in_specs = [_blocked_spec(seq, t), pl.BlockSpec((seq, LANES), lambda b, p, it: (b, p)), _blocked_spec(seq, t),
                _const_spec(bias), _const_spec(lam_params), _const_spec(g2)]
    kernel_fn = functools.partial(_diff_pipe_kernel, lambda_init=lambda_init)
    return _pipe_call(kernel_fn, "diff_attention", 3, bsz, seq, DIFF_HEADS, in_specs,
                      (q_t, k, v_t, bias, lam_params, g2), _softmax_scratch(t), SOFTMAX_ITEMS_PER_ITER)


def _sb_pipe_kernel(items_ref, q_ref, k_ref, vt_ref, lmat_ref, bias_ref, o_ref,
                    z_scr, sp_scr, cs_scr, w_scr, acc_ref, carry_ref):
    t = z_scr.shape[-1]
    n_slots = sp_scr.shape[0]
    lmat = lmat_ref[...]
    acc_ref[...] = jnp.zeros_like(acc_ref)
    carry_ref[...] = jnp.zeros_like(carry_ref)

    def scores(w):
        qi, j, first = items_ref[0, w], items_ref[1, w], items_ref[2, w]
        kb = k_ref[pl.ds(pl.multiple_of(j * t, t), t), :]
        qt = q_ref[qi]
        bias = bias_ref[first]
        for hd, rows in enumerate(([(0, HEAD_DIM)], [(HEAD_DIM, 2 * HEAD_DIM)])):
            z_scr[w & (2 * n_slots - 1), hd] = _dot(kb, _keep_rows(qt, rows)) + bias

    def softplus(w):
        for hd in range(2):
            z = z_scr[w & (2 * n_slots - 1), hd]
            sp = jnp.maximum(z, 0.0) + jnp.log(1.0 + jnp.exp2(jnp.abs(z) * -LOG2E))
            sp_scr[w & (n_slots - 1), hd] = sp.astype(BF16)
            z_scr[w & (2 * n_slots - 1), hd] = z - sp

    def sums(w):
        for hd in range(2):
            cs_scr[w & (n_slots - 1), hd] = _dot(lmat, sp_scr[w & (n_slots - 1), hd])

    def weights(w):
        first = items_ref[2, w] == 1
        for hd in range(2):
            cs = cs_scr[w & (n_slots - 1), hd]
            carry = jnp.where(first, 0.0, carry_ref[hd])
            w_scr[w & (n_slots - 1), hd] = jnp.exp(z_scr[w & (2 * n_slots - 1), hd] - cs - carry).astype(BF16)
            carry_ref[hd] = carry + cs[0:1, :] + sp_scr[w & (n_slots - 1), hd, 0:1, :].astype(F32)

    def values(w):
        vb = vt_ref[items_ref[1, w]]
        keep = jnp.where(items_ref[2, w] == 1, 0.0, 1.0)
        par = items_ref[4, w]
        for hd in range(2):
            acc_ref[par, hd] = keep * acc_ref[par, hd] + _dot(vb, w_scr[w & (n_slots - 1), hd])

    def finalize(w):
        @pl.when(items_ref[3, w] == 1)
        def _():
            par = items_ref[4, w]
            o_t = jnp.concatenate([acc_ref[par, 0, :HEAD_DIM], acc_ref[par, 1, HEAD_DIM:]], axis=0)
            _store_tile(o_ref, items_ref[0, w], o_t)

    _run_pipeline([scores, softplus, sums, weights, values], items_ref.shape[1], finalize, n_slots // 2)


def _sb_attention(q_t, k, v_t, bsz, seq):
    t = ATT_TILE
    lmat = (jnp.arange(t)[None, :] > jnp.arange(t)[:, None]).astype(BF16)
    bias = _diag_bias(t, True)
    in_specs = [_blocked_spec(seq, t), pl.BlockSpec((seq, LANES), lambda b, p, it: (b, p)), _blocked_spec(seq, t),
                _const_spec(lmat), _const_spec(bias)]
    n = 2 * SB_ITEMS_PER_ITER
    scratch = [pltpu.VMEM((2 * n, 2, t, t), F32), pltpu.VMEM((n, 2, t, t), BF16), pltpu.VMEM((n, 2, t, t), F32),
               pltpu.VMEM((n, 2, t, t), BF16), pltpu.VMEM((2, 2, LANES, t), F32), pltpu.VMEM((2, 1, t), F32)]
    return _pipe_call(_sb_pipe_kernel, "sb_attention", 5, bsz, seq, SB_HEADS // 2, in_specs,
                      (q_t, k, v_t, lmat, bias), scratch, SB_ITEMS_PER_ITER)


PAIRS_PER_GROUP = EXPERTS_PER_GROUP * (EXPERTS_PER_GROUP - 1) // 2
N_CLASSES = N_GROUPS * PAIRS_PER_GROUP


def _class_experts():
    lo, hi = [], []
    for g in range(N_GROUPS):
        for a in range(EXPERTS_PER_GROUP):
            for b in range(a + 1, EXPERTS_PER_GROUP):
                lo.append(g * EXPERTS_PER_GROUP + a)
                hi.append(g * EXPERTS_PER_GROUP + b)
    return lo, hi


def _route(logits):
    lane = lax.broadcasted_iota(jnp.int32, logits.shape, 1).astype(F32)
    big = jnp.float32(1 << 20)

    def top1(vals):
        v = jnp.max(vals, axis=1, keepdims=True)
        i = jnp.min(jnp.where(vals == v, lane, big), axis=1, keepdims=True)
        return v, i

    is_group = (lane >= N_EXPERTS) & (lane < N_EXPERTS + N_GROUPS)
    gl = jnp.where(is_group, logits, -jnp.inf)
    gmax, gidx = top1(gl)
    g_w = 1.0 / jnp.sum(jnp.exp(gl - gmax), axis=1, keepdims=True)
    first = (gidx - N_EXPERTS) * EXPERTS_PER_GROUP
    el = jnp.where((lane >= first) & (lane < first + EXPERTS_PER_GROUP), logits, -jnp.inf)
    v1, i1 = top1(el)
    v2, i2 = top1(jnp.where(lane == i1, -jnp.inf, el))
    e2 = jnp.exp(v2 - v1)
    w1 = g_w / (1.0 + e2)
    w2 = g_w * e2 / (1.0 + e2)
    group = gidx - N_EXPERTS
    a = jnp.minimum(i1, i2) - first
    b = jnp.maximum(i1, i2) - first
    pair = a * (2 * EXPERTS_PER_GROUP - 1 - a) * 0.5 + (b - a - 1.0)
    cls = group * PAIRS_PER_GROUP + pair
    w_lo = jnp.where(i1 < i2, w1, w2)
    w_hi = jnp.where(i1 < i2, w2, w1)
    return jnp.where(lane == 0, cls, jnp.where(lane == 1, w_lo, jnp.where(lane == 2, w_hi, 0.0)))


def _oproj_kernel(a_ref, wo_ref, x_ref, gt_ref, g_ref, sc_ref, sh_ref, wr_ref, br_ref, xo_ref, hx_ref):
    d = x_ref.shape[1]
    xn = x_ref[...] + gt_ref[0] * _dot(a_ref[...], wo_ref[...])
    xo_ref[...] = xn
    h = _rms(xn) * g_ref[...]
    h = h * (1.0 + sc_ref[0]) + sh_ref[0]
    hx_ref[:, :d] = h
    h_hi = h.astype(BF16)
    h_lo = (h - h_hi.astype(F32)).astype(BF16)
    both = _dot(h_hi, wr_ref[...])
    logits = both[:, :LANES] + both[:, LANES:] + _dot(h_lo, wr_ref[:, :LANES]) + br_ref[...]
    hx_ref[:, d:] = _route(logits)


def _oproj(attn, w_o, x, gt, g, sc, sh, w_route, b_route, seq):
    t, d = x.shape
    tm = ROW_TILE
    per_batch = seq // tm
    row = pl.BlockSpec((tm, d), lambda i: (i, 0))
    mod = pl.BlockSpec((1, 1, d), lambda i: (i // per_batch, 0, 0))

    def full(a):
        return pl.BlockSpec(a.shape, lambda i: (0,) * a.ndim)

    g2 = g.reshape(1, d)
    return pl.pallas_call(
        _oproj_kernel,
        grid=(t // tm,),
        in_specs=[pl.BlockSpec((tm, attn.shape[1]), lambda i: (i, 0)), full(w_o), row, mod, full(g2), mod, mod,
                  full(w_route), full(b_route)],
        out_specs=[row, pl.BlockSpec((tm, d + LANES), lambda i: (i, 0))],
        out_shape=[jax.ShapeDtypeStruct((t, d), F32), jax.ShapeDtypeStruct((t, d + LANES), F32)],
        compiler_params=_cparams(1),
        name="oproj_router",
    )(attn, w_o, x, gt, g2, sc, sh, w_route, b_route)


ROWS_PER_STEP = 1024


def _sort_kernel(route_ref, ltri_ref, utri_ref, pos_ref, counts_ref, cnt_scr, off_scr, *, tile):
    phase, i = pl.program_id(0), pl.program_id(1)
    lane = lax.broadcasted_iota(jnp.int32, route_ref.shape, 1).astype(F32)
    onehot = jnp.where(lane == route_ref[:, 0:1], 1.0, 0.0)

    @pl.when((phase == 0) & (i == 0))
    def _():
        cnt_scr[...] = jnp.zeros_like(cnt_scr)

    @pl.when((phase == 1) & (i == 0))
    def _():
        cnt = cnt_scr[...]
        counts_ref[...] = cnt
        n_tiles = jnp.floor((cnt + (tile - 1.0)) * (1.0 / tile))
        before = _dot(jnp.broadcast_to(n_tiles, (8, LANES)).astype(BF16), utri_ref[...])
        off_scr[...] = before[0:1] * tile
        cnt_scr[...] = jnp.zeros_like(cnt_scr)

    @pl.when(phase == 1)
    def _():
        earlier = _dot(ltri_ref[...], onehot.astype(BF16))
        row = jnp.sum(onehot * (earlier + off_scr[...] + cnt_scr[...]), axis=1, keepdims=True)
        pos_ref[...] = row.astype(jnp.int32)

    cnt_scr[...] += jnp.sum(onehot, axis=0, keepdims=True)


def _sorted_rows(hx, tile):
    t, w = hx.shape
    tm = 2 * ROW_TILE
    ltri = (jnp.arange(tm)[:, None] > jnp.arange(tm)[None, :]).astype(BF16)
    utri = (jnp.arange(LANES)[:, None] < jnp.arange(LANES)[None, :]).astype(BF16)
    pos, counts = pl.pallas_call(
        functools.partial(_sort_kernel, tile=tile),
        grid=(2, t // tm),
        in_specs=[pl.BlockSpec((tm, LANES), lambda ph, i: (i, w // LANES - 1)),
                  pl.BlockSpec((tm, tm), lambda ph, i: (0, 0)), pl.BlockSpec((LANES, LANES), lambda ph, i: (0, 0))],
        out_specs=[pl.BlockSpec((tm, 1), lambda ph, i: (i * ph, 0)), pl.BlockSpec((1, LANES), lambda ph, i: (0, 0))],
        out_shape=[jax.ShapeDtypeStruct((t, 1), jnp.int32), jax.ShapeDtypeStruct((1, LANES), F32)],
        scratch_shapes=[pltpu.VMEM((1, LANES), F32), pltpu.VMEM((1, LANES), F32)],
        compiler_params=_cparams(2),
        name="moe_sort",
    )(hx, ltri, utri)
    return pos.reshape(t), counts.reshape(LANES)


def _row_dma_params():
    return pltpu.CompilerParams(dimension_semantics=("arbitrary",), vmem_limit_bytes=VMEM_LIMIT,
                                disable_bounds_checks=True)


def _dispatch_kernel(pos_ref, hx_ref, zeros_hbm, xs_hbm, sem):
    del zeros_hbm
    rows = hx_ref.shape[0]
    base = pl.program_id(0) * rows

    def issue(r, c):
        pltpu.make_async_copy(hx_ref.at[pl.ds(r, 1)], xs_hbm.at[pl.ds(pos_ref[base + r], 1)], sem).start()
        return c

    lax.fori_loop(0, rows, issue, 0, unroll=8)
    pltpu.make_async_copy(hx_ref, xs_hbm.at[pl.ds(0, rows)], sem).wait()


def _dispatch(pos, hx, n_rows):
    t, w = hx.shape
    any_spec = pl.BlockSpec(memory_space=pl.ANY)
    grid_spec = pltpu.PrefetchScalarGridSpec(
        num_scalar_prefetch=1, grid=(t // ROWS_PER_STEP,),
        in_specs=[pl.BlockSpec((ROWS_PER_STEP, w), lambda i, pos: (i, 0)), any_spec], out_specs=any_spec,
        scratch_shapes=[pltpu.SemaphoreType.DMA])
    return pl.pallas_call(
        _dispatch_kernel,
        grid_spec=grid_spec,
        out_shape=jax.ShapeDtypeStruct((n_rows, w), F32),
        input_output_aliases={2: 0},
        compiler_params=_row_dma_params(),
        name="moe_dispatch",
    )(pos, hx, jnp.zeros((n_rows, w), F32))


def _experts_kernel(lo_ref, hi_ref, used_ref, xs_ref, wgu_lo_ref, wgu_hi_ref, wd_lo_ref, wd_hi_ref, ys_ref):
    del lo_ref, hi_ref
    d = ys_ref.shape[1]

    @pl.when(used_ref[pl.program_id(0)] == 0)
    def _():
        ys_ref[...] = jnp.zeros_like(ys_ref)

    @pl.when(used_ref[pl.program_id(0)] == 1)
    def _():
        xb = xs_ref[:, :d].astype(BF16)
        route = xs_ref[:, d:]
        route_lane = lax.broadcasted_iota(jnp.int32, route.shape, 1)
        y = None
        for lane, wgu_ref, wd_ref in ((1, wgu_lo_ref, wd_lo_ref), (2, wgu_hi_ref, wd_hi_ref)):
            gu = _dot(xb, wgu_ref[0, 0].astype(BF16))
            gate, up = gu[:, :EXPERT_FF], gu[:, EXPERT_FF:]
            act = (gate / (1.0 + jnp.exp(-gate)) * up).astype(BF16)
            gate_w = jnp.sum(jnp.where(route_lane == lane, route, 0.0), axis=1, keepdims=True)
            term = gate_w * _dot(act, wd_ref[0, 0].astype(BF16))
            y = term if y is None else y + term
        ys_ref[...] = y


def _experts(xs, tile_lo, tile_hi, tile_used, w_gu, w_d, layer, tile):
    n_rows, w = xs.shape
    d = w - LANES
    gu_block, d_block = (1, 1) + w_gu.shape[2:], (1, 1) + w_d.shape[2:]
    grid_spec = pltpu.PrefetchScalarGridSpec(
        num_scalar_prefetch=3,
        grid=(n_rows // tile,),
        in_specs=[pl.BlockSpec((tile, w), lambda i, lo, hi, used: (i, 0)),
                  pl.BlockSpec(gu_block, lambda i, lo, hi, used: (layer, lo[i], 0, 0)),
                  pl.BlockSpec(gu_block, lambda i, lo, hi, used: (layer, hi[i], 0, 0)),
                  pl.BlockSpec(d_block, lambda i, lo, hi, used: (layer, lo[i], 0, 0)),
                  pl.BlockSpec(d_block, lambda i, lo, hi, used: (layer, hi[i], 0, 0))],
        out_specs=pl.BlockSpec((tile, d), lambda i, lo, hi, used: (i, 0)),
    )
    return pl.pallas_call(
        _experts_kernel,
        grid_spec=grid_spec,
        out_shape=jax.ShapeDtypeStruct((n_rows, d), F32),
        compiler_params=_cparams(1),
        name="moe_experts",
    )(tile_lo, tile_hi, tile_used, xs, w_gu, w_gu, w_d, w_d)


def _combine_kernel(pos_ref, ys_hbm, x_ref, gt_ref, g_ref, o_ref, buf, sem, *, final):
    tm = x_ref.shape[0]
    base = pl.program_id(0) * tm

    def issue(r, c):
        pltpu.make_async_copy(ys_hbm.at[pl.ds(pos_ref[base + r], 1)], buf.at[pl.ds(r, 1)], sem).start()
        return c

    lax.fori_loop(0, tm, issue, 0, unroll=8)
    pltpu.make_async_copy(ys_hbm.at[pl.ds(0, tm)], buf, sem).wait()
    y = x_ref[...] + gt_ref[0] * buf[...]
    o_ref[...] = _rms(y) * g_ref[...] if final else y


def _combine(pos, ys, x, gt, g_final, seq, final):
    t, d = x.shape
    tm = ROW_TILE
    per_batch = seq // tm
    grid_spec = pltpu.PrefetchScalarGridSpec(
        num_scalar_prefetch=1,
        grid=(t // tm,),
        in_specs=[pl.BlockSpec(memory_space=pl.ANY), pl.BlockSpec((tm, d), lambda i, pos: (i, 0)),
                  pl.BlockSpec((1, 1, d), lambda i, pos: (i // per_batch, 0, 0)),
                  pl.BlockSpec((1, d), lambda i, pos: (0, 0))],
        out_specs=pl.BlockSpec((tm, d), lambda i, pos: (i, 0)),
        scratch_shapes=[pltpu.VMEM((tm, d), F32), pltpu.SemaphoreType.DMA],
    )
    return pl.pallas_call(
        functools.partial(_combine_kernel, final=final),
        grid_spec=grid_spec,
        out_shape=jax.ShapeDtypeStruct((t, d), F32),
        compiler_params=_row_dma_params(),
        name="moe_combine",
    )(pos, ys, x, gt, g_final.reshape(1, d))


def _moe(hx, w_gu, w_d, layer, x, gt, g_final, seq, final):
    t = x.shape[0]
    tile = MOE_TILE
    n_tiles = t // tile + N_CLASSES
    pos, counts = _sorted_rows(hx, tile)
    tiles_per_class = jnp.ceil(counts[:N_CLASSES] / tile).astype(jnp.int32)
    ends = jnp.cumsum(tiles_per_class)
    tile_ids = jnp.arange(n_tiles, dtype=jnp.int32)
    tile_class = jnp.minimum(jnp.searchsorted(ends, tile_ids, side="right"), N_CLASSES - 1)
    tile_used = (tile_ids < ends[-1]).astype(jnp.int32)
    lo, hi = _class_experts()
    tile_lo = jnp.asarray(lo, jnp.int32)[tile_class]
    tile_hi = jnp.asarray(hi, jnp.int32)[tile_class]
    xs = _dispatch(pos, hx, n_tiles * tile)
    ys = _experts(xs, tile_lo, tile_hi, tile_used, w_gu, w_d, layer, tile)
    return _combine(pos, ys, x, gt, g_final, seq, final)


def _diff_qk_layout(w):
    d = w.shape[0]
    w = w.reshape(d, DIFF_HEADS, 2, 2, HEAD_DIM // 2)
    return w.transpose(0, 1, 3, 2, 4).reshape(d, -1)


def _mla_layouts(w_in, w_q_up, w_kv_up):
    d = w_in.shape[0]
    half = MLA_ROPE // 2
    base = MLA_Q_RANK + MLA_KV_RANK
    r1, r2 = w_in[:, base:base + half], w_in[:, base + half:]
    z = jnp.zeros((d, 64 - 2 * half), w_in.dtype)
    w_in_l = jnp.concatenate([w_in[:, :base], r1, r1, z, r2, r2, z], axis=1)

    r = w_q_up.shape[0]
    wq = w_q_up.reshape(r, MLA_HEADS, MLA_NOPE + MLA_ROPE)
    nope = wq[:, :, :MLA_NOPE].reshape(r, -1)
    q1 = wq[:, :, MLA_NOPE:MLA_NOPE + half].reshape(r, MLA_HEADS // 2, 2 * half)
    q2 = wq[:, :, MLA_NOPE + half:].reshape(r, MLA_HEADS // 2, 2 * half)
    zq = jnp.zeros((r, MLA_HEADS // 2, 64 - 2 * half), w_q_up.dtype)
    rope = jnp.concatenate([q1, zq, q2, zq], axis=2).reshape(r, -1)
    w_q_l = jnp.concatenate([nope, rope], axis=1)

    rk = w_kv_up.shape[0]
    wkv = w_kv_up.reshape(rk, MLA_HEADS, MLA_NOPE + MLA_V)
    w_kn = wkv[:, :, :MLA_NOPE].reshape(rk, -1)
    w_v = wkv[:, :, MLA_NOPE:].reshape(rk, -1)
    return w_in_l.astype(BF16), w_q_l.T.astype(BF16), w_kn.astype(BF16), w_v.T.astype(BF16)


def _router_layout(w_group, b_group, w_router, b_router):
    d = w_group.shape[0]
    pad = LANES - N_EXPERTS - N_GROUPS
    w = jnp.concatenate([w_router, w_group, jnp.zeros((d, pad), F32)], axis=1)
    b = jnp.concatenate([b_router, b_group, jnp.zeros((pad,), F32)]).reshape(1, LANES)
    w_hi = w.astype(BF16)
    w_lo = (w - w_hi.astype(F32)).astype(BF16)
    return jnp.concatenate([w_hi, w_lo], axis=1), b


def kernel(x, c, positions, norm_mix_g, norm_ffn_g, ada_w, ada_b, sb_w_qkv, sb_w_o, mla_w_in, mla_g_q, mla_g_kv, mla_w_q_up, mla_w_kv_up, mla_w_o, diff_w_qkv, diff_lam_q1, diff_lam_k1, diff_lam_q2, diff_lam_k2, diff_g_sub, diff_w_o, moe_w_group, moe_b_group, moe_w_router, moe_b_router, moe_w_gate_up, moe_w_down, final_g):
    bsz, seq, d = x.shape
    depth = ada_w.shape[0]
    xt = x.reshape(bsz * seq, d)
    mod = _adaln(c, ada_w, ada_b)
    rope_mla = _rope_tables(positions, MLA_ROPE) if depth > 1 else None
    rope_diff = _rope_tables(positions, HEAD_DIM) if depth > 2 else None

    for i in range(depth):
        sh_m, sc_m, gt_m, sh_f, sc_f, gt_f = (mod[i, :, k * d:(k + 1) * d].reshape(bsz, 1, d) for k in range(6))
        kind, j = i % N_MIXERS, i // N_MIXERS
        if kind == 0:
            w = sb_w_qkv[j].astype(BF16)
            q_t, k, v_t = _qkv_proj(xt, norm_mix_g[i], sc_m, sh_m, w[:, :d].T, w[:, d:2 * d], w[:, 2 * d:].T, seq)
            attn = _sb_attention(q_t, k, v_t, bsz, seq)
            w_o = sb_w_o[j]
        elif kind == 1:
            w_in_l, wq_t, w_kn, wv_t = _mla_layouts(mla_w_in[j], mla_w_q_up[j], mla_w_kv_up[j])
            q_t, kn, kr, v_t = _mla_proj(xt, norm_mix_g[i], sc_m, sh_m, w_in_l, mla_g_q[j], mla_g_kv[j], wq_t, w_kn,
                                         wv_t, rope_mla, seq)
            attn = _mla_attention(q_t, kn, kr, v_t, bsz, seq)
            w_o = mla_w_o[j]
        else:
            w = diff_w_qkv[j].astype(BF16)
            q_t, k, v_t = _qkv_proj(xt, norm_mix_g[i], sc_m, sh_m, _diff_qk_layout(w[:, :d]).T,
                                    _diff_qk_layout(w[:, d:2 * d]), w[:, 2 * d:].T, seq, rope_diff)
            lam_params = jnp.stack([diff_lam_q1[j], diff_lam_k1[j], diff_lam_q2[j], diff_lam_k2[j]])
            lambda_init = 0.8 - 0.6 * math.exp(-0.3 * i)
            attn = _diff_attention(q_t, k, v_t, lam_params, diff_g_sub[j], lambda_init, bsz, seq)
            w_o = diff_w_o[j]
        w_route, b_route = _router_layout(moe_w_group[i], moe_b_group[i], moe_w_router[i], moe_b_router[i])
        xt, hx = _oproj(attn, w_o.astype(BF16), xt, gt_m, norm_ffn_g[i], sc_f, sh_f, w_route, b_route, seq)
        xt = _moe(hx, moe_w_gate_up, moe_w_down, i, xt, gt_f, final_g, seq, final=i == depth - 1)
    return xt.reshape(bsz, seq, d)
```

```python
import functools
import math

import jax
import jax.numpy as jnp
from jax import lax
from jax.experimental import pallas as pl
from jax.experimental.pallas import tpu as pltpu

F32 = jnp.float32
BF16 = jnp.bfloat16

N_MIXERS = 3
ROPE_THETA = 10000.0
EPS = 1e-6
HEAD_DIM = 64
SB_HEADS = 16
MLA_HEADS = 16
MLA_Q_RANK = 384
MLA_KV_RANK = 256
MLA_NOPE = 64
MLA_ROPE = 32
MLA_V = 64
DIFF_HEADS = 8
N_GROUPS = 4
EXPERTS_PER_GROUP = 4
N_EXPERTS = N_GROUPS * EXPERTS_PER_GROUP
EXPERT_FF = 512

LANES = 128
LOG2E = 1.4426950408889634
NEG_BIG = -1e30
VMEM_LIMIT = 56 * 1024 * 1024

ROW_TILE = 512
ATT_TILE = 256
MOE_TILE = 256
PROJ_CHUNK = 256


def _cparams(n_axes):
    return pltpu.CompilerParams(dimension_semantics=("arbitrary",) * n_axes, vmem_limit_bytes=VMEM_LIMIT)


def _dot(a, b):
    return jnp.dot(a, b, preferred_element_type=F32)


def _dot_nt(a, b):
    return lax.dot_general(a, b, (((1,), (1,)), ((), ())), preferred_element_type=F32)


def _rms(x):
    return x * lax.rsqrt(jnp.mean(x * x, axis=-1, keepdims=True) + EPS)


def _rope_blocks(y, cos, sin_signed):
    out = []
    for j in range(y.shape[1] // LANES):
        yb = y[:, j * LANES:(j + 1) * LANES]
        out.append(yb * cos + pltpu.roll(yb, 64, 1) * sin_signed)
    return out[0] if len(out) == 1 else jnp.concatenate(out, axis=1)


def _rope_rows(y, cos_t, sin_t):
    out = []
    for j in range(y.shape[0] // LANES):
        yb = y[j * LANES:(j + 1) * LANES]
        out.append(yb * cos_t + jnp.concatenate([yb[64:], yb[:64]], axis=0) * sin_t)
    return out[0] if len(out) == 1 else jnp.concatenate(out, axis=0)


def _adaln_kernel(c_ref, w_ref, b_ref, o_ref):
    c = c_ref[...]
    ca = c / (1.0 + jnp.exp(-c))
    o_ref[0] = jnp.dot(ca, w_ref[0], preferred_element_type=F32, precision=lax.Precision.HIGHEST) + b_ref[0]


def _adaln(c, ada_w, ada_b):
    depth, d, n = ada_w.shape
    bsz = c.shape[0]
    rows = 8
    cp = jnp.zeros((rows, d), F32).at[:bsz].set(c)
    tn = 1536
    out = pl.pallas_call(
        _adaln_kernel,
        grid=(depth, n // tn),
        in_specs=[
            pl.BlockSpec((rows, d), lambda i, j: (0, 0)),
            pl.BlockSpec((1, d, tn), lambda i, j: (i, 0, j)),
            pl.BlockSpec((1, 1, tn), lambda i, j: (i, 0, j)),
        ],
        out_specs=pl.BlockSpec((1, rows, tn), lambda i, j: (i, 0, j)),
        out_shape=jax.ShapeDtypeStruct((depth, rows, n), F32),
        compiler_params=_cparams(2),
        name="adaln_mod",
    )(cp, ada_w, ada_b.reshape(depth, 1, n))
    return out[:, :bsz]


def _rope_table_kernel(pos_ref, invf_ref, sign_ref, cos_ref, sin_ref, cost_ref, sint_ref):
    ang = pos_ref[...].astype(F32) * invf_ref[...]
    cos = jnp.cos(ang)
    sin = jnp.sin(ang) * sign_ref[...]
    cos_ref[...] = cos
    sin_ref[...] = sin
    cost_ref[...] = cos.T
    sint_ref[...] = sin.T


def _rope_tables(positions, dim):
    t = positions.size
    half = dim // 2
    inv_freq = ROPE_THETA ** (-jnp.arange(0, dim, 2, dtype=F32) / dim)
    invf = jnp.tile(inv_freq, LANES // half)
    sign = jnp.where(jnp.arange(LANES) < 64, -1.0, 1.0).astype(F32)
    tm = 2048
    small_r = pl.BlockSpec((1, LANES), lambda i: (0, 0))
    return pl.pallas_call(
        _rope_table_kernel,
        grid=(t // tm,),
        in_specs=[pl.BlockSpec((tm, 1), lambda i: (i, 0)), small_r, small_r],
        out_specs=[pl.BlockSpec((tm, LANES), lambda i: (i, 0))] * 2 + [pl.BlockSpec((LANES, tm), lambda i: (0, i))] * 2,
        out_shape=[jax.ShapeDtypeStruct((t, LANES), F32)] * 2 + [jax.ShapeDtypeStruct((LANES, t), F32)] * 2,
        compiler_params=_cparams(1),
        name="rope_tables",
    )(positions.reshape(t, 1), invf.reshape(1, LANES), sign.reshape(1, LANES))


def _modulated_norm(x_ref, g_ref, sc_ref, sh_ref):
    h = _rms(x_ref[...]) * g_ref[...]
    return h * (1.0 + sc_ref[0]) + sh_ref[0]


def _store_token_blocks(o_ref, w_t_ref, hb, post=None):
    tb = o_ref.shape[2]
    for r in range(0, w_t_ref.shape[0], PROJ_CHUNK):
        y = _dot_nt(w_t_ref[r:r + PROJ_CHUNK, :], hb)
        if post is not None:
            y = post(y, r)
        y = y.astype(BF16)
        for c in range(o_ref.shape[0]):
            o_ref[c, r:r + PROJ_CHUNK, :] = y[:, c * tb:(c + 1) * tb]


def _qkv_kernel(*refs, rope, q_scale):
    x_ref, g_ref, sc_ref, sh_ref, wqt_ref, wk_ref, wvt_ref = refs[:7]
    q_ref, k_ref, v_ref = refs[-3:]
    hb = _modulated_norm(x_ref, g_ref, sc_ref, sh_ref).astype(BF16)
    if rope:
        cos_ref, sin_ref, cost_ref, sint_ref = refs[7:11]
        _store_token_blocks(q_ref, wqt_ref, hb, lambda qt, r: _rope_rows(qt, cost_ref[...], sint_ref[...]) * q_scale)
    else:
        _store_token_blocks(q_ref, wqt_ref, hb, lambda qt, r: qt * q_scale)
    for r in range(0, wk_ref.shape[1], 512):
        k = _dot(hb, wk_ref[:, r:r + 512])
        if rope:
            k = _rope_blocks(k, cos_ref[...], sin_ref[...])
        k_ref[:, r:r + 512] = k.astype(BF16)
    _store_token_blocks(v_ref, wvt_ref, hb)


def _qkv_proj(x, g, sc, sh, wq_t, wk, wv_t, seq, rope=None):
    t, d = x.shape
    tm, tk = ROW_TILE, ATT_TILE
    per_batch = seq // tm
    row = pl.BlockSpec((tm, d), lambda i: (i, 0))
    mod = pl.BlockSpec((1, 1, d), lambda i: (i // per_batch, 0, 0))

    def full(a):
        return pl.BlockSpec(a.shape, lambda i: (0,) * a.ndim)

    g2 = g.reshape(1, d)
    in_specs = [row, full(g2), mod, mod, full(wq_t), full(wk), full(wv_t)]
    if rope is not None:
        tab = pl.BlockSpec((tm, LANES), lambda i: (i, 0))
        tab_t = pl.BlockSpec((LANES, tm), lambda i: (0, i))
        in_specs += [tab, tab, tab_t, tab_t]
    nq, nk, nv = wq_t.shape[0], wk.shape[1], wv_t.shape[0]
    return pl.pallas_call(
        functools.partial(_qkv_kernel, rope=rope is not None, q_scale=HEAD_DIM ** -0.5),
        grid=(t // tm,),
        in_specs=in_specs,
        out_specs=[pl.BlockSpec((tm // tk, nq, tk), lambda i: (i, 0, 0)), pl.BlockSpec((tm, nk), lambda i: (i, 0)),
                   pl.BlockSpec((tm // tk, nv, tk), lambda i: (i, 0, 0))],
        out_shape=[jax.ShapeDtypeStruct((t // tk, nq, tk), BF16), jax.ShapeDtypeStruct((t, nk), BF16),
                   jax.ShapeDtypeStruct((t // tk, nv, tk), BF16)],
        compiler_params=_cparams(1),
        name="qkv_proj",
    )(x, g2, sc, sh, wq_t, wk, wv_t, *(rope or ()))


def _mla_proj_kernel(x_ref, g_ref, sc_ref, sh_ref, win_ref, gq_ref, gkv_ref, wqt_ref, wkn_ref, wvt_ref,
                     cos_ref, sin_ref, cost_ref, sint_ref, q_ref, kn_ref, kr_ref, v_ref):
    hb = _modulated_norm(x_ref, g_ref, sc_ref, sh_ref).astype(BF16)
    lat = _dot(hb, win_ref[...])
    cq = (_rms(lat[:, :MLA_Q_RANK]) * gq_ref[...]).astype(BF16)
    ckv = (_rms(lat[:, MLA_Q_RANK:MLA_Q_RANK + MLA_KV_RANK]) * gkv_ref[...]).astype(BF16)
    kr_ref[...] = _rope_blocks(lat[:, MLA_Q_RANK + MLA_KV_RANK:], cos_ref[...], sin_ref[...]).astype(BF16)
    n_nope = MLA_HEADS * MLA_NOPE
    _store_token_blocks(q_ref, wqt_ref, cq,
                        lambda qt, r: _rope_rows(qt, cost_ref[...], sint_ref[...]) if r >= n_nope else qt)
    kn_ref[...] = _dot(ckv, wkn_ref[...]).astype(BF16)
    _store_token_blocks(v_ref, wvt_ref, ckv)


def _mla_proj(x, g, sc, sh, w_in, g_q, g_kv, wq_t, wkn, wv_t, rope, seq):
    t, d = x.shape
    tm, tk = ROW_TILE, ATT_TILE
    per_batch = seq // tm
    row = pl.BlockSpec((tm, d), lambda i: (i, 0))
    mod = pl.BlockSpec((1, 1, d), lambda i: (i // per_batch, 0, 0))

    def full(a):
        return pl.BlockSpec(a.shape, lambda i: (0,) * a.ndim)

    tab = pl.BlockSpec((tm, LANES), lambda i: (i, 0))
    tab_t = pl.BlockSpec((LANES, tm), lambda i: (0, i))
    g2, gq2, gkv2 = g.reshape(1, d), g_q.reshape(1, -1), g_kv.reshape(1, -1)
    nq, nk, nv = wq_t.shape[0], wkn.shape[1], wv_t.shape[0]
    return pl.pallas_call(
        _mla_proj_kernel,
        grid=(t // tm,),
        in_specs=[row, full(g2), mod, mod, full(w_in), full(gq2), full(gkv2), full(wq_t), full(wkn), full(wv_t),
                  tab, tab, tab_t, tab_t],
        out_specs=[pl.BlockSpec((tm // tk, nq, tk), lambda i: (i, 0, 0)), pl.BlockSpec((tm, nk), lambda i: (i, 0)),
                   pl.BlockSpec((tm, LANES), lambda i: (i, 0)), pl.BlockSpec((tm // tk, nv, tk), lambda i: (i, 0, 0))],
        out_shape=[jax.ShapeDtypeStruct((t // tk, nq, tk), BF16), jax.ShapeDtypeStruct((t, nk), BF16),
                   jax.ShapeDtypeStruct((t, LANES), BF16), jax.ShapeDtypeStruct((t // tk, nv, tk), BF16)],
        compiler_params=_cparams(1),
        name="mla_proj",
    )(x, g2, sc, sh, w_in, gq2, gkv2, wq_t, wkn, wv_t, *rope)


SOFTMAX_ITEMS_PER_ITER = 2
SB_ITEMS_PER_ITER = 1
SOFTMAX_ROWS = 32


def _attention_items(nq, depth, per_iter):
    rows = [(qi, qi - s, int(s == 0), int(s == qi), qi & 1, qi & 3) for qi in range(nq) for s in range(qi + 1)]
    rows += [(0, 0, 1, 0, nq & 1, nq & 3)] * (-len(rows) % per_iter + (depth - 1) * per_iter)
    return jnp.asarray(list(zip(*rows)), jnp.int32)


def _diag_bias(t, strict):
    key = jnp.arange(t)[:, None]
    query = jnp.arange(t)[None, :]
    masked = key >= query if strict else key > query
    return jnp.stack([jnp.zeros((t, t), F32), jnp.where(masked, NEG_BIG, 0.0).astype(F32)])


def _run_pipeline(stages, n_items, finalize, per_iter):
    depth = len(stages)

    def iteration(it, static):
        for k in reversed(range(depth)):
            if static and it < k:
                continue
            for u in range(per_iter):
                stages[k]((it - k) * per_iter + u)

    for it in range(depth - 1):
        iteration(it, True)

    def body(it, c):
        iteration(it, False)
        for u in range(per_iter):
            finalize((it - (depth - 1)) * per_iter + u)
        return c

    lax.fori_loop(depth - 1, n_items // per_iter, body, 0)


def _pipe_call(kernel_fn, name, depth, bsz, seq, pairs, in_specs, args, scratch, per_iter):
    t = ATT_TILE
    items = _attention_items(seq // t, depth, per_iter)
    grid_spec = pltpu.PrefetchScalarGridSpec(
        num_scalar_prefetch=1,
        grid=(bsz, pairs),
        in_specs=in_specs,
        out_specs=pl.BlockSpec((seq, LANES), lambda b, p, it: (b, p)),
        scratch_shapes=scratch,
    )
    return pl.pallas_call(
        kernel_fn,
        grid_spec=grid_spec,
        out_shape=jax.ShapeDtypeStruct((bsz * seq, pairs * LANES), BF16),
        compiler_params=_cparams(2),
        name=name,
    )(items, *args)


def _blocked_spec(seq, t, offset=0):
    return pl.BlockSpec((seq // t, LANES, t), lambda b, p, it: (b, offset + p, 0))


def _const_spec(a):
    return pl.BlockSpec(a.shape, lambda b, p, it: (0,) * a.ndim)


def _keep_rows(x, keep):
    n_rows, n = x.shape
    parts, pos = [], 0
    for a, b in keep:
        if a > pos:
            parts.append(jnp.zeros((a - pos, n), x.dtype))
        parts.append(x[a:b])
        pos = b
    if pos < n_rows:
        parts.append(jnp.zeros((n_rows - pos, n), x.dtype))
    return jnp.concatenate(parts, axis=0)


def _store_tile(o_ref, qi, o_t):
    t = o_t.shape[1]
    o_ref[pl.ds(pl.multiple_of(qi * t, t), t), :] = o_t.T.astype(o_ref.dtype)


def _softmax_pipeline(items_ref, q_heads_of, k_ref, vt_ref, bias_ref, scr, c_log2, finalize_tile):
    s_scr, p_scr, alpha_scr, lfin_scr, m_ref, l_ref, acc_ref = scr
    t = s_scr.shape[-1]
    n_slots = s_scr.shape[0]
    m_ref[...] = jnp.zeros_like(m_ref)
    l_ref[...] = jnp.zeros_like(l_ref)
    acc_ref[...] = jnp.zeros_like(acc_ref)

    def scores(w):
        qi, j, first = items_ref[0, w], items_ref[1, w], items_ref[2, w]
        kb = k_ref[pl.ds(pl.multiple_of(j * t, t), t), :]
        bias = bias_ref[first]
        for hd, qt in enumerate(q_heads_of(qi)):
            s_scr[w & (n_slots - 1), hd] = _dot(kb, qt) * c_log2 + bias

    def softmax(w):
        slot = w & (n_slots - 1)
        first = items_ref[2, w] == 1
        for hd in range(2):
            mx = s_scr[slot, hd, 0:SOFTMAX_ROWS, :]
            for r in range(SOFTMAX_ROWS, t, SOFTMAX_ROWS):
                mx = jnp.maximum(mx, s_scr[slot, hd, r:r + SOFTMAX_ROWS, :])
            m_old = jnp.where(first, NEG_BIG, m_ref[hd])
            m_new = jnp.maximum(m_old, jnp.max(mx, axis=0, keepdims=True))
            alpha = jnp.where(first, 0.0, jnp.exp2(m_old - m_new))
            psum = None
            for r in range(0, t, SOFTMAX_ROWS):
                p = jnp.exp2(s_scr[slot, hd, r:r + SOFTMAX_ROWS, :] - m_new)
                p_scr[slot, hd, r:r + SOFTMAX_ROWS, :] = p.astype(BF16)
                psum = p if psum is None else psum + p
            l_new = alpha * l_ref[hd] + jnp.sum(psum, axis=0, keepdims=True)
            l_ref[hd] = l_new
            m_ref[hd] = m_new
            alpha_scr[slot, hd] = alpha
            lfin_scr[items_ref[5, w], hd] = l_new

    def values(w):
        slot = w & (n_slots - 1)
        vb = vt_ref[items_ref[1, w]]
        par = items_ref[4, w]
        for hd in range(2):
            acc_ref[par, hd] = alpha_scr[slot, hd] * acc_ref[par, hd] + _dot(vb, p_scr[slot, hd])

    def finalize(w):
        @pl.when(items_ref[3, w] == 1)
        def _():
            finalize_tile(items_ref[0, w], acc_ref.at[items_ref[4, w]], lfin_scr[items_ref[5, w]])

    _run_pipeline([scores, softmax, values], items_ref.shape[1], finalize, n_slots // 2)


def _mla_pipe_kernel(items_ref, qn_ref, qr_ref, kn_ref, kr_ref, vt_ref, bias_ref, o_ref, kcat_ref, *scr, scale):
    kcat_ref[:, :LANES] = kn_ref[...]
    kcat_ref[:, LANES:] = kr_ref[...]
    half = MLA_ROPE // 2

    def q_heads_of(qi):
        qn = qn_ref[qi]
        qr = qr_ref[qi]
        return (
            jnp.concatenate([_keep_rows(qn, [(0, MLA_NOPE)]), _keep_rows(qr, [(0, half), (64, 64 + half)])], axis=0),
            jnp.concatenate([_keep_rows(qn, [(MLA_NOPE, 2 * MLA_NOPE)]),
                             _keep_rows(qr, [(half, 2 * half), (64 + half, 64 + 2 * half)])], axis=0),
        )

    def finalize_tile(qi, acc_ref, l_fin):
        o_t = jnp.concatenate([acc_ref[0, :MLA_V] / l_fin[0], acc_ref[1, MLA_V:] / l_fin[1]], axis=0)
        _store_tile(o_ref, qi, o_t)

    _softmax_pipeline(items_ref, q_heads_of, kcat_ref, vt_ref, bias_ref, scr, scale * LOG2E, finalize_tile)


def _diff_pipe_kernel(items_ref, q_ref, k_ref, vt_ref, bias_ref, lam_ref, gsub_ref, o_ref, *scr, lambda_init):
    def q_heads_of(qi):
        qt = q_ref[qi]
        return _keep_rows(qt, [(0, 32), (64, 96)]), _keep_rows(qt, [(32, 64), (96, 128)])

    def finalize_tile(qi, acc_ref, l_fin):
        lp = lam_ref[...]
        lam = (jnp.exp(jnp.sum(lp[0:1] * lp[1:2], axis=1, keepdims=True))
               - jnp.exp(jnp.sum(lp[2:3] * lp[3:4], axis=1, keepdims=True)) + lambda_init)
        o_t = acc_ref[0] / l_fin[0] - lam * (acc_ref[1] / l_fin[1])
        o_t = o_t * lax.rsqrt(jnp.mean(o_t * o_t, axis=0, keepdims=True) + EPS)
        t = o_t.shape[1]
        o = o_t.T * gsub_ref[...] * (1.0 - lambda_init)
        o_ref[pl.ds(pl.multiple_of(qi * t, t), t), :] = o.astype(o_ref.dtype)

    _softmax_pipeline(items_ref, q_heads_of, k_ref, vt_ref, bias_ref, scr, LOG2E, finalize_tile)


def _softmax_scratch(t):
    n = 2 * SOFTMAX_ITEMS_PER_ITER
    return [pltpu.VMEM((n, 2, t, t), F32), pltpu.VMEM((n, 2, t, t), BF16), pltpu.VMEM((n, 2, 1, t), F32),
            pltpu.VMEM((4, 2, 1, t), F32), pltpu.VMEM((2, 1, t), F32), pltpu.VMEM((2, 1, t), F32),
            pltpu.VMEM((2, 2, LANES, t), F32)]


def _mla_attention(q_t, kn, kr, v_t, bsz, seq):
    t = ATT_TILE
    pairs = MLA_HEADS // 2
    bias = _diag_bias(t, False)
    in_specs = [_blocked_spec(seq, t), _blocked_spec(seq, t, pairs),
                pl.BlockSpec((seq, LANES), lambda b, p, it: (b, p)), pl.BlockSpec((seq, LANES), lambda b, p, it: (b, 0)),
                _blocked_spec(seq, t), _const_spec(bias)]
    scratch = [pltpu.VMEM((seq, 2 * LANES), BF16)] + _softmax_scratch(t)
    kernel_fn = functools.partial(_mla_pipe_kernel, scale=(MLA_NOPE + MLA_ROPE) ** -0.5)
    return _pipe_call(kernel_fn, "mla_attention", 3, bsz, seq, pairs, in_specs, (q_t, q_t, kn, kr, v_t, bias),
                      scratch, SOFTMAX_ITEMS_PER_ITER)


def _diff_attention(q_t, k, v_t, lam_params, g_sub, lambda_init, bsz, seq):
    t = ATT_TILE
    bias = _diag_bias(t, False)
    g2 = g_sub.reshape(1, LANES)
    in_specs = [_blocked_spec(seq, t), pl.BlockSpec((seq, LANES), lambda b, p, it: (b, p)), _blocked_spec(seq, t),
                _const_spec(bias), _const_spec(lam_params), _const_spec(g2)]
    kernel_fn = functools.partial(_diff_pipe_kernel, lambda_init=lambda_init)
    return _pipe_call(kernel_fn, "diff_attention", 3, bsz, seq, DIFF_HEADS, in_specs,
                      (q_t, k, v_t, bias, lam_params, g2), _softmax_scratch(t), SOFTMAX_ITEMS_PER_ITER)


def _sb_pipe_kernel(items_ref, q_ref, k_ref, vt_ref, lmat_ref, bias_ref, o_ref,
                    z_scr, sp_scr, cs_scr, w_scr, acc_ref, carry_ref):
    t = z_scr.shape[-1]
    n_slots = sp_scr.shape[0]
    lmat = lmat_ref[...]
    acc_ref[...] = jnp.zeros_like(acc_ref)
    carry_ref[...] = jnp.zeros_like(carry_ref)

    def scores(w):
        qi, j, first = items_ref[0, w], items_ref[1, w], items_ref[2, w]
        kb = k_ref[pl.ds(pl.multiple_of(j * t, t), t), :]
        qt = q_ref[qi]
        bias = bias_ref[first]
        for hd, rows in enumerate(([(0, HEAD_DIM)], [(HEAD_DIM, 2 * HEAD_DIM)])):
            z_scr[w & (2 * n_slots - 1), hd] = _dot(kb, _keep_rows(qt, rows)) + bias

    def softplus(w):
        for hd in range(2):
            z = z_scr[w & (2 * n_slots - 1), hd]
            sp = jnp.maximum(z, 0.0) + jnp.log(1.0 + jnp.exp2(jnp.abs(z) * -LOG2E))
            sp_scr[w & (n_slots - 1), hd] = sp.astype(BF16)
            z_scr[w & (2 * n_slots - 1), hd] = z - sp

    def sums(w):
        for hd in range(2):
            cs_scr[w & (n_slots - 1), hd] = _dot(lmat, sp_scr[w & (n_slots - 1), hd])

    def weights(w):
        first = items_ref[2, w] == 1
        for hd in range(2):
            cs = cs_scr[w & (n_slots - 1), hd]
            carry = jnp.where(first, 0.0, carry_ref[hd])
            w_scr[w & (n_slots - 1), hd] = jnp.exp(z_scr[w & (2 * n_slots - 1), hd] - cs - carry).astype(BF16)
            carry_ref[hd] = carry + cs[0:1, :] + sp_scr[w & (n_slots - 1), hd, 0:1, :].astype(F32)

    def values(w):
        vb = vt_ref[items_ref[1, w]]
        keep = jnp.where(items_ref[2, w] == 1, 0.0, 1.0)
        par = items_ref[4, w]
        for hd in range(2):
            acc_ref[par, hd] = keep * acc_ref[par, hd] + _dot(vb, w_scr[w & (n_slots - 1), hd])

    def finalize(w):
        @pl.when(items_ref[3, w] == 1)
        def _():
            par = items_ref[4, w]
            o_t = jnp.concatenate([acc_ref[par, 0, :HEAD_DIM], acc_ref[par, 1, HEAD_DIM:]], axis=0)
            _store_tile(o_ref, items_ref[0, w], o_t)

    _run_pipeline([scores, softplus, sums, weights, values], items_ref.shape[1], finalize, n_slots // 2)


def _sb_attention(q_t, k, v_t, bsz, seq):
    t = ATT_TILE
    lmat = (jnp.arange(t)[None, :] > jnp.arange(t)[:, None]).astype(BF16)
    bias = _diag_bias(t, True)
    in_specs = [_blocked_spec(seq, t), pl.BlockSpec((seq, LANES), lambda b, p, it: (b, p)), _blocked_spec(seq, t),
                _const_spec(lmat), _const_spec(bias)]
    n = 2 * SB_ITEMS_PER_ITER
    scratch = [pltpu.VMEM((2 * n, 2, t, t), F32), pltpu.VMEM((n, 2, t, t), BF16), pltpu.VMEM((n, 2, t, t), F32),
               pltpu.VMEM((n, 2, t, t), BF16), pltpu.VMEM((2, 2, LANES, t), F32), pltpu.VMEM((2, 1, t), F32)]
    return _pipe_call(_sb_pipe_kernel, "sb_attention", 5, bsz, seq, SB_HEADS // 2, in_specs,
                      (q_t, k, v_t, lmat, bias), scratch, SB_ITEMS_PER_ITER)


PAIRS_PER_GROUP = EXPERTS_PER_GROUP * (EXPERTS_PER_GROUP - 1) // 2
N_CLASSES = N_GROUPS * PAIRS_PER_GROUP


def _class_experts():
    lo, hi = [], []
    for g in range(N_GROUPS):
        for a in range(EXPERTS_PER_GROUP):
            for b in range(a + 1, EXPERTS_PER_GROUP):
                lo.append(g * EXPERTS_PER_GROUP + a)
                hi.append(g * EXPERTS_PER_GROUP + b)
    return lo, hi


def _route(logits):
    lane = lax.broadcasted_iota(jnp.int32, logits.shape, 1).astype(F32)
    big = jnp.float32(1 << 20)

    def top1(vals):
        v = jnp.max(vals, axis=1, keepdims=True)
        i = jnp.min(jnp.where(vals == v, lane, big), axis=1, keepdims=True)
        return v, i

    is_group = (lane >= N_EXPERTS) & (lane < N_EXPERTS + N_GROUPS)
    gl = jnp.where(is_group, logits, -jnp.inf)
    gmax, gidx = top1(gl)
    g_w = 1.0 / jnp.sum(jnp.exp(gl - gmax), axis=1, keepdims=True)
    first = (gidx - N_EXPERTS) * EXPERTS_PER_GROUP
    el = jnp.where((lane >= first) & (lane < first + EXPERTS_PER_GROUP), logits, -jnp.inf)
    v1, i1 = top1(el)
    v2, i2 = top1(jnp.where(lane == i1, -jnp.inf, el))
    e2 = jnp.exp(v2 - v1)
    w1 = g_w / (1.0 + e2)
    w2 = g_w * e2 / (1.0 + e2)
    group = gidx - N_EXPERTS
    a = jnp.minimum(i1, i2) - first
    b = jnp.maximum(i1, i2) - first
    pair = a * (2 * EXPERTS_PER_GROUP - 1 - a) * 0.5 + (b - a - 1.0)
    cls = group * PAIRS_PER_GROUP + pair
    w_lo = jnp.where(i1 < i2, w1, w2)
    w_hi = jnp.where(i1 < i2, w2, w1)
    return jnp.where(lane == 0, cls, jnp.where(lane == 1, w_lo, jnp.where(lane == 2, w_hi, 0.0)))


def _oproj_kernel(a_ref, wo_ref, x_ref, gt_ref, g_ref, sc_ref, sh_ref, wr_ref, br_ref, xo_ref, hx_ref):
    d = x_ref.shape[1]
    xn = x_ref[...] + gt_ref[0] * _dot(a_ref[...], wo_ref[...])
    xo_ref[...] = xn
    h = _rms(xn) * g_ref[...]
    h = h * (1.0 + sc_ref[0]) + sh_ref[0]
    hx_ref[:, :d] = h
    h_hi = h.astype(BF16)
    h_lo = (h - h_hi.astype(F32)).astype(BF16)
    both = _dot(h_hi, wr_ref[...])
    logits = both[:, :LANES] + both[:, LANES:] + _dot(h_lo, wr_ref[:, :LANES]) + br_ref[...]
    hx_ref[:, d:] = _route(logits)


def _oproj(attn, w_o, x, gt, g, sc, sh, w_route, b_route, seq):
    t, d = x.shape
    tm = ROW_TILE
    per_batch = seq // tm
    row = pl.BlockSpec((tm, d), lambda i: (i, 0))
    mod = pl.BlockSpec((1, 1, d), lambda i: (i // per_batch, 0, 0))

    def full(a):
        return pl.BlockSpec(a.shape, lambda i: (0,) * a.ndim)

    g2 = g.reshape(1, d)
    return pl.pallas_call(
        _oproj_kernel,
        grid=(t // tm,),
        in_specs=[pl.BlockSpec((tm, attn.shape[1]), lambda i: (i, 0)), full(w_o), row, mod, full(g2), mod, mod,
                  full(w_route), full(b_route)],
        out_specs=[row, pl.BlockSpec((tm, d + LANES), lambda i: (i, 0))],
        out_shape=[jax.ShapeDtypeStruct((t, d), F32), jax.ShapeDtypeStruct((t, d + LANES), F32)],
        compiler_params=_cparams(1),
        name="oproj_router",
    )(attn, w_o, x, gt, g2, sc, sh, w_route, b_route)


ROWS_PER_STEP = 1024


def _sort_kernel(route_ref, ltri_ref, utri_ref, pos_ref, counts_ref, cnt_scr, off_scr, *, tile):
    phase, i = pl.program_id(0), pl.program_id(1)
    lane = lax.broadcasted_iota(jnp.int32, route_ref.shape, 1).astype(F32)
    onehot = jnp.where(lane == route_ref[:, 0:1], 1.0, 0.0)

    @pl.when((phase == 0) & (i == 0))
    def _():
        cnt_scr[...] = jnp.zeros_like(cnt_scr)

    @pl.when((phase == 1) & (i == 0))
    def _():
        cnt = cnt_scr[...]
        counts_ref[...] = cnt
        n_tiles = jnp.floor((cnt + (tile - 1.0)) * (1.0 / tile))
        before = _dot(jnp.broadcast_to(n_tiles, (8, LANES)).astype(BF16), utri_ref[...])
        off_scr[...] = before[0:1] * tile
        cnt_scr[...] = jnp.zeros_like(cnt_scr)

    @pl.when(phase == 1)
    def _():
        earlier = _dot(ltri_ref[...], onehot.astype(BF16))
        row = jnp.sum(onehot * (earlier + off_scr[...] + cnt_scr[...]), axis=1, keepdims=True)
        pos_ref[...] = row.astype(jnp.int32)

    cnt_scr[...] += jnp.sum(onehot, axis=0, keepdims=True)


def _sorted_rows(hx, tile):
    t, w = hx.shape
    tm = 2 * ROW_TILE
    ltri = (jnp.arange(tm)[:, None] > jnp.arange(tm)[None, :]).astype(BF16)
    utri = (jnp.arange(LANES)[:, None] < jnp.arange(LANES)[None, :]).astype(BF16)
    pos, counts = pl.pallas_call(
        functools.partial(_sort_kernel, tile=tile),
        grid=(2, t // tm),
        in_specs=[pl.BlockSpec((tm, LANES), lambda ph, i: (i, w // LANES - 1)),
                  pl.BlockSpec((tm, tm), lambda ph, i: (0, 0)), pl.BlockSpec((LANES, LANES), lambda ph, i: (0, 0))],
        out_specs=[pl.BlockSpec((tm, 1), lambda ph, i: (i * ph, 0)), pl.BlockSpec((1, LANES), lambda ph, i: (0, 0))],
        out_shape=[jax.ShapeDtypeStruct((t, 1), jnp.int32), jax.ShapeDtypeStruct((1, LANES), F32)],
        scratch_shapes=[pltpu.VMEM((1, LANES), F32), pltpu.VMEM((1, LANES), F32)],
        compiler_params=_cparams(2),
        name="moe_sort",
    )(hx, ltri, utri)
    return pos.reshape(t), counts.reshape(LANES)


def _row_dma_params():
    return pltpu.CompilerParams(dimension_semantics=("arbitrary",), vmem_limit_bytes=VMEM_LIMIT,
                                disable_bounds_checks=True)


def _dispatch_kernel(pos_ref, hx_ref, zeros_hbm, xs_hbm, sem):
    del zeros_hbm
    rows = hx_ref.shape[0]
    base = pl.program_id(0) * rows

    def issue(g, c):
        for u in range(8):
            r = g * 8 + u
            pltpu.make_async_copy(hx_ref.at[pl.ds(r, 1)], xs_hbm.at[pl.ds(pos_ref[base + r], 1)], sem).start(
                priority=u % 2)
        return c

    lax.fori_loop(0, rows // 8, issue, 0)
    pltpu.make_async_copy(hx_ref, xs_hbm.at[pl.ds(0, rows)], sem).wait()


def _dispatch(pos, hx, n_rows):
    t, w = hx.shape
    any_spec = pl.BlockSpec(memory_space=pl.ANY)
    grid_spec = pltpu.PrefetchScalarGridSpec(
        num_scalar_prefetch=1, grid=(t // ROWS_PER_STEP,),
        in_specs=[pl.BlockSpec((ROWS_PER_STEP, w), lambda i, pos: (i, 0)), any_spec], out_specs=any_spec,
        scratch_shapes=[pltpu.SemaphoreType.DMA])
    return pl.pallas_call(
        _dispatch_kernel,
        grid_spec=grid_spec,
        out_shape=jax.ShapeDtypeStruct((n_rows, w), F32),
        input_output_aliases={2: 0},
        compiler_params=_row_dma_params(),
        name="moe_dispatch",
    )(pos, hx, jnp.zeros((n_rows, w), F32))


def _experts_kernel(lo_ref, hi_ref, used_ref, xs_ref, wgu_lo_ref, wgu_hi_ref, wd_lo_ref, wd_hi_ref, ys_ref):
    del lo_ref, hi_ref
    d = ys_ref.shape[1]

    @pl.when(used_ref[pl.program_id(0)] == 0)
    def _():
        ys_ref[...] = jnp.zeros_like(ys_ref)

    @pl.when(used_ref[pl.program_id(0)] == 1)
    def _():
        xb = xs_ref[:, :d].astype(BF16)
        route = xs_ref[:, d:]
        route_lane = lax.broadcasted_iota(jnp.int32, route.shape, 1)
        y = None
        for lane, wgu_ref, wd_ref in ((1, wgu_lo_ref, wd_lo_ref), (2, wgu_hi_ref, wd_hi_ref)):
            gu = _dot(xb, wgu_ref[0, 0].astype(BF16))
            gate, up = gu[:, :EXPERT_FF], gu[:, EXPERT_FF:]
            act = (gate / (1.0 + jnp.exp(-gate)) * up).astype(BF16)
            gate_w = jnp.sum(jnp.where(route_lane == lane, route, 0.0), axis=1, keepdims=True)
            term = gate_w * _dot(act, wd_ref[0, 0].astype(BF16))
            y = term if y is None else y + term
        ys_ref[...] = y


def _experts(xs, tile_lo, tile_hi, tile_used, w_gu, w_d, layer, tile):
    n_rows, w = xs.shape
    d = w - LANES
    gu_block, d_block = (1, 1) + w_gu.shape[2:], (1, 1) + w_d.shape[2:]
    grid_spec = pltpu.PrefetchScalarGridSpec(
        num_scalar_prefetch=3,
        grid=(n_rows // tile,),
        in_specs=[pl.BlockSpec((tile, w), lambda i, lo, hi, used: (i, 0)),
                  pl.BlockSpec(gu_block, lambda i, lo, hi, used: (layer, lo[i], 0, 0)),
                  pl.BlockSpec(gu_block, lambda i, lo, hi, used: (layer, hi[i], 0, 0)),
                  pl.BlockSpec(d_block, lambda i, lo, hi, used: (layer, lo[i], 0, 0)),
                  pl.BlockSpec(d_block, lambda i, lo, hi, used: (layer, hi[i], 0, 0))],
        out_specs=pl.BlockSpec((tile, d), lambda i, lo, hi, used: (i, 0)),
    )
    return pl.pallas_call(
        _experts_kernel,
        grid_spec=grid_spec,
        out_shape=jax.ShapeDtypeStruct((n_rows, d), F32),
        compiler_params=_cparams(1),
        name="moe_experts",
    )(tile_lo, tile_hi, tile_used, xs, w_gu, w_gu, w_d, w_d)


def _combine_kernel(pos_ref, ys_hbm, x_ref, gt_ref, g_ref, o_ref, buf, sem, *, final):
    tm = x_ref.shape[0]
    base = pl.program_id(0) * tm

    def issue(g, c):
        for u in range(8):
            r = g * 8 + u
            pltpu.make_async_copy(ys_hbm.at[pl.ds(pos_ref[base + r], 1)], buf.at[pl.ds(r, 1)], sem).start(
                priority=u % 2)
        return c

    lax.fori_loop(0, tm // 8, issue, 0)
    pltpu.make_async_copy(ys_hbm.at[pl.ds(0, tm)], buf, sem).wait()
    y = x_ref[...] + gt_ref[0] * buf[...]
    o_ref[...] = _rms(y) * g_ref[...] if final else y


def _combine(pos, ys, x, gt, g_final, seq, final):
    t, d = x.shape
    tm = ROW_TILE
    per_batch = seq // tm
    grid_spec = pltpu.PrefetchScalarGridSpec(
        num_scalar_prefetch=1,
        grid=(t // tm,),
        in_specs=[pl.BlockSpec(memory_space=pl.ANY), pl.BlockSpec((tm, d), lambda i, pos: (i, 0)),
                  pl.BlockSpec((1, 1, d), lambda i, pos: (i // per_batch, 0, 0)),
                  pl.BlockSpec((1, d), lambda i, pos: (0, 0))],
        out_specs=pl.BlockSpec((tm, d), lambda i, pos: (i, 0)),
        scratch_shapes=[pltpu.VMEM((tm, d), F32), pltpu.SemaphoreType.DMA],
    )
    return pl.pallas_call(
        functools.partial(_combine_kernel, final=final),
        grid_spec=grid_spec,
        out_shape=jax.ShapeDtypeStruct((t, d), F32),
        compiler_params=_row_dma_params(),
        name="moe_combine",
    )(pos, ys, x, gt, g_final.reshape(1, d))


def _moe(hx, w_gu, w_d, layer, x, gt, g_final, seq, final):
    t = x.shape[0]
    tile = MOE_TILE
    n_tiles = t // tile + N_CLASSES
    pos, counts = _sorted_rows(hx, tile)
    tiles_per_class = jnp.ceil(counts[:N_CLASSES] / tile).astype(jnp.int32)
    ends = jnp.cumsum(tiles_per_class)
    tile_ids = jnp.arange(n_tiles, dtype=jnp.int32)
    tile_class = jnp.minimum(jnp.searchsorted(ends, tile_ids, side="right"), N_CLASSES - 1)
    tile_used = (tile_ids < ends[-1]).astype(jnp.int32)
    lo, hi = _class_experts()
    tile_lo = jnp.asarray(lo, jnp.int32)[tile_class]
    tile_hi = jnp.asarray(hi, jnp.int32)[tile_class]
    xs = _dispatch(pos, hx, n_tiles * tile)
    ys = _experts(xs, tile_lo, tile_hi, tile_used, w_gu, w_d, layer, tile)
    return _combine(pos, ys, x, gt, g_final, seq, final)


def _diff_qk_layout(w):
    d = w.shape[0]
    w = w.reshape(d, DIFF_HEADS, 2, 2, HEAD_DIM // 2)
    return w.transpose(0, 1, 3, 2, 4).reshape(d, -1)


def _mla_layouts(w_in, w_q_up, w_kv_up):
    d = w_in.shape[0]
    half = MLA_ROPE // 2
    base = MLA_Q_RANK + MLA_KV_RANK
    r1, r2 = w_in[:, base:base + half], w_in[:, base + half:]
    z = jnp.zeros((d, 64 - 2 * half), w_in.dtype)
    w_in_l = jnp.concatenate([w_in[:, :base], r1, r1, z, r2, r2, z], axis=1)

    r = w_q_up.shape[0]
    wq = w_q_up.reshape(r, MLA_HEADS, MLA_NOPE + MLA_ROPE)
    nope = wq[:, :, :MLA_NOPE].reshape(r, -1)
    q1 = wq[:, :, MLA_NOPE:MLA_NOPE + half].reshape(r, MLA_HEADS // 2, 2 * half)
    q2 = wq[:, :, MLA_NOPE + half:].reshape(r, MLA_HEADS // 2, 2 * half)
    zq = jnp.zeros((r, MLA_HEADS // 2, 64 - 2 * half), w_q_up.dtype)
    rope = jnp.concatenate([q1, zq, q2, zq], axis=2).reshape(r, -1)
    w_q_l = jnp.concatenate([nope, rope], axis=1)

    rk = w_kv_up.shape[0]
    wkv = w_kv_up.reshape(rk, MLA_HEADS, MLA_NOPE + MLA_V)
    w_kn = wkv[:, :, :MLA_NOPE].reshape(rk, -1)
    w_v = wkv[:, :, MLA_NOPE:].reshape(rk, -1)
    return w_in_l.astype(BF16), w_q_l.T.astype(BF16), w_kn.astype(BF16), w_v.T.astype(BF16)


def _router_layout(w_group, b_group, w_router, b_router):
    d = w_group.shape[0]
    pad = LANES - N_EXPERTS - N_GROUPS
    w = jnp.concatenate([w_router, w_group, jnp.zeros((d, pad), F32)], axis=1)
    b = jnp.concatenate([b_router, b_group, jnp.zeros((pad,), F32)]).reshape(1, LANES)
    w_hi = w.astype(BF16)
    w_lo = (w - w_hi.astype(F32)).astype(BF16)
    return jnp.concatenate([w_hi, w_lo], axis=1), b


def kernel(x, c, positions, norm_mix_g, norm_ffn_g, ada_w, ada_b, sb_w_qkv, sb_w_o, mla_w_in, mla_g_q, mla_g_kv, mla_w_q_up, mla_w_kv_up, mla_w_o, diff_w_qkv, diff_lam_q1, diff_lam_k1, diff_lam_q2, diff_lam_k2, diff_g_sub, diff_w_o, moe_w_group, moe_b_group, moe_w_router, moe_b_router, moe_w_gate_up, moe_w_down, final_g):
    bsz, seq, d = x.shape
    depth = ada_w.shape[0]
    xt = x.reshape(bsz * seq, d)
    mod = _adaln(c, ada_w, ada_b)
    rope_mla = _rope_tables(positions, MLA_ROPE) if depth > 1 else None
    rope_diff = _rope_tables(positions, HEAD_DIM) if depth > 2 else None

    for i in range(depth):
        sh_m, sc_m, gt_m, sh_f, sc_f, gt_f = (mod[i, :, k * d:(k + 1) * d].reshape(bsz, 1, d) for k in range(6))
        kind, j = i % N_MIXERS, i // N_MIXERS
        if kind == 0:
            w = sb_w_qkv[j].astype(BF16)
            q_t, k, v_t = _qkv_proj(xt, norm_mix_g[i], sc_m, sh_m, w[:, :d].T, w[:, d:2 * d], w[:, 2 * d:].T, seq)
            attn = _sb_attention(q_t, k, v_t, bsz, seq)
            w_o = sb_w_o[j]
        elif kind == 1:
            w_in_l, wq_t, w_kn, wv_t = _mla_layouts(mla_w_in[j], mla_w_q_up[j], mla_w_kv_up[j])
            q_t, kn, kr, v_t = _mla_proj(xt, norm_mix_g[i], sc_m, sh_m, w_in_l, mla_g_q[j], mla_g_kv[j], wq_t, w_kn,
                                         wv_t, rope_mla, seq)
            attn = _mla_attention(q_t, kn, kr, v_t, bsz, seq)
            w_o = mla_w_o[j]
        else:
            w = diff_w_qkv[j].astype(BF16)
            q_t, k, v_t = _qkv_proj(xt, norm_mix_g[i], sc_m, sh_m, _diff_qk_layout(w[:, :d]).T,
                                    _diff_qk_layout(w[:, d:2 * d]), w[:, 2 * d:].T, seq, rope_diff)
            lam_params = jnp.stack([diff_lam_q1[j], diff_lam_k1[j], diff_lam_q2[j], diff_lam_k2[j]])
            lambda_init = 0.8 - 0.6 * math.exp(-0.3 * i)
            attn = _diff_attention(q_t, k, v_t, lam_params, diff_g_sub[j], lambda_init, bsz, seq)
            w_o = diff_w_o[j]
        w_route, b_route = _router_layout(moe_w_group[i], moe_b_group[i], moe_w_router[i], moe_b_router[i])
        xt, hx = _oproj(attn, w_o.astype(BF16), xt, gt_m, norm_ffn_g[i], sc_f, sh_f, w_route, b_route, seq)
        xt = _moe(hx, moe_w_gate_up, moe_w_down, i, xt, gt_f, final_g, seq, final=i == depth - 1)
    return xt.reshape(bsz, seq, d)
```
